```python
import math
import jax, jax.numpy as jnp
from jax import lax
import numpy as np

D_MODEL = 1024
BATCH = 8
SEQ = 8192
DEPTH = 2

CHUNK = 64
PLE_DIM = 256
N_A = max(1, DEPTH // 2)
N_B = DEPTH - N_A
SSM_WIDTH = D_MODEL
GROUP_SIZE = 16
N_GROUPS = SSM_WIDTH // GROUP_SIZE
STATE = 64
DT_MIN = 0.001
DT_MAX = 0.1
HEAD_DIM = 64
SB_WIDTH = D_MODEL
N_HEADS = SB_WIDTH // HEAD_DIM
Q_BLOCK = 128
EPS = 1e-6

kernel_name = "hybrid_s5_stickbreaking_yoco"


def rms_norm(x, g):
    xf = x.astype(jnp.float32)
    y = xf * lax.rsqrt(jnp.mean(xf * xf, axis=-1, keepdims=True) + EPS)
    return (y * g.astype(jnp.float32)).astype(x.dtype)


def _linear_recurrence_combine(left, right):
    ar1, ai1, br1, bi1 = left
    ar2, ai2, br2, bi2 = right
    return (ar2 * ar1 - ai2 * ai1,
            ar2 * ai1 + ai2 * ar1,
            ar2 * br1 - ai2 * bi1 + br2,
            ar2 * bi1 + ai2 * br1 + bi2)


def s5_scan(u, lam_re, lam_im, log_dt, b_re, b_im, c_re, c_im):
    bsz, seq, _ = u.shape
    f32 = jnp.float32
    lr = jnp.minimum(lam_re.astype(f32), -1e-4)
    li = lam_im.astype(f32)
    dt = jnp.exp(log_dt.astype(f32))[:, None]
    mag = jnp.exp(lr * dt)
    a_re = mag * jnp.cos(li * dt)
    a_im = mag * jnp.sin(li * dt)
    den = lr * lr + li * li
    nr = a_re - 1.0
    f_re = (nr * lr + a_im * li) / den
    f_im = (a_im * lr - nr * li) / den
    br = b_re.astype(f32)
    bi = b_im.astype(f32)
    bb_re = f_re[..., None] * br - f_im[..., None] * bi
    bb_im = f_re[..., None] * bi + f_im[..., None] * br
    cr = c_re.astype(f32)
    ci = c_im.astype(f32)
    n_chunks = seq // CHUNK
    uc = u.astype(f32).reshape(bsz, n_chunks, CHUNK, N_GROUPS, GROUP_SIZE).transpose(1, 0, 2, 3, 4)

    def step(carry, u_chunk):
        h_re, h_im = carry
        x_re = jnp.einsum('bcgh,gph->bcgp', u_chunk, bb_re)
        x_im = jnp.einsum('bcgh,gph->bcgp', u_chunk, bb_im)
        x_re = x_re.at[:, 0].add(a_re * h_re - a_im * h_im)
        x_im = x_im.at[:, 0].add(a_re * h_im + a_im * h_re)
        ar = jnp.broadcast_to(a_re, x_re.shape)
        ai = jnp.broadcast_to(a_im, x_im.shape)
        _, _, s_re, s_im = lax.associative_scan(_linear_recurrence_combine, (ar, ai, x_re, x_im), axis=1)
        y = jnp.einsum('bcgp,ghp->bcgh', s_re, cr) - jnp.einsum('bcgp,ghp->bcgh', s_im, ci)
        return (s_re[:, -1], s_im[:, -1]), y

    h0 = jnp.zeros((bsz, N_GROUPS, STATE), f32)
    _, ys = lax.scan(step, (h0, h0), uc)
    return ys.transpose(1, 0, 2, 3, 4).reshape(bsz, seq, N_GROUPS * GROUP_SIZE)


def s5_mixer(h, w_in, lam_re, lam_im, log_dt, b_re, b_im, c_re, c_im, d_skip, w_glu, b_glu, w_out):
    f32 = jnp.float32
    proj = h @ w_in
    u, gate = jnp.split(proj, 2, axis=-1)
    y = s5_scan(u, lam_re, lam_im, log_dt, b_re, b_im, c_re, c_im) + d_skip.astype(f32) * u.astype(f32)
    g = jax.nn.gelu(y)
    y = g * jax.nn.sigmoid(g @ w_glu.astype(f32) + b_glu.astype(f32))
    y = y.astype(h.dtype) * jax.nn.silu(gate)
    return y @ w_out


def stick_breaking_attention(q, k, v):
    f32 = jnp.float32
    seq = q.shape[2]
    scale = HEAD_DIM ** -0.5
    outs = []
    for start in range(0, seq, Q_BLOCK):
        end = start + Q_BLOCK
        qb = q[:, :, start:end].astype(f32)
        kp = k[:, :, :end].astype(f32)
        vp = v[:, :, :end].astype(f32)
        z = jnp.einsum('bhqd,bhkd->bhqk', qb, kp) * scale
        mask = jnp.arange(end)[None, :] < jnp.arange(start, end)[:, None]
        log_keep = jnp.where(mask, jax.nn.log_sigmoid(-z), 0.0)
        tail = lax.cumsum(log_keep, axis=3, reverse=True) - log_keep
        w = jnp.where(mask, jnp.exp(jax.nn.log_sigmoid(z) + tail), 0.0)
        outs.append(jnp.einsum('bhqk,bhkd->bhqd', w, vp))
    return jnp.concatenate(outs, axis=2)


def split_heads(t):
    bsz, seq, _ = t.shape
    return t.reshape(bsz, seq, N_HEADS, HEAD_DIM).transpose(0, 2, 1, 3)


def shared_kv(x, kv_norm, w_kv):
    kv = rms_norm(x, kv_norm) @ w_kv
    k, v = jnp.split(kv, 2, axis=-1)
    return split_heads(k), split_heads(v)


def stick_breaking_mixer(h, k, v, w_in, w_out):
    bsz, seq, _ = h.shape
    proj = h @ w_in
    q, gate = jnp.split(proj, 2, axis=-1)
    o = stick_breaking_attention(split_heads(q), k, v)
    o = o.transpose(0, 2, 1, 3).reshape(bsz, seq, SB_WIDTH).astype(h.dtype)
    return (o * jax.nn.silu(gate)) @ w_out


def _fwd_setup_inputs(seed: int = 0) -> dict:
    key = jax.random.key(seed)
    ks = jax.random.split(key, 24)
    f32 = jnp.float32
    nrm = lambda k, shape, s: jax.random.normal(k, shape, f32) * s
    gain = lambda k, shape: 1.0 + 0.05 * jax.random.normal(k, shape, f32)
    lam_im_base = jnp.pi * jnp.arange(STATE, dtype=f32)
    return {
        'x': jax.random.normal(ks[0], (BATCH, SEQ, D_MODEL), f32),
        'p': jax.random.normal(ks[1], (DEPTH, BATCH, SEQ, PLE_DIM), f32),
        'a_norm_pre': gain(ks[2], (N_A, D_MODEL)),
        'a_norm_post': gain(ks[3], (N_A, D_MODEL)),
        'a_w_in': nrm(ks[4], (N_A, D_MODEL, 2 * SSM_WIDTH), D_MODEL ** -0.5),
        'a_lam_re': -0.5 + nrm(ks[5], (N_A, N_GROUPS, STATE), 0.01),
        'a_lam_im': lam_im_base + nrm(ks[6], (N_A, N_GROUPS, STATE), 0.01),
        'a_log_dt': jax.random.uniform(ks[7], (N_A, N_GROUPS), f32, math.log(DT_MIN), math.log(DT_MAX)),
        'a_b_re': nrm(ks[8], (N_A, N_GROUPS, STATE, GROUP_SIZE), (2.0 * GROUP_SIZE) ** -0.5),
        'a_b_im': nrm(ks[9], (N_A, N_GROUPS, STATE, GROUP_SIZE), (2.0 * GROUP_SIZE) ** -0.5),
        'a_c_re': nrm(ks[10], (N_A, N_GROUPS, GROUP_SIZE, STATE), (2.0 * STATE) ** -0.5),
        'a_c_im': nrm(ks[11], (N_A, N_GROUPS, GROUP_SIZE, STATE), (2.0 * STATE) ** -0.5),
        'a_d_skip': nrm(ks[12], (N_A, SSM_WIDTH), 1.0),
        'a_w_glu': nrm(ks[13], (N_A, SSM_WIDTH, SSM_WIDTH), SSM_WIDTH ** -0.5),
        'a_b_glu': nrm(ks[14], (N_A, SSM_WIDTH), 0.01),
        'a_w_out': nrm(ks[15], (N_A, SSM_WIDTH, D_MODEL), SSM_WIDTH ** -0.5),
        'kv_norm': gain(ks[16], (D_MODEL,)),
        'w_kv': nrm(ks[17], (D_MODEL, 2 * SB_WIDTH), D_MODEL ** -0.5),
        'b_norm_pre': gain(ks[18], (N_B, D_MODEL)),
        'b_norm_post': gain(ks[19], (N_B, D_MODEL)),
        'b_w_in': nrm(ks[20], (N_B, D_MODEL, 2 * SB_WIDTH), D_MODEL ** -0.5),
        'b_w_out': nrm(ks[21], (N_B, SB_WIDTH, D_MODEL), SB_WIDTH ** -0.5),
        'ple_w_proj': nrm(ks[22], (DEPTH, PLE_DIM, D_MODEL), PLE_DIM ** -0.5),
        'ple_w_gate': nrm(ks[23], (DEPTH, D_MODEL, D_MODEL), D_MODEL ** -0.5),
    }


def _fwd_reference(x, p, a_norm_pre, a_norm_post, a_w_in, a_lam_re, a_lam_im, a_log_dt, a_b_re, a_b_im,
              a_c_re, a_c_im, a_d_skip, a_w_glu, a_b_glu, a_w_out, kv_norm, w_kv,
              b_norm_pre, b_norm_post, b_w_in, b_w_out, ple_w_proj, ple_w_gate):
    k = None
    v = None
    for i in range(DEPTH):
        if i < N_A:
            j = i
            h = rms_norm(x, a_norm_pre[j])
            y = s5_mixer(h, a_w_in[j], a_lam_re[j], a_lam_im[j], a_log_dt[j], a_b_re[j], a_b_im[j],
                         a_c_re[j], a_c_im[j], a_d_skip[j], a_w_glu[j], a_b_glu[j], a_w_out[j])
            x = x + rms_norm(y, a_norm_post[j])
        else:
            j = i - N_A
            h = rms_norm(x, b_norm_pre[j])
            y = stick_breaking_mixer(h, k, v, b_w_in[j], b_w_out[j])
            x = x + rms_norm(y, b_norm_post[j])
        x = x + jax.nn.sigmoid(x @ ple_w_gate[i]) * (p[i] @ ple_w_proj[i])
        if i == N_A - 1:
            k, v = shared_kv(x, kv_norm, w_kv)
    return x


import jax as _jax
import jax.numpy as _jnp

TWIN_FORMAT = 'train_step'
FWD_PARAMS = ['x', 'p', 'a_norm_pre', 'a_norm_post', 'a_w_in', 'a_lam_re', 'a_lam_im', 'a_log_dt', 'a_b_re', 'a_b_im', 'a_c_re', 'a_c_im', 'a_d_skip', 'a_w_glu', 'a_b_glu', 'a_w_out', 'kv_norm', 'w_kv', 'b_norm_pre', 'b_norm_post', 'b_w_in', 'b_w_out', 'ple_w_proj', 'ple_w_gate']
TWIN_WEIGHTS = ['a_norm_pre', 'a_norm_post', 'a_w_in', 'a_lam_re', 'a_lam_im', 'a_log_dt', 'a_b_re', 'a_b_im', 'a_c_re', 'a_c_im', 'a_d_skip', 'a_w_glu', 'a_b_glu', 'a_w_out', 'kv_norm', 'w_kv', 'b_norm_pre', 'b_norm_post', 'b_w_in', 'b_w_out', 'ple_w_proj', 'ple_w_gate']
TWIN_DIFF_INPUT = 'x'
TWIN_INPUTS = ['x', 'p', 'a_norm_pre', 'a_norm_post', 'a_w_in', 'a_lam_re', 'a_lam_im', 'a_log_dt', 'a_b_re', 'a_b_im', 'a_c_re', 'a_c_im', 'a_d_skip', 'a_w_glu', 'a_b_glu', 'a_w_out', 'kv_norm', 'w_kv', 'b_norm_pre', 'b_norm_post', 'b_w_in', 'b_w_out', 'ple_w_proj', 'ple_w_gate', 'loss_target', 'm_a_norm_pre', 'm_a_norm_post', 'm_a_w_in', 'm_a_lam_re', 'm_a_lam_im', 'm_a_log_dt', 'm_a_b_re', 'm_a_b_im', 'm_a_c_re', 'm_a_c_im', 'm_a_d_skip', 'm_a_w_glu', 'm_a_b_glu', 'm_a_w_out', 'm_kv_norm', 'm_w_kv', 'm_b_norm_pre', 'm_b_norm_post', 'm_b_w_in', 'm_b_w_out', 'm_ple_w_proj', 'm_ple_w_gate', 'v_a_norm_pre', 'v_a_norm_post', 'v_a_w_in', 'v_a_lam_re', 'v_a_lam_im', 'v_a_log_dt', 'v_a_b_re', 'v_a_b_im', 'v_a_c_re', 'v_a_c_im', 'v_a_d_skip', 'v_a_w_glu', 'v_a_b_glu', 'v_a_w_out', 'v_kv_norm', 'v_w_kv', 'v_b_norm_pre', 'v_b_norm_post', 'v_b_w_in', 'v_b_w_out', 'v_ple_w_proj', 'v_ple_w_gate']
TWIN_OUTPUTS = ['loss', 'grad_x', 'grad_a_norm_pre', 'grad_a_norm_post', 'grad_a_w_in', 'grad_a_lam_re', 'grad_a_lam_im', 'grad_a_log_dt', 'grad_a_b_re', 'grad_a_b_im', 'grad_a_c_re', 'grad_a_c_im', 'grad_a_d_skip', 'grad_a_w_glu', 'grad_a_b_glu', 'grad_a_w_out', 'grad_kv_norm', 'grad_w_kv', 'grad_b_norm_pre', 'grad_b_norm_post', 'grad_b_w_in', 'grad_b_w_out', 'grad_ple_w_proj', 'grad_ple_w_gate', 'delta_a_norm_pre', 'delta_a_norm_post', 'delta_a_w_in', 'delta_a_lam_re', 'delta_a_lam_im', 'delta_a_log_dt', 'delta_a_b_re', 'delta_a_b_im', 'delta_a_c_re', 'delta_a_c_im', 'delta_a_d_skip', 'delta_a_w_glu', 'delta_a_b_glu', 'delta_a_w_out', 'delta_kv_norm', 'delta_w_kv', 'delta_b_norm_pre', 'delta_b_norm_post', 'delta_b_w_in', 'delta_b_w_out', 'delta_ple_w_proj', 'delta_ple_w_gate', 'new_m_a_norm_pre', 'new_m_a_norm_post', 'new_m_a_w_in', 'new_m_a_lam_re', 'new_m_a_lam_im', 'new_m_a_log_dt', 'new_m_a_b_re', 'new_m_a_b_im', 'new_m_a_c_re', 'new_m_a_c_im', 'new_m_a_d_skip', 'new_m_a_w_glu', 'new_m_a_b_glu', 'new_m_a_w_out', 'new_m_kv_norm', 'new_m_w_kv', 'new_m_b_norm_pre', 'new_m_b_norm_post', 'new_m_b_w_in', 'new_m_b_w_out', 'new_m_ple_w_proj', 'new_m_ple_w_gate', 'new_v_a_norm_pre', 'new_v_a_norm_post', 'new_v_a_w_in', 'new_v_a_lam_re', 'new_v_a_lam_im', 'new_v_a_log_dt', 'new_v_a_b_re', 'new_v_a_b_im', 'new_v_a_c_re', 'new_v_a_c_im', 'new_v_a_d_skip', 'new_v_a_w_glu', 'new_v_a_b_glu', 'new_v_a_w_out', 'new_v_kv_norm', 'new_v_w_kv', 'new_v_b_norm_pre', 'new_v_b_norm_post', 'new_v_b_w_in', 'new_v_b_w_out', 'new_v_ple_w_proj', 'new_v_ple_w_gate']
TWIN_LEAF_KINDS = {'loss': 'loss', 'grad_x': 'grad_x', 'grad_a_norm_pre': 'grad_w', 'grad_a_norm_post': 'grad_w', 'grad_a_w_in': 'grad_w', 'grad_a_lam_re': 'grad_w', 'grad_a_lam_im': 'grad_w', 'grad_a_log_dt': 'grad_w', 'grad_a_b_re': 'grad_w', 'grad_a_b_im': 'grad_w', 'grad_a_c_re': 'grad_w', 'grad_a_c_im': 'grad_w', 'grad_a_d_skip': 'grad_w', 'grad_a_w_glu': 'grad_w', 'grad_a_b_glu': 'grad_w', 'grad_a_w_out': 'grad_w', 'grad_kv_norm': 'grad_w', 'grad_w_kv': 'grad_w', 'grad_b_norm_pre': 'grad_w', 'grad_b_norm_post': 'grad_w', 'grad_b_w_in': 'grad_w', 'grad_b_w_out': 'grad_w', 'grad_ple_w_proj': 'grad_w', 'grad_ple_w_gate': 'grad_w', 'delta_a_norm_pre': 'delta_w', 'delta_a_norm_post': 'delta_w', 'delta_a_w_in': 'delta_w', 'delta_a_lam_re': 'delta_w', 'delta_a_lam_im': 'delta_w', 'delta_a_log_dt': 'delta_w', 'delta_a_b_re': 'delta_w', 'delta_a_b_im': 'delta_w', 'delta_a_c_re': 'delta_w', 'delta_a_c_im': 'delta_w', 'delta_a_d_skip': 'delta_w', 'delta_a_w_glu': 'delta_w', 'delta_a_b_glu': 'delta_w', 'delta_a_w_out': 'delta_w', 'delta_kv_norm': 'delta_w', 'delta_w_kv': 'delta_w', 'delta_b_norm_pre': 'delta_w', 'delta_b_norm_post': 'delta_w', 'delta_b_w_in': 'delta_w', 'delta_b_w_out': 'delta_w', 'delta_ple_w_proj': 'delta_w', 'delta_ple_w_gate': 'delta_w', 'new_m_a_norm_pre': 'new_m', 'new_m_a_norm_post': 'new_m', 'new_m_a_w_in': 'new_m', 'new_m_a_lam_re': 'new_m', 'new_m_a_lam_im': 'new_m', 'new_m_a_log_dt': 'new_m', 'new_m_a_b_re': 'new_m', 'new_m_a_b_im': 'new_m', 'new_m_a_c_re': 'new_m', 'new_m_a_c_im': 'new_m', 'new_m_a_d_skip': 'new_m', 'new_m_a_w_glu': 'new_m', 'new_m_a_b_glu': 'new_m', 'new_m_a_w_out': 'new_m', 'new_m_kv_norm': 'new_m', 'new_m_w_kv': 'new_m', 'new_m_b_norm_pre': 'new_m', 'new_m_b_norm_post': 'new_m', 'new_m_b_w_in': 'new_m', 'new_m_b_w_out': 'new_m', 'new_m_ple_w_proj': 'new_m', 'new_m_ple_w_gate': 'new_m', 'new_v_a_norm_pre': 'new_v', 'new_v_a_norm_post': 'new_v', 'new_v_a_w_in': 'new_v', 'new_v_a_lam_re': 'new_v', 'new_v_a_lam_im': 'new_v', 'new_v_a_log_dt': 'new_v', 'new_v_a_b_re': 'new_v', 'new_v_a_b_im': 'new_v', 'new_v_a_c_re': 'new_v', 'new_v_a_c_im': 'new_v', 'new_v_a_d_skip': 'new_v', 'new_v_a_w_glu': 'new_v', 'new_v_a_b_glu': 'new_v', 'new_v_a_w_out': 'new_v', 'new_v_kv_norm': 'new_v', 'new_v_w_kv': 'new_v', 'new_v_b_norm_pre': 'new_v', 'new_v_b_norm_post': 'new_v', 'new_v_b_w_in': 'new_v', 'new_v_b_w_out': 'new_v', 'new_v_ple_w_proj': 'new_v', 'new_v_ple_w_gate': 'new_v'}


def _forward(args):
    return _fwd_reference(*[args[k] for k in FWD_PARAMS])


def _output_shape():
    def fwd():
        inp = _fwd_setup_inputs(0)
        return _fwd_reference(*[inp[k] for k in FWD_PARAMS])
    out = _jax.eval_shape(fwd)
    return out.shape, out.dtype

N_MICROBATCH = 1
ADAM_LR = 0.001
ADAM_B1 = 0.9
ADAM_B2 = 0.999
ADAM_EPS = 1e-08
ADAM_WD = 0.01
ADAM_STEP = 10
PER_EXAMPLE_BATCH_AXIS = {'x': 0, 'p': 1, 'loss_target': 0}
SHARED_INPUTS = []
_WEIGHT_DTYPES = {'a_norm_pre': _jnp.float32, 'a_norm_post': _jnp.float32, 'a_w_in': _jnp.float32, 'a_lam_re': _jnp.float32, 'a_lam_im': _jnp.float32, 'a_log_dt': _jnp.float32, 'a_b_re': _jnp.float32, 'a_b_im': _jnp.float32, 'a_c_re': _jnp.float32, 'a_c_im': _jnp.float32, 'a_d_skip': _jnp.float32, 'a_w_glu': _jnp.float32, 'a_b_glu': _jnp.float32, 'a_w_out': _jnp.float32, 'kv_norm': _jnp.float32, 'w_kv': _jnp.float32, 'b_norm_pre': _jnp.float32, 'b_norm_post': _jnp.float32, 'b_w_in': _jnp.float32, 'b_w_out': _jnp.float32, 'ple_w_proj': _jnp.float32, 'ple_w_gate': _jnp.float32}
MOMENT_SCALE = {'a_norm_pre': 1.203852e+00, 'a_norm_post': 6.642413e+01, 'a_w_in': 8.824119e-01, 'a_lam_re': 4.675948e-02, 'a_lam_im': 4.106116e-02, 'a_log_dt': 4.525259e+01, 'a_b_re': 2.752586e-02, 'a_b_im': 2.854204e-02, 'a_c_re': 5.579920e-02, 'a_c_im': 5.605527e-02, 'a_d_skip': 1.736778e+00, 'a_w_glu': 2.539523e-01, 'a_b_glu': 5.701044e-01, 'a_w_out': 1.454046e+00, 'kv_norm': 7.476637e-01, 'w_kv': 5.332202e-01, 'b_norm_pre': 6.205547e-01, 'b_norm_post': 6.490922e+01, 'b_w_in': 4.558437e-01, 'b_w_out': 7.283736e-01, 'ple_w_proj': 9.577702e-01, 'ple_w_gate': 7.212724e-01}


def _to_microbatches(a, axis):
    t = _jnp.moveaxis(a, axis, 0)
    t = t.reshape((N_MICROBATCH, t.shape[0] // N_MICROBATCH) + t.shape[1:])
    return _jnp.moveaxis(t, 1, axis + 1)


def setup_inputs(seed: int = 0) -> dict:
    inp = _fwd_setup_inputs(seed)
    key = _jax.random.fold_in(_jax.random.key(seed), 7919)
    shape, _ = _output_shape()
    out = dict(inp)
    out["loss_target"] = _jax.random.normal(_jax.random.fold_in(key, 0), shape, _jnp.float32)
    for i, name in enumerate(TWIN_WEIGHTS):
        w = inp[name].astype(_jnp.float32)
        if MOMENT_SCALE is None:
            s = _jnp.sqrt(_jnp.mean(_jnp.square(w)) + 1e-30)
        else:
            s = MOMENT_SCALE[name]
        km, kv = _jax.random.split(_jax.random.fold_in(key, i + 1))
        out[name] = w
        out["m_" + name] = s * _jax.random.normal(km, w.shape, _jnp.float32)
        out["v_" + name] = (s * s) * _jax.random.uniform(kv, w.shape, _jnp.float32, 0.5, 1.5)
    if N_MICROBATCH > 1:
        for name, axis in PER_EXAMPLE_BATCH_AXIS.items():
            out[name] = _to_microbatches(out[name], axis)
    return {'x': out['x'], 'p': out['p'], 'a_norm_pre': out['a_norm_pre'], 'a_norm_post': out['a_norm_post'], 'a_w_in': out['a_w_in'], 'a_lam_re': out['a_lam_re'], 'a_lam_im': out['a_lam_im'], 'a_log_dt': out['a_log_dt'], 'a_b_re': out['a_b_re'], 'a_b_im': out['a_b_im'], 'a_c_re': out['a_c_re'], 'a_c_im': out['a_c_im'], 'a_d_skip': out['a_d_skip'], 'a_w_glu': out['a_w_glu'], 'a_b_glu': out['a_b_glu'], 'a_w_out': out['a_w_out'], 'kv_norm': out['kv_norm'], 'w_kv': out['w_kv'], 'b_norm_pre': out['b_norm_pre'], 'b_norm_post': out['b_norm_post'], 'b_w_in': out['b_w_in'], 'b_w_out': out['b_w_out'], 'ple_w_proj': out['ple_w_proj'], 'ple_w_gate': out['ple_w_gate'], 'loss_target': out['loss_target'], 'm_a_norm_pre': out['m_a_norm_pre'], 'm_a_norm_post': out['m_a_norm_post'], 'm_a_w_in': out['m_a_w_in'], 'm_a_lam_re': out['m_a_lam_re'], 'm_a_lam_im': out['m_a_lam_im'], 'm_a_log_dt': out['m_a_log_dt'], 'm_a_b_re': out['m_a_b_re'], 'm_a_b_im': out['m_a_b_im'], 'm_a_c_re': out['m_a_c_re'], 'm_a_c_im': out['m_a_c_im'], 'm_a_d_skip': out['m_a_d_skip'], 'm_a_w_glu': out['m_a_w_glu'], 'm_a_b_glu': out['m_a_b_glu'], 'm_a_w_out': out['m_a_w_out'], 'm_kv_norm': out['m_kv_norm'], 'm_w_kv': out['m_w_kv'], 'm_b_norm_pre': out['m_b_norm_pre'], 'm_b_norm_post': out['m_b_norm_post'], 'm_b_w_in': out['m_b_w_in'], 'm_b_w_out': out['m_b_w_out'], 'm_ple_w_proj': out['m_ple_w_proj'], 'm_ple_w_gate': out['m_ple_w_gate'], 'v_a_norm_pre': out['v_a_norm_pre'], 'v_a_norm_post': out['v_a_norm_post'], 'v_a_w_in': out['v_a_w_in'], 'v_a_lam_re': out['v_a_lam_re'], 'v_a_lam_im': out['v_a_lam_im'], 'v_a_log_dt': out['v_a_log_dt'], 'v_a_b_re': out['v_a_b_re'], 'v_a_b_im': out['v_a_b_im'], 'v_a_c_re': out['v_a_c_re'], 'v_a_c_im': out['v_a_c_im'], 'v_a_d_skip': out['v_a_d_skip'], 'v_a_w_glu': out['v_a_w_glu'], 'v_a_b_glu': out['v_a_b_glu'], 'v_a_w_out': out['v_a_w_out'], 'v_kv_norm': out['v_kv_norm'], 'v_w_kv': out['v_w_kv'], 'v_b_norm_pre': out['v_b_norm_pre'], 'v_b_norm_post': out['v_b_norm_post'], 'v_b_w_in': out['v_b_w_in'], 'v_b_w_out': out['v_b_w_out'], 'v_ple_w_proj': out['v_ple_w_proj'], 'v_ple_w_gate': out['v_ple_w_gate']}


def _loss(weights, diff, rest, loss_target):
    with _jax.named_scope("forward"):
        args = {**rest, TWIN_DIFF_INPUT: diff, **{k: w.astype(_WEIGHT_DTYPES[k]) for k, w in weights.items()}}
        y = _forward(args)
    with _jax.named_scope("loss_head"):
        err = _jnp.square(y.astype(_jnp.float32) - loss_target)
        return 0.5 * _jnp.sum(_jnp.mean(err, axis=-1)) if err.ndim else 0.5 * err


def _adamw(w, g, m, v):
    m = ADAM_B1 * m + (1.0 - ADAM_B1) * g
    v = ADAM_B2 * v + (1.0 - ADAM_B2) * _jnp.square(g)
    m_hat = m / (1.0 - ADAM_B1 ** ADAM_STEP)
    v_hat = v / (1.0 - ADAM_B2 ** ADAM_STEP)
    delta = -ADAM_LR * (m_hat / (_jnp.sqrt(v_hat) + ADAM_EPS) + ADAM_WD * w)
    return delta, m, v


def reference(x, p, a_norm_pre, a_norm_post, a_w_in, a_lam_re, a_lam_im, a_log_dt, a_b_re, a_b_im, a_c_re, a_c_im, a_d_skip, a_w_glu, a_b_glu, a_w_out, kv_norm, w_kv, b_norm_pre, b_norm_post, b_w_in, b_w_out, ple_w_proj, ple_w_gate, loss_target, m_a_norm_pre, m_a_norm_post, m_a_w_in, m_a_lam_re, m_a_lam_im, m_a_log_dt, m_a_b_re, m_a_b_im, m_a_c_re, m_a_c_im, m_a_d_skip, m_a_w_glu, m_a_b_glu, m_a_w_out, m_kv_norm, m_w_kv, m_b_norm_pre, m_b_norm_post, m_b_w_in, m_b_w_out, m_ple_w_proj, m_ple_w_gate, v_a_norm_pre, v_a_norm_post, v_a_w_in, v_a_lam_re, v_a_lam_im, v_a_log_dt, v_a_b_re, v_a_b_im, v_a_c_re, v_a_c_im, v_a_d_skip, v_a_w_glu, v_a_b_glu, v_a_w_out, v_kv_norm, v_w_kv, v_b_norm_pre, v_b_norm_post, v_b_w_in, v_b_w_out, v_ple_w_proj, v_ple_w_gate):
    given = dict(x=x, p=p, a_norm_pre=a_norm_pre, a_norm_post=a_norm_post, a_w_in=a_w_in, a_lam_re=a_lam_re, a_lam_im=a_lam_im, a_log_dt=a_log_dt, a_b_re=a_b_re, a_b_im=a_b_im, a_c_re=a_c_re, a_c_im=a_c_im, a_d_skip=a_d_skip, a_w_glu=a_w_glu, a_b_glu=a_b_glu, a_w_out=a_w_out, kv_norm=kv_norm, w_kv=w_kv, b_norm_pre=b_norm_pre, b_norm_post=b_norm_post, b_w_in=b_w_in, b_w_out=b_w_out, ple_w_proj=ple_w_proj, ple_w_gate=ple_w_gate, loss_target=loss_target, m_a_norm_pre=m_a_norm_pre, m_a_norm_post=m_a_norm_post, m_a_w_in=m_a_w_in, m_a_lam_re=m_a_lam_re, m_a_lam_im=m_a_lam_im, m_a_log_dt=m_a_log_dt, m_a_b_re=m_a_b_re, m_a_b_im=m_a_b_im, m_a_c_re=m_a_c_re, m_a_c_im=m_a_c_im, m_a_d_skip=m_a_d_skip, m_a_w_glu=m_a_w_glu, m_a_b_glu=m_a_b_glu, m_a_w_out=m_a_w_out, m_kv_norm=m_kv_norm, m_w_kv=m_w_kv, m_b_norm_pre=m_b_norm_pre, m_b_norm_post=m_b_norm_post, m_b_w_in=m_b_w_in, m_b_w_out=m_b_w_out, m_ple_w_proj=m_ple_w_proj, m_ple_w_gate=m_ple_w_gate, v_a_norm_pre=v_a_norm_pre, v_a_norm_post=v_a_norm_post, v_a_w_in=v_a_w_in, v_a_lam_re=v_a_lam_re, v_a_lam_im=v_a_lam_im, v_a_log_dt=v_a_log_dt, v_a_b_re=v_a_b_re, v_a_b_im=v_a_b_im, v_a_c_re=v_a_c_re, v_a_c_im=v_a_c_im, v_a_d_skip=v_a_d_skip, v_a_w_glu=v_a_w_glu, v_a_b_glu=v_a_b_glu, v_a_w_out=v_a_w_out, v_kv_norm=v_kv_norm, v_w_kv=v_w_kv, v_b_norm_pre=v_b_norm_pre, v_b_norm_post=v_b_norm_post, v_b_w_in=v_b_w_in, v_b_w_out=v_b_w_out, v_ple_w_proj=v_ple_w_proj, v_ple_w_gate=v_ple_w_gate)
    weights = {n: given[n] for n in TWIN_WEIGHTS}
    shared = {n: given[n] for n in SHARED_INPUTS}
    per_example = {n: given[n] for n in ['x', 'p']}
    grad_fn = _jax.value_and_grad(_loss, argnums=(0, 1))

    def one_microbatch(ex, loss_target):
        ex = dict(ex)
        diff = ex.pop(TWIN_DIFF_INPUT)
        return grad_fn(weights, diff, {**shared, **ex}, loss_target)

    if N_MICROBATCH == 1:
        loss, (grad_w, grad_x) = one_microbatch(per_example, given["loss_target"])
    else:
        def body(carry, xs):
            loss_sum, grad_sum = carry
            l_k, (gw_k, gx_k) = one_microbatch(xs[0], xs[1])
            with _jax.named_scope("update"):
                return (loss_sum + l_k, _jax.tree.map(_jnp.add, grad_sum, gw_k)), gx_k

        init = (_jnp.zeros((), _jnp.float32), _jax.tree.map(_jnp.zeros_like, weights))
        (loss, grad_w), grad_x = _jax.lax.scan(body, init, (per_example, given["loss_target"]))
    with _jax.named_scope("update"):
        delta_w, new_m, new_v = {}, {}, {}
        for n in TWIN_WEIGHTS:
            delta_w[n], new_m[n], new_v[n] = _adamw(weights[n], grad_w[n], given["m_" + n], given["v_" + n])
    return (loss, grad_x, *[grad_w[n] for n in TWIN_WEIGHTS], *[delta_w[n] for n in TWIN_WEIGHTS],
            *[new_m[n] for n in TWIN_WEIGHTS], *[new_v[n] for n in TWIN_WEIGHTS])
```

```python
import functools
import math

import jax
import jax.numpy as jnp
from jax import lax
from jax.experimental import pallas as pl
from jax.experimental.pallas import tpu as pltpu

F32 = jnp.float32
MXU_DTYPE = jnp.bfloat16

D_MODEL = 1024
N_GROUPS = 64
GROUP_SIZE = 16
STATE = 64
GROUPS_PER_BLOCK = 8
SSM_LANES = GROUPS_PER_BLOCK * STATE
N_GROUP_BLOCKS = N_GROUPS // GROUPS_PER_BLOCK
HEAD_DIM = 64
HEADS_PER_BLOCK = 2
EPS = 1e-6
SLAB_W = 512
SUBLANES = 8
VMEM_LIMIT = 56 * 1024 * 1024

ADAM_LR = 0.001
ADAM_B1 = 0.9
ADAM_B2 = 0.999
ADAM_EPS = 1e-08
ADAM_WD = 0.01
ADAM_STEP = 10

MESH = pl.DeviceIdType.MESH
ANY = pl.BlockSpec(memory_space=pl.ANY)


def _cparams(sem):
    return pltpu.CompilerParams(dimension_semantics=sem, vmem_limit_bytes=VMEM_LIMIT)


def _mm(a, b, *, ta=False, tb=False, add=None, out_dtype=F32, name, tm=512, tn=512, tk=512):
    if ta:
        kdim, m = a.shape
    else:
        m, kdim = a.shape
    if tb:
        n, kb = b.shape
    else:
        kb, n = b.shape
    assert kdim == kb, (a.shape, b.shape, ta, tb)
    tm, tn, tk = min(tm, m), min(tn, n), min(tk, kdim)
    assert m % tm == 0 and n % tn == 0 and kdim % tk == 0
    nk = kdim // tk
    dims = (((0 if ta else 1,), (1 if tb else 0,)), ((), ()))
    has_add = add is not None

    def body(*refs):
        if has_add:
            a_ref, b_ref, add_ref, o_ref, acc_ref = refs
        else:
            a_ref, b_ref, o_ref, acc_ref = refs
        k = pl.program_id(2)

        @pl.when(k == 0)
        def _():
            acc_ref[...] = jnp.zeros_like(acc_ref)

        acc_ref[...] += lax.dot_general(a_ref[...].astype(MXU_DTYPE), b_ref[...].astype(MXU_DTYPE), dims,
                                        preferred_element_type=F32)

        @pl.when(k == nk - 1)
        def _():
            r = acc_ref[...]
            if has_add:
                r = r + add_ref[...].astype(F32)
            o_ref[...] = r.astype(o_ref.dtype)

    a_spec = pl.BlockSpec((tk, tm), lambda i, j, k: (k, i)) if ta else pl.BlockSpec((tm, tk), lambda i, j, k: (i, k))
    b_spec = pl.BlockSpec((tn, tk), lambda i, j, k: (j, k)) if tb else pl.BlockSpec((tk, tn), lambda i, j, k: (k, j))
    o_spec = pl.BlockSpec((tm, tn), lambda i, j, k: (i, j))
    in_specs = [a_spec, b_spec] + ([o_spec] if has_add else [])
    args = (a, b) + ((add,) if has_add else ())
    return pl.pallas_call(
        body, name=name, grid=(m // tm, n // tn, nk), in_specs=in_specs, out_specs=o_spec,
        out_shape=jax.ShapeDtypeStruct((m, n), out_dtype), scratch_shapes=[pltpu.VMEM((tm, tn), F32)],
        compiler_params=_cparams(("parallel", "parallel", "arbitrary")),
    )(*args)


def _ew(fn, rows, params, row_dtypes, n_acc, *, name, tr=256):
    nrows = rows[0].shape[0]
    tr = min(tr, nrows)
    assert nrows % tr == 0
    n_in, n_par, n_row = len(rows), len(params), len(row_dtypes)
    tile_avals = [jax.ShapeDtypeStruct((tr, r.shape[1]), F32) for r in rows]
    par_avals = [jax.ShapeDtypeStruct(q.shape, F32) for q in params]
    out_avals = jax.eval_shape(fn, *tile_avals, *par_avals)
    assert len(out_avals) == n_row + n_acc

    def body(*refs):
        ins = [r[...].astype(F32) for r in refs[:n_in + n_par]]
        outs = fn(*ins)
        row_refs = refs[n_in + n_par:n_in + n_par + n_row]
        acc_refs = refs[n_in + n_par + n_row:]
        for r, o in zip(row_refs, outs[:n_row]):
            r[...] = o.astype(r.dtype)
        if n_acc:
            @pl.when(pl.program_id(0) == 0)
            def _():
                for r in acc_refs:
                    r[...] = jnp.zeros_like(r)
            for r, o in zip(acc_refs, outs[n_row:]):
                r[...] += o

    in_specs = [pl.BlockSpec((tr, r.shape[1]), lambda i: (i, 0)) for r in rows]
    in_specs += [pl.BlockSpec(q.shape, lambda i: (0, 0)) for q in params]
    out_specs = [pl.BlockSpec((tr, o.shape[1]), lambda i: (i, 0)) for o in out_avals[:n_row]]
    out_specs += [pl.BlockSpec(o.shape, lambda i: (0, 0)) for o in out_avals[n_row:]]
    out_shape = [jax.ShapeDtypeStruct((nrows, o.shape[1]), dt) for o, dt in zip(out_avals[:n_row], row_dtypes)]
    out_shape += [jax.ShapeDtypeStruct(o.shape, F32) for o in out_avals[n_row:]]
    return pl.pallas_call(
        body, name=name, grid=(nrows // tr,), in_specs=in_specs, out_specs=out_specs, out_shape=out_shape,
        compiler_params=_cparams(("arbitrary",)),
    )(*rows, *params)


def _colsum(v):
    return jnp.sum(v, axis=0, keepdims=True)


def _rms(xt, g):
    r = lax.rsqrt(jnp.mean(xt * xt, axis=-1, keepdims=True) + EPS)
    return xt * r * g


def _gelu_skip(ys, u, dskip):
    return jax.nn.gelu(ys + dskip * u)


def _glu_gate(g, t, gate, bglu):
    return g * jax.nn.sigmoid(t + bglu) * jax.nn.silu(gate)


def _ogate(o, gate):
    return o * jax.nn.silu(gate)


def _ple(xt, gt, pp):
    return xt + jax.nn.sigmoid(gt) * pp


def _s5_disc(lam_re, lam_im, log_dt, bt_re, bt_im):
    lr = jnp.minimum(lam_re, -1e-4)
    li = lam_im
    dt = jnp.exp(log_dt)
    mag = jnp.exp(lr * dt)
    a_re = mag * jnp.cos(li * dt)
    a_im = mag * jnp.sin(li * dt)
    den = lr * lr + li * li
    nr = a_re - 1.0
    f_re = (nr * lr + a_im * li) / den
    f_im = (a_im * lr - nr * li) / den
    bb_re = f_re[:, None, :] * bt_re - f_im[:, None, :] * bt_im
    bb_im = f_re[:, None, :] * bt_im + f_im[:, None, :] * bt_re
    return a_re, a_im, bb_re, bb_im


def _s5_params_fwd(lam_re, lam_im, log_dt, bt_re, bt_im):
    def body(lr_ref, li_ref, dt_ref, br_ref, bi_ref, ar_ref, ai_ref, bbr_ref, bbi_ref, pr_ref, pi_ref):
        a_re, a_im, bb_re, bb_im = _s5_disc(lr_ref[...], li_ref[...], dt_ref[...], br_ref[...], bi_ref[...])
        ar_ref[...] = a_re
        ai_ref[...] = a_im
        bbr_ref[...] = bb_re
        bbi_ref[...] = bb_im
        pr, pi = a_re, a_im
        for t in range(SUBLANES):
            pr_ref[t] = pr
            pi_ref[t] = pi
            pr, pi = pr * a_re - pi * a_im, pr * a_im + pi * a_re

    g, s = lam_re.shape
    sd = jax.ShapeDtypeStruct
    return pl.pallas_call(
        body, name="s5_params_fwd",
        out_shape=[sd((g, s), F32), sd((g, s), F32), sd(bt_re.shape, F32), sd(bt_re.shape, F32),
                   sd((SUBLANES, g, s), F32), sd((SUBLANES, g, s), F32)],
    )(lam_re, lam_im, log_dt, bt_re, bt_im)


def _s5_params_bwd(lam_re, lam_im, log_dt, bt_re, bt_im, d_ar, d_ai, d_bbr, d_bbi):
    def body(lr_ref, li_ref, dt_ref, br_ref, bi_ref, c0, c1, c2, c3, o0, o1, o2, o3, o4):
        _, vjp = jax.vjp(_s5_disc, lr_ref[...], li_ref[...], dt_ref[...], br_ref[...], bi_ref[...])
        grads = vjp((jnp.sum(c0[...], axis=0), jnp.sum(c1[...], axis=0), c2[...], c3[...]))
        for o, gval in zip((o0, o1, o2, o3, o4), grads):
            o[...] = gval

    sd = jax.ShapeDtypeStruct
    return pl.pallas_call(
        body, name="s5_params_bwd",
        out_shape=[sd(lam_re.shape, F32), sd(lam_im.shape, F32), sd(log_dt.shape, F32), sd(bt_re.shape, F32),
                   sd(bt_im.shape, F32)],
    )(lam_re, lam_im, log_dt, bt_re, bt_im, d_ar, d_ai, d_bbr, d_bbi)


def _block_diag(t):
    nb, g, r, c = t.shape
    eye = jnp.eye(g, dtype=t.dtype)
    return jnp.einsum("ngrc,gh->ngrhc", t, eye).reshape(nb, g * r, g * c)


def _block_diag_extract(m, r, c):
    nb = m.shape[0]
    g = GROUPS_PER_BLOCK
    m = m.reshape(nb, g, r, g, c)
    idx = jnp.arange(g)
    return m[:, idx, :, idx, :].transpose(1, 0, 2, 3)


def _per_sublane(acc):
    return acc.reshape(N_GROUP_BLOCKS, SUBLANES, GROUPS_PER_BLOCK, STATE).transpose(1, 0, 2, 3).reshape(
        SUBLANES, N_GROUPS, STATE)


def _scan_tables(pr, pi):
    row = jnp.arange(SUBLANES)[:, None, None]

    def tables(im_sign, fwd):
        kinds = []
        for k in (1, 2, 4):
            mask = (row >= k) if fwd else (row < SUBLANES - k)
            kinds.append(jnp.where(mask, pr[k - 1][None], 0.0))
            kinds.append(jnp.where(mask, im_sign * pi[k - 1][None], 0.0))
        if fwd:
            kinds += [pr, im_sign * pi]
        else:
            kinds += [pr[::-1], im_sign * pi[::-1]]
        return jnp.stack(kinds, axis=0).transpose(2, 0, 1, 3)

    return tables(1.0, True), tables(-1.0, False)


def _scan_rows(buf_ref, tab_ref, carry_r, carry_i, n_tiles, reverse):
    w = SSM_LANES

    def tile(j, carry):
        cr, ci = carry
        jj = (n_tiles - 1 - j) if reverse else j
        rows = pl.ds(pl.multiple_of(jj * SUBLANES, SUBLANES), SUBLANES)
        zr = buf_ref[rows, 0:w]
        zi = buf_ref[rows, w:2 * w]
        for n, k in enumerate((1, 2, 4)):
            ar = tab_ref[2 * n]
            ai = tab_ref[2 * n + 1]
            sh = (SUBLANES - k) if reverse else k
            rr = pltpu.roll(zr, sh, 0)
            ri = pltpu.roll(zi, sh, 0)
            zr, zi = zr + ar * rr - ai * ri, zi + ar * ri + ai * rr
        pr = tab_ref[6]
        pi = tab_ref[7]
        zr, zi = zr + pr * cr - pi * ci, zi + pr * ci + pi * cr
        buf_ref[rows, 0:w] = zr
        buf_ref[rows, w:2 * w] = zi
        if reverse:
            return zr[0:1], zi[0:1]
        return zr[SUBLANES - 1:SUBLANES], zi[SUBLANES - 1:SUBLANES]

    return lax.fori_loop(0, n_tiles, tile, (carry_r, carry_i))


def _s5_fwd(u, bbd, cbd, tab_f, *, tt=512):
    seq = u.shape[0]
    tt = min(tt, seq)
    nt = seq // tt
    w = SSM_LANES

    def body(u_ref, b_ref, c_ref, tab_ref, y_ref, hs_ref, buf_ref, h_ref):
        it = pl.program_id(1)

        @pl.when(it == 0)
        def _():
            h_ref[...] = jnp.zeros_like(h_ref)

        hs_ref[...] = h_ref[...]
        buf_ref[...] = jnp.dot(u_ref[...].astype(MXU_DTYPE), b_ref[...], preferred_element_type=F32)
        hr, hi = _scan_rows(buf_ref, tab_ref, h_ref[:, 0:w], h_ref[:, w:2 * w], tt // SUBLANES, False)
        h_ref[:, 0:w] = hr
        h_ref[:, w:2 * w] = hi
        y_ref[...] = jnp.dot(buf_ref[...].astype(MXU_DTYPE), c_ref[...], preferred_element_type=F32)

    return pl.pallas_call(
        body, name="s5_fwd", grid=(N_GROUP_BLOCKS, nt),
        in_specs=[pl.BlockSpec((tt, 128), lambda g, t: (t, g)),
                  pl.BlockSpec((None, 128, 2 * w), lambda g, t: (g, 0, 0)),
                  pl.BlockSpec((None, 2 * w, 128), lambda g, t: (g, 0, 0)),
                  pl.BlockSpec((None, 8, SUBLANES, w), lambda g, t: (g, 0, 0, 0))],
        out_specs=[pl.BlockSpec((tt, 128), lambda g, t: (t, g)),
                   pl.BlockSpec((None, None, 1, 2 * w), lambda g, t: (g, t, 0, 0))],
        out_shape=[jax.ShapeDtypeStruct((seq, D_MODEL), F32),
                   jax.ShapeDtypeStruct((N_GROUP_BLOCKS, nt, 1, 2 * w), F32)],
        scratch_shapes=[pltpu.VMEM((tt, 2 * w), F32), pltpu.VMEM((1, 2 * w), F32)],
        compiler_params=_cparams(("parallel", "arbitrary")),
    )(u, bbd, cbd, tab_f)


def _s5_bwd(u, dy, du_add, hs, bbd, cbd_t, tab_f, tab_b, *, tt=512):
    seq = u.shape[0]
    tt = min(tt, seq)
    nt = seq // tt
    w = SSM_LANES
    n_tiles = tt // SUBLANES

    def body(u_ref, dy_ref, dua_ref, hs_ref, b_ref, ct_ref, tabf_ref, tabb_ref,
             du_ref, db_ref, dc_ref, dar_ref, dai_ref, s_ref, l_ref, lam_ref):
        it = pl.program_id(1)

        @pl.when(it == 0)
        def _():
            lam_ref[...] = jnp.zeros_like(lam_ref)
            db_ref[...] = jnp.zeros_like(db_ref)
            dc_ref[...] = jnp.zeros_like(dc_ref)
            dar_ref[...] = jnp.zeros_like(dar_ref)
            dai_ref[...] = jnp.zeros_like(dai_ref)

        ub = u_ref[...].astype(MXU_DTYPE)
        dyb = dy_ref[...].astype(MXU_DTYPE)
        s_ref[...] = jnp.dot(ub, b_ref[...], preferred_element_type=F32)
        h_in_r = hs_ref[:, 0:w]
        h_in_i = hs_ref[:, w:2 * w]
        _scan_rows(s_ref, tabf_ref, h_in_r, h_in_i, n_tiles, False)
        dc_ref[...] += lax.dot_general(s_ref[...].astype(MXU_DTYPE), dyb, (((0,), (0,)), ((), ())),
                                       preferred_element_type=F32)
        l_ref[...] = jnp.dot(dyb, ct_ref[...], preferred_element_type=F32)
        lr, li = _scan_rows(l_ref, tabb_ref, lam_ref[:, 0:w], lam_ref[:, w:2 * w], n_tiles, True)
        lam_ref[:, 0:w] = lr
        lam_ref[:, w:2 * w] = li
        lb = l_ref[...].astype(MXU_DTYPE)
        db_ref[...] += lax.dot_general(ub, lb, (((0,), (0,)), ((), ())), preferred_element_type=F32)
        du = lax.dot_general(lb, b_ref[...], (((1,), (1,)), ((), ())), preferred_element_type=F32)
        du_ref[...] = (du + dua_ref[...].astype(F32)).astype(du_ref.dtype)

        def tile(j, carry):
            pr, pi, accr, acci = carry
            rows = pl.ds(pl.multiple_of(j * SUBLANES, SUBLANES), SUBLANES)
            sr = s_ref[rows, 0:w]
            si = s_ref[rows, w:2 * w]
            first = lax.broadcasted_iota(jnp.int32, (SUBLANES, w), 0) == 0
            sr_prev = jnp.where(first, pr, pltpu.roll(sr, 1, 0))
            si_prev = jnp.where(first, pi, pltpu.roll(si, 1, 0))
            gr = l_ref[rows, 0:w]
            gi = l_ref[rows, w:2 * w]
            accr = accr + gr * sr_prev + gi * si_prev
            acci = acci + gi * sr_prev - gr * si_prev
            return sr[SUBLANES - 1:SUBLANES], si[SUBLANES - 1:SUBLANES], accr, acci

        zero = jnp.zeros((SUBLANES, w), F32)
        _, _, accr, acci = lax.fori_loop(0, n_tiles, tile, (h_in_r, h_in_i, zero, zero))
        dar_ref[...] += accr
        dai_ref[...] += acci

    rev = lambda g, t: (nt - 1 - t, g)
    return pl.pallas_call(
        body, name="s5_bwd", grid=(N_GROUP_BLOCKS, nt),
        in_specs=[pl.BlockSpec((tt, 128), rev), pl.BlockSpec((tt, 128), rev), pl.BlockSpec((tt, 128), rev),
                  pl.BlockSpec((None, None, 1, 2 * w), lambda g, t: (g, nt - 1 - t, 0, 0)),
                  pl.BlockSpec((None, 128, 2 * w), lambda g, t: (g, 0, 0)),
                  pl.BlockSpec((None, 128, 2 * w), lambda g, t: (g, 0, 0)),
                  pl.BlockSpec((None, 8, SUBLANES, w), lambda g, t: (g, 0, 0, 0)),
                  pl.BlockSpec((None, 8, SUBLANES, w), lambda g, t: (g, 0, 0, 0))],
        out_specs=[pl.BlockSpec((tt, 128), rev),
                   pl.BlockSpec((None, 128, 2 * w), lambda g, t: (g, 0, 0)),
                   pl.BlockSpec((None, 2 * w, 128), lambda g, t: (g, 0, 0)),
                   pl.BlockSpec((None, SUBLANES, w), lambda g, t: (g, 0, 0)),
                   pl.BlockSpec((None, SUBLANES, w), lambda g, t: (g, 0, 0))],
        out_shape=[jax.ShapeDtypeStruct((seq, D_MODEL), MXU_DTYPE),
                   jax.ShapeDtypeStruct((N_GROUP_BLOCKS, 128, 2 * w), F32),
                   jax.ShapeDtypeStruct((N_GROUP_BLOCKS, 2 * w, 128), F32),
                   jax.ShapeDtypeStruct((N_GROUP_BLOCKS, SUBLANES, w), F32),
                   jax.ShapeDtypeStruct((N_GROUP_BLOCKS, SUBLANES, w), F32)],
        scratch_shapes=[pltpu.VMEM((tt, 2 * w), F32), pltpu.VMEM((tt, 2 * w), F32), pltpu.VMEM((1, 2 * w), F32)],
        compiler_params=_cparams(("parallel", "arbitrary")),
    )(u, dy, du_add, hs, bbd, cbd_t, tab_f, tab_b)


def _split_hi_lo(v):
    hi = v.astype(MXU_DTYPE)
    lo = (v - hi.astype(F32)).astype(MXU_DTYPE)
    return hi, lo


def _neg_softplus(z):
    e = jnp.exp(-jnp.abs(z))
    return -(jnp.maximum(z, 0.0) + jnp.log(1.0 + e)), e


def _head_masks():
    lane = lax.broadcasted_iota(jnp.int32, (1, HEADS_PER_BLOCK * HEAD_DIM), 1)
    return [(lane >= h * HEAD_DIM) & (lane < (h + 1) * HEAD_DIM) for h in range(HEADS_PER_BLOCK)]


def _sba_fwd(q, k, v, *, tb=256):
    seq = q.shape[0]
    tb = min(tb, seq)
    nq = seq // tb
    cw = HEADS_PER_BLOCK * HEAD_DIM
    scale = HEAD_DIM ** -0.5

    def body(q_ref, k_ref, v_ref, o_ref, t_ref, acc_ref, r_ref):
        qi = pl.program_id(1)
        masks = _head_masks()
        qf = q_ref[...].astype(F32) * scale
        qs = [jnp.where(m, qf, 0.0).astype(MXU_DTYPE) for m in masks]
        row = lax.broadcasted_iota(jnp.int32, (tb, tb), 0)
        col = lax.broadcasted_iota(jnp.int32, (tb, tb), 1)
        upper = jnp.where(row >= col, 1.0, 0.0).astype(MXU_DTYPE)
        causal = col < row
        acc_ref[...] = jnp.zeros_like(acc_ref)
        r_ref[...] = jnp.zeros_like(r_ref)

        def block(kj, diag):
            ks = pl.ds(pl.multiple_of(kj * tb, tb), tb)
            kb = k_ref[ks, :].astype(MXU_DTYPE)
            vb = v_ref[ks, :]
            for h in range(HEADS_PER_BLOCK):
                z = lax.dot_general(qs[h], kb, (((1,), (1,)), ((), ())), preferred_element_type=F32)
                lk, _ = _neg_softplus(z)
                if diag:
                    lk = jnp.where(causal, lk, 0.0)
                hi, lo = _split_hi_lo(lk)
                incl = jnp.dot(hi, upper, preferred_element_type=F32) + jnp.dot(lo, upper, preferred_element_type=F32)
                a = z + incl + r_ref[h]
                if diag:
                    a = jnp.where(causal, a, -1e30)
                wgt = jnp.exp(a).astype(MXU_DTYPE)
                vh = jnp.where(masks[h], vb, jnp.zeros_like(vb)).astype(MXU_DTYPE)
                acc_ref[...] += jnp.dot(wgt, vh, preferred_element_type=F32)
                r_ref[h] += incl[:, 0:1]

        block(qi, True)

        def step(i, c):
            block(qi - 1 - i, False)
            return c

        lax.fori_loop(0, qi, step, 0)
        o_ref[...] = acc_ref[...]
        for h in range(HEADS_PER_BLOCK):
            t_ref[:, h:h + 1] = r_ref[h]

    n_hp = D_MODEL // cw
    return pl.pallas_call(
        body, name="sba_fwd", grid=(n_hp, nq),
        in_specs=[pl.BlockSpec((tb, cw), lambda h, i: (i, h)),
                  pl.BlockSpec((seq, cw), lambda h, i: (0, h)),
                  pl.BlockSpec((seq, cw), lambda h, i: (0, h))],
        out_specs=[pl.BlockSpec((tb, cw), lambda h, i: (i, h)),
                   pl.BlockSpec((None, tb, HEADS_PER_BLOCK), lambda h, i: (h, i, 0))],
        out_shape=[jax.ShapeDtypeStruct((seq, D_MODEL), F32),
                   jax.ShapeDtypeStruct((n_hp, seq, HEADS_PER_BLOCK), F32)],
        scratch_shapes=[pltpu.VMEM((tb, cw), F32), pltpu.VMEM((HEADS_PER_BLOCK, tb, 1), F32)],
        compiler_params=_cparams(("parallel", "arbitrary")),
    )(q, k, v)


def _sba_bwd(q, k, v, do, tot, *, tb=256):
    seq = q.shape[0]
    tb = min(tb, seq)
    nq = seq // tb
    cw = HEADS_PER_BLOCK * HEAD_DIM
    scale = HEAD_DIM ** -0.5

    def body(q_ref, k_ref, v_ref, do_ref, t_ref, dq_ref, dk_ref, dv_ref, acc_ref, lp_ref, dp_ref):
        qi = pl.program_id(1)

        @pl.when(qi == 0)
        def _():
            dk_ref[...] = jnp.zeros_like(dk_ref)
            dv_ref[...] = jnp.zeros_like(dv_ref)

        masks = _head_masks()
        qf = q_ref[...].astype(F32) * scale
        qs = [jnp.where(m, qf, 0.0).astype(MXU_DTYPE) for m in masks]
        dof = do_ref[...].astype(F32)
        dos = [jnp.where(m, dof, 0.0).astype(MXU_DTYPE) for m in masks]
        row = lax.broadcasted_iota(jnp.int32, (tb, tb), 0)
        col = lax.broadcasted_iota(jnp.int32, (tb, tb), 1)
        upper = jnp.where(row >= col, 1.0, 0.0).astype(MXU_DTYPE)
        lower = jnp.where(row <= col, 1.0, 0.0).astype(MXU_DTYPE)
        causal = col < row
        acc_ref[...] = jnp.zeros_like(acc_ref)
        lp_ref[...] = jnp.zeros_like(lp_ref)
        dp_ref[...] = jnp.zeros_like(dp_ref)

        def block(kj, diag):
            ks = pl.ds(pl.multiple_of(kj * tb, tb), tb)
            kb = k_ref[ks, :]
            vb = v_ref[ks, :]
            for h in range(HEADS_PER_BLOCK):
                kh = jnp.where(masks[h], kb, jnp.zeros_like(kb)).astype(MXU_DTYPE)
                vh = jnp.where(masks[h], vb, jnp.zeros_like(vb)).astype(MXU_DTYPE)
                z = lax.dot_general(qs[h], kh, (((1,), (1,)), ((), ())), preferred_element_type=F32)
                lk, e = _neg_softplus(z)
                if diag:
                    lk = jnp.where(causal, lk, 0.0)
                hi, lo = _split_hi_lo(lk)
                incl = jnp.dot(hi, upper, preferred_element_type=F32) + jnp.dot(lo, upper, preferred_element_type=F32)
                lp_next = lp_ref[h] + incl[:, 0:1]
                lp_ref[h] = lp_next
                a = z + incl + (t_ref[:, h:h + 1] - lp_next)
                if diag:
                    a = jnp.where(causal, a, -1e30)
                wgt = jnp.exp(a)
                dw = lax.dot_general(dos[h], vh, (((1,), (1,)), ((), ())), preferred_element_type=F32)
                da = dw * wgt
                hi, lo = _split_hi_lo(da)
                pre = jnp.dot(hi, lower, preferred_element_type=F32) + jnp.dot(lo, lower, preferred_element_type=F32)
                pre = pre + dp_ref[h]
                dp_ref[h] = pre[:, tb - 1:tb]
                inv = 1.0 / (1.0 + e)
                sig = jnp.where(z >= 0, inv, e * inv)
                dz = da - pre * sig
                if diag:
                    dz = jnp.where(causal, dz, 0.0)
                dzb = dz.astype(MXU_DTYPE)
                acc_ref[...] += jnp.dot(dzb, kh, preferred_element_type=F32)
                dk_ref[ks, :] += lax.dot_general(dzb, qs[h], (((0,), (0,)), ((), ())), preferred_element_type=F32)
                dv_ref[ks, :] += lax.dot_general(wgt.astype(MXU_DTYPE), dos[h], (((0,), (0,)), ((), ())),
                                                 preferred_element_type=F32)

        def step(i, c):
            block(i, False)
            return c

        lax.fori_loop(0, qi, step, 0)
        block(qi, True)
        dq_ref[...] = acc_ref[...] * scale

    n_hp = D_MODEL // cw
    qspec = pl.BlockSpec((tb, cw), lambda h, i: (i, h))
    full = pl.BlockSpec((seq, cw), lambda h, i: (0, h))
    return pl.pallas_call(
        body, name="sba_bwd", grid=(n_hp, nq),
        in_specs=[qspec, full, full, qspec, pl.BlockSpec((None, tb, HEADS_PER_BLOCK), lambda h, i: (h, i, 0))],
        out_specs=[qspec, full, full],
        out_shape=[jax.ShapeDtypeStruct((seq, D_MODEL), F32)] * 3,
        scratch_shapes=[pltpu.VMEM((tb, cw), F32), pltpu.VMEM((HEADS_PER_BLOCK, tb, 1), F32),
                        pltpu.VMEM((HEADS_PER_BLOCK, tb, 1), F32)],
        compiler_params=_cparams(("parallel", "arbitrary")),
    )(q, k, v, do, tot)


def _my_place():
    return lax.axis_index("x"), lax.axis_index("y"), lax.axis_index("c")


def _other_chips(x, y):
    return [(1 - x, y), (x, 1 - y), (1 - x, 1 - y)]


def _all_gather_chips(slabs, *, name):
    n = len(slabs)

    def body(*refs):
        ins, outs = refs[:n], refs[n:2 * n]
        send_sems, recv_sems, loc_sems = refs[2 * n:]
        x, y, c = _my_place()
        me = 2 * x + y
        local = [pltpu.make_async_copy(ins[i], outs[i].at[me], loc_sems.at[i]) for i in range(n)]
        for cp in local:
            cp.start()
        remote = []
        for j, (px, py) in enumerate(_other_chips(x, y)):
            for i in range(n):
                remote.append(pltpu.make_async_remote_copy(
                    src_ref=ins[i], dst_ref=outs[i].at[me], send_sem=send_sems.at[j * n + i],
                    recv_sem=recv_sems.at[j * n + i], device_id=(px, py, c), device_id_type=MESH))
        for cp in remote:
            cp.start()
        for cp in remote:
            cp.wait_recv()
        for cp in remote:
            cp.wait_send()
        for cp in local:
            cp.wait()

    return pl.pallas_call(
        body, name=name, in_specs=[ANY] * n, out_specs=[ANY] * n,
        out_shape=[jax.ShapeDtypeStruct((4,) + s.shape, s.dtype) for s in slabs],
        scratch_shapes=[pltpu.SemaphoreType.DMA((3 * n,)), pltpu.SemaphoreType.DMA((3 * n,)),
                        pltpu.SemaphoreType.DMA((n,))],
    )(*slabs)


def _sibling_exchange(send, *, name):
    def body(s_ref, o_ref, send_sem, recv_sem):
        x, y, c = _my_place()
        cp = pltpu.make_async_remote_copy(src_ref=s_ref, dst_ref=o_ref, send_sem=send_sem, recv_sem=recv_sem,
                                          device_id=(x, y, 1 - c), device_id_type=MESH)
        cp.start()
        cp.wait_recv()
        cp.wait_send()

    return pl.pallas_call(
        body, name=name, in_specs=[ANY], out_specs=ANY,
        out_shape=jax.ShapeDtypeStruct(send.shape, send.dtype),
        scratch_shapes=[pltpu.SemaphoreType.DMA, pltpu.SemaphoreType.DMA],
    )(send)


def _all_to_all_chips(parts, *, name):
    def body(p_ref, o_ref, send_sems, recv_sems, loc_sem):
        x, y, c = _my_place()
        me = 2 * x + y
        local = pltpu.make_async_copy(p_ref.at[me], o_ref.at[me], loc_sem)
        local.start()
        remote = []
        for j, (px, py) in enumerate(_other_chips(x, y)):
            remote.append(pltpu.make_async_remote_copy(
                src_ref=p_ref.at[2 * px + py], dst_ref=o_ref.at[me], send_sem=send_sems.at[j],
                recv_sem=recv_sems.at[j], device_id=(px, py, c), device_id_type=MESH))
        for cp in remote:
            cp.start()
        for cp in remote:
            cp.wait_recv()
        for cp in remote:
            cp.wait_send()
        local.wait()

    return pl.pallas_call(
        body, name=name, in_specs=[ANY], out_specs=ANY,
        out_shape=jax.ShapeDtypeStruct(parts.shape, parts.dtype),
        scratch_shapes=[pltpu.SemaphoreType.DMA((3,)), pltpu.SemaphoreType.DMA((3,)), pltpu.SemaphoreType.DMA],
    )(parts)


_SHARDED = [("a_norm_pre", 1), ("a_norm_post", 1), ("a_d_skip", 1), ("a_b_glu", 1), ("a_w_in", 2), ("a_w_glu", 1),
            ("a_w_out", 1), ("w_kv", 1), ("b_w_in", 2), ("b_w_out", 1), ("ple_w_proj", 2), ("ple_w_gate", 1)]
_REPLICATED = ["a_lam_re", "a_lam_im", "a_log_dt", "a_b_re", "a_b_im", "a_c_re", "a_c_im", "kv_norm", "b_norm_pre",
               "b_norm_post"]
_WEIGHT_ORDER = ["a_norm_pre", "a_norm_post", "a_w_in", "a_lam_re", "a_lam_im", "a_log_dt", "a_b_re", "a_b_im",
                 "a_c_re", "a_c_im", "a_d_skip", "a_w_glu", "a_b_glu", "a_w_out", "kv_norm", "w_kv", "b_norm_pre",
                 "b_norm_post", "b_w_in", "b_w_out", "ple_w_proj", "ple_w_gate"]


def _round_up(n, m):
    return (n + m - 1) // m * m


def _pack(pieces, rows):
    flat = jnp.concatenate([p.reshape(-1).astype(F32) for p in pieces])
    flat = jnp.pad(flat, (0, rows * SLAB_W - flat.shape[0]))
    return flat.reshape(rows, SLAB_W)


def _unpack(slab, shapes):
    flat = slab.reshape(-1)
    out, off = [], 0
    for s in shapes:
        n = math.prod(s)
        out.append(flat[off:off + n].reshape(s))
        off += n
    return out


def _pick_tile(n, cap):
    return max(t for t in range(SUBLANES, cap + 1, SUBLANES) if n % t == 0)


def _size_rows(shapes, mult):
    return _round_up(_round_up(sum(math.prod(s) for s in shapes), SLAB_W) // SLAB_W, mult)


def _adamw_tile(w, g, m, v):
    m2 = ADAM_B1 * m + (1.0 - ADAM_B1) * g
    v2 = ADAM_B2 * v + (1.0 - ADAM_B2) * (g * g)
    m_hat = m2 / (1.0 - ADAM_B1 ** ADAM_STEP)
    v_hat = v2 / (1.0 - ADAM_B2 ** ADAM_STEP)
    delta = -ADAM_LR * (m_hat / (jnp.sqrt(v_hat) + ADAM_EPS) + ADAM_WD * w)
    return delta, m2, v2


def _forward_backward(x0, p0, p1, tgt, W):
    seq = x0.shape[0]
    sw = D_MODEL
    g = {}

    h1, = _ew(lambda xt, gn: (_rms(xt, gn),), [x0], [W["a_norm_pre"]], [MXU_DTYPE], 0, name="a_pre_norm")
    w_u, w_ga = W["a_w_in"][:, :sw], W["a_w_in"][:, sw:]
    u = _mm(h1, w_u, name="a_in_u")
    gate = _mm(h1, w_ga, name="a_in_gate")
    a_re, a_im, bbt_re, bbt_im, pw_re, pw_im = _s5_params_fwd(W["lam_re"], W["lam_im"], W["log_dt"], W["bt_re"],
                                                               W["bt_im"])
    nb, gpb = N_GROUP_BLOCKS, GROUPS_PER_BLOCK
    bbd = jnp.concatenate([_block_diag(bbt_re.reshape(nb, gpb, GROUP_SIZE, STATE)),
                           _block_diag(bbt_im.reshape(nb, gpb, GROUP_SIZE, STATE))], axis=2).astype(MXU_DTYPE)
    crt = W["c_re"].transpose(0, 2, 1).reshape(nb, gpb, STATE, GROUP_SIZE)
    cit = W["c_im"].transpose(0, 2, 1).reshape(nb, gpb, STATE, GROUP_SIZE)
    cbd = jnp.concatenate([_block_diag(crt), -_block_diag(cit)], axis=1).astype(MXU_DTYPE)
    cbd_t = cbd.transpose(0, 2, 1)
    tab_f, tab_b = _scan_tables(pw_re.reshape(SUBLANES, nb, SSM_LANES), pw_im.reshape(SUBLANES, nb, SSM_LANES))
    ys, hs = _s5_fwd(u, bbd, cbd, tab_f)
    gl, = _ew(lambda a, b, d: (_gelu_skip(a, b, d),), [ys, u], [W["a_d_skip"]], [F32], 0, name="a_gelu")
    t = _mm(gl, W["a_w_glu"], name="a_glu")
    y3, = _ew(lambda a, b, c, d: (_glu_gate(a, b, c, d),), [gl, t, gate], [W["a_b_glu"]], [MXU_DTYPE], 0, name="a_gate")
    y4 = _mm(y3, W["a_w_out"], name="a_out")
    x1, = _ew(lambda a, b, gn: (a + _rms(b, gn),), [x0, y4], [W["a_norm_post"]], [F32], 0, name="a_post_norm")
    gt0 = _mm(x1, W["ple_w_gate"][0], name="ple0_gate")
    pp0 = _mm(p0, W["ple_w_proj"][0], name="ple0_proj")
    x2, hk, h2 = _ew(lambda a, b, c, g1, g2: (_ple(a, b, c), _rms(_ple(a, b, c), g1), _rms(_ple(a, b, c), g2)),
                     [x1, gt0, pp0], [W["kv_norm"], W["b_norm_pre"]], [F32, MXU_DTYPE, MXU_DTYPE], 0, name="ple0_mix")

    w_k, w_v = W["w_kv"][:, :sw], W["w_kv"][:, sw:]
    k = _mm(hk, w_k, out_dtype=MXU_DTYPE, name="kv_k")
    v = _mm(hk, w_v, out_dtype=MXU_DTYPE, name="kv_v")
    w_q, w_gb = W["b_w_in"][:, :sw], W["b_w_in"][:, sw:]
    q = _mm(h2, w_q, out_dtype=MXU_DTYPE, name="b_in_q")
    gate2 = _mm(h2, w_gb, name="b_in_gate")
    o, tot = _sba_fwd(q, k, v)
    y5in, = _ew(lambda a, b: (_ogate(a, b),), [o, gate2], [], [MXU_DTYPE], 0, name="b_gate")
    y5 = _mm(y5in, W["b_w_out"], name="b_out")
    x3, = _ew(lambda a, b, gn: (a + _rms(b, gn),), [x2, y5], [W["b_norm_post"]], [F32], 0, name="b_post_norm")
    gt1 = _mm(x3, W["ple_w_gate"][1], name="ple1_gate")
    pp1 = _mm(p1, W["ple_w_proj"][1], name="ple1_proj")

    def loss_fn(xt, gt, pp, tg):
        s = jax.nn.sigmoid(gt)
        d = (xt + s * pp - tg) * (1.0 / D_MODEL)
        lsum = jnp.sum(d * d, axis=(0, 1), keepdims=True) * (0.5 * D_MODEL)
        return d, d * pp * s * (1.0 - s), d * s, lsum

    dx3a, dgt1, dpp1, loss = _ew(loss_fn, [x3, gt1, pp1, tgt], [], [F32, MXU_DTYPE, MXU_DTYPE], 1, name="loss_head")
    g_gate1 = _mm(x3, dgt1, ta=True, name="ple1_gate_dw")
    g_proj1 = _mm(p1, dpp1, ta=True, name="ple1_proj_dw")
    dx3 = _mm(dgt1, W["ple_w_gate"][1], tb=True, add=dx3a, name="ple1_gate_dx")

    def post_norm_bwd(ct, yt, gn):
        _, vjp = jax.vjp(_rms, yt, gn)
        dy, dg = vjp(ct)
        return dy, dg

    dy5, g["b_norm_post"] = _ew(post_norm_bwd, [dx3, y5], [W["b_norm_post"]], [MXU_DTYPE], 1, name="b_post_norm_bwd")
    g["b_w_out"] = _mm(y5in, dy5, ta=True, name="b_out_dw")
    dy5in = _mm(dy5, W["b_w_out"], tb=True, name="b_out_dx")

    def ogate_bwd(ct, ot, gt):
        _, vjp = jax.vjp(_ogate, ot, gt)
        return vjp(ct)

    do, dgate2 = _ew(ogate_bwd, [dy5in, o, gate2], [], [MXU_DTYPE, MXU_DTYPE], 0, name="b_gate_bwd")
    dq, dk, dv = _sba_bwd(q, k, v, do, tot)
    g["b_w_in"] = jnp.concatenate([_mm(h2, dq, ta=True, name="b_in_q_dw"),
                                   _mm(h2, dgate2, ta=True, name="b_in_gate_dw")], axis=1)
    dh2 = _mm(dgate2, w_gb, tb=True, add=_mm(dq, w_q, tb=True, name="b_in_q_dx"), name="b_in_gate_dx")
    g["w_kv"] = jnp.concatenate([_mm(hk, dk, ta=True, name="kv_k_dw"), _mm(hk, dv, ta=True, name="kv_v_dw")], axis=1)
    dhk = _mm(dv, w_v, tb=True, add=_mm(dk, w_k, tb=True, name="kv_k_dx"), name="kv_v_dx")

    def mix_bwd(ct, c2, ck, xt, gt, pp, g1, g2):
        x2v, vjp_ple = jax.vjp(_ple, xt, gt, pp)
        _, vjp_k = jax.vjp(_rms, x2v, g1)
        _, vjp_b = jax.vjp(_rms, x2v, g2)
        dxk, dg1 = vjp_k(ck)
        dxb, dg2 = vjp_b(c2)
        dx2 = ct + dxk + dxb
        dx1, dgt, dpp = vjp_ple(dx2)
        return dx1, dgt, dpp, dg1, dg2

    dx1a, dgt0, dpp0, g["kv_norm"], g["b_norm_pre"] = _ew(
        mix_bwd, [dx3, dh2, dhk, x1, gt0, pp0], [W["kv_norm"], W["b_norm_pre"]], [F32, MXU_DTYPE, MXU_DTYPE], 2,
        name="ple0_mix_bwd")
    g_gate0 = _mm(x1, dgt0, ta=True, name="ple0_gate_dw")
    g_proj0 = _mm(p0, dpp0, ta=True, name="ple0_proj_dw")
    dx1 = _mm(dgt0, W["ple_w_gate"][0], tb=True, add=dx1a, name="ple0_gate_dx")
    g["ple_w_gate"] = jnp.stack([g_gate0, g_gate1])
    g["ple_w_proj"] = jnp.stack([g_proj0, g_proj1])

    dy4, g["a_norm_post"] = _ew(post_norm_bwd, [dx1, y4], [W["a_norm_post"]], [MXU_DTYPE], 1, name="a_post_norm_bwd")
    g["a_w_out"] = _mm(y3, dy4, ta=True, name="a_out_dw")
    dy3 = _mm(dy4, W["a_w_out"], tb=True, name="a_out_dx")

    def gate_bwd(ct, gt_, tt_, gat, bg):
        _, vjp = jax.vjp(_glu_gate, gt_, tt_, gat, bg)
        dg_, dt_, dgate_, dbg = vjp(ct)
        return dg_, dt_, dgate_, dbg

    dgl_a, dt, dgate, g["a_b_glu"] = _ew(gate_bwd, [dy3, gl, t, gate], [W["a_b_glu"]], [F32, MXU_DTYPE, MXU_DTYPE], 1,
                                         name="a_gate_bwd")
    g["a_w_glu"] = _mm(gl, dt, ta=True, name="a_glu_dw")
    dgl = _mm(dt, W["a_w_glu"], tb=True, add=dgl_a, name="a_glu_dx")

    def gelu_bwd(ct, yt, ut, ds):
        _, vjp = jax.vjp(_gelu_skip, yt, ut, ds)
        return vjp(ct)

    dys, du_a, g["a_d_skip"] = _ew(gelu_bwd, [dgl, ys, u], [W["a_d_skip"]], [MXU_DTYPE, F32], 1, name="a_gelu_bwd")
    du, d_bbd, d_cbd, d_ar8, d_ai8 = _s5_bwd(u, dys, du_a, hs, bbd, cbd_t, tab_f, tab_b)
    g["a_w_in"] = jnp.concatenate([_mm(h1, du, ta=True, name="a_in_u_dw"),
                                   _mm(h1, dgate, ta=True, name="a_in_gate_dw")], axis=1)
    dh1 = _mm(dgate, w_ga, tb=True, add=_mm(du, w_u, tb=True, name="a_in_u_dx"), name="a_in_gate_dx")

    def pre_norm_bwd(ct, ch, xt, gn):
        _, vjp = jax.vjp(_rms, xt, gn)
        dx, dg = vjp(ch)
        return ct + dx, dg

    dx0, g["a_norm_pre"] = _ew(pre_norm_bwd, [dx1, dh1, x0], [W["a_norm_pre"]], [F32], 1, name="a_pre_norm_bwd")

    w = SSM_LANES
    d_bbt_re = _block_diag_extract(d_bbd[:, :, :w], GROUP_SIZE, STATE).reshape(N_GROUPS, GROUP_SIZE, STATE)
    d_bbt_im = _block_diag_extract(d_bbd[:, :, w:], GROUP_SIZE, STATE).reshape(N_GROUPS, GROUP_SIZE, STATE)
    d_crt = _block_diag_extract(d_cbd[:, :w, :], STATE, GROUP_SIZE).reshape(N_GROUPS, STATE, GROUP_SIZE)
    d_cit = -_block_diag_extract(d_cbd[:, w:, :], STATE, GROUP_SIZE).reshape(N_GROUPS, STATE, GROUP_SIZE)
    g["a_c_re"] = d_crt.transpose(0, 2, 1)[None]
    g["a_c_im"] = d_cit.transpose(0, 2, 1)[None]
    d_lr, d_li, d_ldt, d_btr, d_bti = _s5_params_bwd(
        W["lam_re"], W["lam_im"], W["log_dt"], W["bt_re"], W["bt_im"], _per_sublane(d_ar8), _per_sublane(d_ai8),
        d_bbt_re, d_bbt_im)
    g["a_lam_re"], g["a_lam_im"] = d_lr[None], d_li[None]
    g["a_log_dt"] = d_ldt.reshape(1, N_GROUPS)
    g["a_b_re"] = d_btr.transpose(0, 2, 1)[None]
    g["a_b_im"] = d_bti.transpose(0, 2, 1)[None]
    return loss, dx0, g


def kernel(x, p, a_norm_pre, a_norm_post, a_w_in, a_lam_re, a_lam_im, a_log_dt, a_b_re, a_b_im, a_c_re, a_c_im, a_d_skip, a_w_glu, a_b_glu, a_w_out, kv_norm, w_kv, b_norm_pre, b_norm_post, b_w_in, b_w_out, ple_w_proj, ple_w_gate, loss_target, m_a_norm_pre, m_a_norm_post, m_a_w_in, m_a_lam_re, m_a_lam_im, m_a_log_dt, m_a_b_re, m_a_b_im, m_a_c_re, m_a_c_im, m_a_d_skip, m_a_w_glu, m_a_b_glu, m_a_w_out, m_kv_norm, m_w_kv, m_b_norm_pre, m_b_norm_post, m_b_w_in, m_b_w_out, m_ple_w_proj, m_ple_w_gate, v_a_norm_pre, v_a_norm_post, v_a_w_in, v_a_lam_re, v_a_lam_im, v_a_log_dt, v_a_b_re, v_a_b_im, v_a_c_re, v_a_c_im, v_a_d_skip, v_a_w_glu, v_a_b_glu, v_a_w_out, v_kv_norm, v_w_kv, v_b_norm_pre, v_b_norm_post, v_b_w_in, v_b_w_out, v_ple_w_proj, v_ple_w_gate):
    loc = dict(a_norm_pre=a_norm_pre, a_norm_post=a_norm_post, a_w_in=a_w_in, a_lam_re=a_lam_re, a_lam_im=a_lam_im,
               a_log_dt=a_log_dt, a_b_re=a_b_re, a_b_im=a_b_im, a_c_re=a_c_re, a_c_im=a_c_im, a_d_skip=a_d_skip,
               a_w_glu=a_w_glu, a_b_glu=a_b_glu, a_w_out=a_w_out, kv_norm=kv_norm, w_kv=w_kv, b_norm_pre=b_norm_pre,
               b_norm_post=b_norm_post, b_w_in=b_w_in, b_w_out=b_w_out, ple_w_proj=ple_w_proj, ple_w_gate=ple_w_gate)
    mom = dict(a_norm_pre=m_a_norm_pre, a_norm_post=m_a_norm_post, a_w_in=m_a_w_in, a_lam_re=m_a_lam_re,
               a_lam_im=m_a_lam_im, a_log_dt=m_a_log_dt, a_b_re=m_a_b_re, a_b_im=m_a_b_im, a_c_re=m_a_c_re,
               a_c_im=m_a_c_im, a_d_skip=m_a_d_skip, a_w_glu=m_a_w_glu, a_b_glu=m_a_b_glu, a_w_out=m_a_w_out,
               kv_norm=m_kv_norm, w_kv=m_w_kv, b_norm_pre=m_b_norm_pre, b_norm_post=m_b_norm_post, b_w_in=m_b_w_in,
               b_w_out=m_b_w_out, ple_w_proj=m_ple_w_proj, ple_w_gate=m_ple_w_gate)
    var = dict(a_norm_pre=v_a_norm_pre, a_norm_post=v_a_norm_post, a_w_in=v_a_w_in, a_lam_re=v_a_lam_re,
               a_lam_im=v_a_lam_im, a_log_dt=v_a_log_dt, a_b_re=v_a_b_re, a_b_im=v_a_b_im, a_c_re=v_a_c_re,
               a_c_im=v_a_c_im, a_d_skip=v_a_d_skip, a_w_glu=v_a_w_glu, a_b_glu=v_a_b_glu, a_w_out=v_a_w_out,
               kv_norm=v_kv_norm, w_kv=v_w_kv, b_norm_pre=v_b_norm_pre, b_norm_post=v_b_norm_post, b_w_in=v_b_w_in,
               b_w_out=v_b_w_out, ple_w_proj=v_ple_w_proj, ple_w_gate=v_ple_w_gate)
    ci = lax.axis_index("c")

    vec_names = [n for n, _ in _SHARDED if loc[n].shape[-2] == 1 and loc[n].ndim == 2]
    mat_names = [n for n, _ in _SHARDED if n not in vec_names]
    axis_of = dict(_SHARDED)
    mat_shapes = [loc[n].shape for n in mat_names]
    vec_shapes = [loc[n].shape for n in vec_names]
    mat_rows = _size_rows(mat_shapes, 16)
    vec_rows = _size_rows(vec_shapes, 8)
    mat_slab = _pack([loc[n] for n in mat_names], mat_rows).astype(MXU_DTYPE)
    vec_slab = _pack([loc[n] for n in vec_names], vec_rows)
    mat_all, vec_all = _all_gather_chips([mat_slab, vec_slab], name="gather_weights")
    W = {}
    for names, slab_all, shapes in ((mat_names, mat_all, mat_shapes), (vec_names, vec_all, vec_shapes)):
        per_chip = [_unpack(slab_all[j], shapes) for j in range(4)]
        for i, n in enumerate(names):
            W[n] = jnp.concatenate([per_chip[j][i] for j in range(4)], axis=axis_of[n])
    for n in ("a_w_in", "a_w_glu", "a_w_out", "b_w_in", "b_w_out"):
        W[n] = W[n][0]
    W["kv_norm"] = kv_norm.reshape(1, D_MODEL)
    W["b_norm_pre"], W["b_norm_post"] = b_norm_pre, b_norm_post
    W["lam_re"], W["lam_im"] = a_lam_re[0], a_lam_im[0]
    W["log_dt"] = a_log_dt.reshape(N_GROUPS, 1)
    W["bt_re"], W["bt_im"] = a_b_re[0].transpose(0, 2, 1), a_b_im[0].transpose(0, 2, 1)
    W["c_re"], W["c_im"] = a_c_re[0], a_c_im[0]

    loss, dx0, g = _forward_backward(x[0], p[0, 0], p[1, 0], loss_target[0], W)
    for n in ("a_w_in", "a_w_glu", "a_w_out", "b_w_in", "b_w_out"):
        g[n] = g[n][None]
    g["kv_norm"] = g["kv_norm"].reshape(D_MODEL)
    loss = lax.psum(loss[0, 0], ("x", "y", "c"))

    rep_shapes = [loc[n].shape for n in _REPLICATED]
    rep_rows = _size_rows(rep_shapes, 32)
    rep_slab = _pack([g[n] for n in _REPLICATED], rep_rows).reshape(4, rep_rows // 4, SLAB_W)
    sh_names = [n for n, _ in _SHARDED]
    sh_shapes = [loc[n].shape for n in sh_names]
    sh_rows = _size_rows(sh_shapes, 8)
    rows = _round_up(sh_rows + rep_rows // 4, 1024)
    half = rows // 2
    slabs = []
    for j in range(4):
        pieces = [jnp.split(g[n], 4, axis=axis_of[n])[j] for n in sh_names]
        sh = _pack(pieces, sh_rows)
        slabs.append(jnp.concatenate([sh, rep_slab[j], jnp.zeros((rows - sh_rows - rep_rows // 4, SLAB_W), F32)]))
    gs = jnp.stack(slabs).reshape(4, 2, half, SLAB_W)
    mine = lax.dynamic_index_in_dim(gs, ci, axis=1, keepdims=False).reshape(4 * half, SLAB_W)
    theirs = lax.dynamic_index_in_dim(gs, 1 - ci, axis=1, keepdims=False).reshape(4 * half, SLAB_W)
    got = _sibling_exchange(theirs, name="reduce_cores")
    part, = _ew(lambda a, b: (a + b,), [mine, got], [], [F32], 0, name="reduce_cores_add", tr=512)
    recv = _all_to_all_chips(part.reshape(4, half, SLAB_W), name="reduce_chips")
    tot_half, = _ew(lambda a, b, c, d: (((a + b) + c) + d,), [recv[0], recv[1], recv[2], recv[3]], [], [F32], 0,
                    name="reduce_chips_add", tr=512)
    other_half = _sibling_exchange(tot_half, name="share_cores")
    lo_half = jnp.where(ci == 0, tot_half, other_half)
    hi_half = jnp.where(ci == 0, other_half, tot_half)
    gsum = jnp.concatenate([lo_half, hi_half])
    rep_q = gsum[sh_rows:sh_rows + rep_rows // 4]
    rep_all, = _all_gather_chips([rep_q], name="gather_replicated")
    gsum_full = jnp.concatenate([gsum[:sh_rows], rep_all.reshape(rep_rows, SLAB_W)])

    all_names = sh_names + _REPLICATED
    all_shapes = sh_shapes + rep_shapes

    def slab_of(d):
        return jnp.concatenate([_pack([d[n] for n in sh_names], sh_rows), _pack([d[n] for n in _REPLICATED], rep_rows)])

    n_rows = sh_rows + rep_rows
    outs = _ew(_adamw_tile, [slab_of(loc), gsum_full, slab_of(mom), slab_of(var)], [], [F32] * 3, 0, name="adamw",
               tr=_pick_tile(n_rows, 512))
    res = []
    for slab in [gsum_full] + list(outs):
        parts = _unpack(slab[:sh_rows], sh_shapes) + _unpack(slab[sh_rows:], rep_shapes)
        res.append(dict(zip(all_names, parts)))
    out = [loss, dx0[None]]
    for d in res:
        out += [d[n] for n in _WEIGHT_ORDER]
    return tuple(out)
```

```python
import functools
import math

import jax
import jax.numpy as jnp
from jax import lax
from jax.experimental import pallas as pl
from jax.experimental.pallas import tpu as pltpu

F32 = jnp.float32
MXU_DTYPE = jnp.bfloat16

D_MODEL = 1024
N_GROUPS = 64
GROUP_SIZE = 16
STATE = 64
GROUPS_PER_BLOCK = 8
SSM_LANES = GROUPS_PER_BLOCK * STATE
N_GROUP_BLOCKS = N_GROUPS // GROUPS_PER_BLOCK
HEAD_DIM = 64
HEADS_PER_BLOCK = 2
EPS = 1e-6
SLAB_W = 512
SUBLANES = 8
VMEM_LIMIT = 56 * 1024 * 1024

ADAM_LR = 0.001
ADAM_B1 = 0.9
ADAM_B2 = 0.999
ADAM_EPS = 1e-08
ADAM_WD = 0.01
ADAM_STEP = 10

MESH = pl.DeviceIdType.MESH
ANY = pl.BlockSpec(memory_space=pl.ANY)


def _cparams(sem):
    return pltpu.CompilerParams(dimension_semantics=sem, vmem_limit_bytes=VMEM_LIMIT)


def _mm(a, b, *, ta=False, tb=False, add=None, out_dtype=F32, name, tm=1024, tn=1024, tk=1024):
    if ta:
        kdim, m = a.shape
    else:
        m, kdim = a.shape
    if tb:
        n, kb = b.shape
    else:
        kb, n = b.shape
    assert kdim == kb, (a.shape, b.shape, ta, tb)
    tm, tn, tk = min(tm, m), min(tn, n), min(tk, kdim)
    assert m % tm == 0 and n % tn == 0 and kdim % tk == 0
    nk = kdim // tk
    dims = (((0 if ta else 1,), (1 if tb else 0,)), ((), ()))
    has_add = add is not None

    def body(*refs):
        if has_add:
            a_ref, b_ref, add_ref, o_ref, acc_ref = refs
        else:
            a_ref, b_ref, o_ref, acc_ref = refs
        k = pl.program_id(2)

        @pl.when(k == 0)
        def _():
            acc_ref[...] = jnp.zeros_like(acc_ref)

        acc_ref[...] += lax.dot_general(a_ref[...].astype(MXU_DTYPE), b_ref[...].astype(MXU_DTYPE), dims,
                                        preferred_element_type=F32)

        @pl.when(k == nk - 1)
        def _():
            r = acc_ref[...]
            if has_add:
                r = r + add_ref[...].astype(F32)
            o_ref[...] = r.astype(o_ref.dtype)

    a_spec = pl.BlockSpec((tk, tm), lambda i, j, k: (k, i)) if ta else pl.BlockSpec((tm, tk), lambda i, j, k: (i, k))
    b_spec = pl.BlockSpec((tn, tk), lambda i, j, k: (j, k)) if tb else pl.BlockSpec((tk, tn), lambda i, j, k: (k, j))
    o_spec = pl.BlockSpec((tm, tn), lambda i, j, k: (i, j))
    in_specs = [a_spec, b_spec] + ([o_spec] if has_add else [])
    args = (a, b) + ((add,) if has_add else ())
    return pl.pallas_call(
        body, name=name, grid=(m // tm, n // tn, nk), in_specs=in_specs, out_specs=o_spec,
        out_shape=jax.ShapeDtypeStruct((m, n), out_dtype), scratch_shapes=[pltpu.VMEM((tm, tn), F32)],
        compiler_params=_cparams(("parallel", "parallel", "arbitrary")),
    )(*args)


def _ew(fn, rows, params, row_dtypes, n_acc, *, name, tr=256):
    nrows = rows[0].shape[0]
    tr = min(tr, nrows)
    assert nrows % tr == 0
    n_in, n_par, n_row = len(rows), len(params), len(row_dtypes)
    tile_avals = [jax.ShapeDtypeStruct((tr, r.shape[1]), F32) for r in rows]
    par_avals = [jax.ShapeDtypeStruct(q.shape, F32) for q in params]
    out_avals = jax.eval_shape(fn, *tile_avals, *par_avals)
    assert len(out_avals) == n_row + n_acc

    def body(*refs):
        ins = [r[...].astype(F32) for r in refs[:n_in + n_par]]
        outs = fn(*ins)
        row_refs = refs[n_in + n_par:n_in + n_par + n_row]
        acc_refs = refs[n_in + n_par + n_row:]
        for r, o in zip(row_refs, outs[:n_row]):
            r[...] = o.astype(r.dtype)
        if n_acc:
            @pl.when(pl.program_id(0) == 0)
            def _():
                for r in acc_refs:
                    r[...] = jnp.zeros_like(r)
            for r, o in zip(acc_refs, outs[n_row:]):
                r[...] += o

    in_specs = [pl.BlockSpec((tr, r.shape[1]), lambda i: (i, 0)) for r in rows]
    in_specs += [pl.BlockSpec(q.shape, lambda i: (0, 0)) for q in params]
    out_specs = [pl.BlockSpec((tr, o.shape[1]), lambda i: (i, 0)) for o in out_avals[:n_row]]
    out_specs += [pl.BlockSpec(o.shape, lambda i: (0, 0)) for o in out_avals[n_row:]]
    out_shape = [jax.ShapeDtypeStruct((nrows, o.shape[1]), dt) for o, dt in zip(out_avals[:n_row], row_dtypes)]
    out_shape += [jax.ShapeDtypeStruct(o.shape, F32) for o in out_avals[n_row:]]
    return pl.pallas_call(
        body, name=name, grid=(nrows // tr,), in_specs=in_specs, out_specs=out_specs, out_shape=out_shape,
        compiler_params=_cparams(("arbitrary",)),
    )(*rows, *params)


def _rms(xt, g):
    r = lax.rsqrt(jnp.mean(xt * xt, axis=-1, keepdims=True) + EPS)
    return xt * r * g


def _gelu_skip(ys, u, dskip):
    return jax.nn.gelu(ys + dskip * u)


def _glu_gate(g, t, gate, bglu):
    return g * jax.nn.sigmoid(t + bglu) * jax.nn.silu(gate)


def _ogate(o, gate):
    return o * jax.nn.silu(gate)


def _ple(xt, gt, pp):
    return xt + jax.nn.sigmoid(gt) * pp


def _s5_disc(lam_re, lam_im, log_dt, bt_re, bt_im):
    lr = jnp.minimum(lam_re, -1e-4)
    li = lam_im
    dt = jnp.exp(log_dt)
    mag = jnp.exp(lr * dt)
    a_re = mag * jnp.cos(li * dt)
    a_im = mag * jnp.sin(li * dt)
    den = lr * lr + li * li
    nr = a_re - 1.0
    f_re = (nr * lr + a_im * li) / den
    f_im = (a_im * lr - nr * li) / den
    bb_re = f_re[:, None, :] * bt_re - f_im[:, None, :] * bt_im
    bb_im = f_re[:, None, :] * bt_im + f_im[:, None, :] * bt_re
    return a_re, a_im, bb_re, bb_im


def _s5_params_fwd(lam_re, lam_im, log_dt, bt_re, bt_im):
    def body(lr_ref, li_ref, dt_ref, br_ref, bi_ref, ar_ref, ai_ref, bbr_ref, bbi_ref, pr_ref, pi_ref):
        a_re, a_im, bb_re, bb_im = _s5_disc(lr_ref[...], li_ref[...], dt_ref[...], br_ref[...], bi_ref[...])
        ar_ref[...] = a_re
        ai_ref[...] = a_im
        bbr_ref[...] = bb_re
        bbi_ref[...] = bb_im
        pr, pi = a_re, a_im
        for t in range(SUBLANES):
            pr_ref[t] = pr
            pi_ref[t] = pi
            pr, pi = pr * a_re - pi * a_im, pr * a_im + pi * a_re

    g, s = lam_re.shape
    sd = jax.ShapeDtypeStruct
    return pl.pallas_call(
        body, name="s5_params_fwd",
        out_shape=[sd((g, s), F32), sd((g, s), F32), sd(bt_re.shape, F32), sd(bt_re.shape, F32),
                   sd((SUBLANES, g, s), F32), sd((SUBLANES, g, s), F32)],
    )(lam_re, lam_im, log_dt, bt_re, bt_im)


def _s5_params_bwd(lam_re, lam_im, log_dt, bt_re, bt_im, d_ar, d_ai, d_bbr, d_bbi):
    def body(lr_ref, li_ref, dt_ref, br_ref, bi_ref, c0, c1, c2, c3, o0, o1, o2, o3, o4):
        _, vjp = jax.vjp(_s5_disc, lr_ref[...], li_ref[...], dt_ref[...], br_ref[...], bi_ref[...])
        grads = vjp((jnp.sum(c0[...], axis=0), jnp.sum(c1[...], axis=0), c2[...], c3[...]))
        for o, gval in zip((o0, o1, o2, o3, o4), grads):
            o[...] = gval

    sd = jax.ShapeDtypeStruct
    return pl.pallas_call(
        body, name="s5_params_bwd",
        out_shape=[sd(lam_re.shape, F32), sd(lam_im.shape, F32), sd(log_dt.shape, F32), sd(bt_re.shape, F32),
                   sd(bt_im.shape, F32)],
    )(lam_re, lam_im, log_dt, bt_re, bt_im, d_ar, d_ai, d_bbr, d_bbi)


def _block_diag(t):
    nb, g, r, c = t.shape
    eye = jnp.eye(g, dtype=t.dtype)
    return jnp.einsum("ngrc,gh->ngrhc", t, eye).reshape(nb, g * r, g * c)


def _block_diag_extract(m, r, c):
    nb = m.shape[0]
    g = GROUPS_PER_BLOCK
    eye = jnp.eye(g, dtype=m.dtype)[None, :, None, :, None]
    return jnp.sum(m.reshape(nb, g, r, g, c) * eye, axis=3)


def _per_sublane(acc):
    return acc.reshape(N_GROUP_BLOCKS, SUBLANES, GROUPS_PER_BLOCK, STATE).transpose(1, 0, 2, 3).reshape(
        SUBLANES, N_GROUPS, STATE)


def _scan_tables(pr, pi):
    row = jnp.arange(SUBLANES)[:, None, None]

    def tables(im_sign, fwd):
        kinds = []
        for k in (1, 2, 4):
            mask = (row >= k) if fwd else (row < SUBLANES - k)
            kinds.append(jnp.where(mask, pr[k - 1][None], 0.0))
            kinds.append(jnp.where(mask, im_sign * pi[k - 1][None], 0.0))
        if fwd:
            kinds += [pr, im_sign * pi]
        else:
            kinds += [pr[::-1], im_sign * pi[::-1]]
        return jnp.stack(kinds, axis=0).transpose(2, 0, 1, 3)

    return tables(1.0, True), tables(-1.0, False)


def _scan_rows(buf_ref, tab_ref, carry_r, carry_i, n_tiles, reverse):
    w = SSM_LANES

    def tile(j, carry):
        cr, ci = carry
        jj = (n_tiles - 1 - j) if reverse else j
        rows = pl.ds(pl.multiple_of(jj * SUBLANES, SUBLANES), SUBLANES)
        zr = buf_ref[rows, 0:w]
        zi = buf_ref[rows, w:2 * w]
        for n, k in enumerate((1, 2, 4)):
            ar = tab_ref[2 * n]
            ai = tab_ref[2 * n + 1]
            sh = (SUBLANES - k) if reverse else k
            rr = pltpu.roll(zr, sh, 0)
            ri = pltpu.roll(zi, sh, 0)
            zr, zi = zr + ar * rr - ai * ri, zi + ar * ri + ai * rr
        pr = tab_ref[6]
        pi = tab_ref[7]
        zr, zi = zr + pr * cr - pi * ci, zi + pr * ci + pi * cr
        buf_ref[rows, 0:w] = zr
        buf_ref[rows, w:2 * w] = zi
        if reverse:
            return zr[0:1], zi[0:1]
        return zr[SUBLANES - 1:SUBLANES], zi[SUBLANES - 1:SUBLANES]

    return lax.fori_loop(0, n_tiles, tile, (carry_r, carry_i))


def _s5_fwd(u, bbd, cbd, tab_f, *, tt=512):
    seq = u.shape[0]
    tt = min(tt, seq)
    nt = seq // tt
    w = SSM_LANES

    def body(u_ref, b_ref, c_ref, tab_ref, y_ref, hs_ref, buf_ref, h_ref):
        it = pl.program_id(1)

        @pl.when(it == 0)
        def _():
            h_ref[...] = jnp.zeros_like(h_ref)

        hs_ref[...] = h_ref[...]
        buf_ref[...] = jnp.dot(u_ref[...].astype(MXU_DTYPE), b_ref[...], preferred_element_type=F32)
        hr, hi = _scan_rows(buf_ref, tab_ref, h_ref[:, 0:w], h_ref[:, w:2 * w], tt // SUBLANES, False)
        h_ref[:, 0:w] = hr
        h_ref[:, w:2 * w] = hi
        y_ref[...] = jnp.dot(buf_ref[...].astype(MXU_DTYPE), c_ref[...], preferred_element_type=F32)

    return pl.pallas_call(
        body, name="s5_fwd", grid=(N_GROUP_BLOCKS, nt),
        in_specs=[pl.BlockSpec((tt, 128), lambda g, t: (t, g)),
                  pl.BlockSpec((None, 128, 2 * w), lambda g, t: (g, 0, 0)),
                  pl.BlockSpec((None, 2 * w, 128), lambda g, t: (g, 0, 0)),
                  pl.BlockSpec((None, 8, SUBLANES, w), lambda g, t: (g, 0, 0, 0))],
        out_specs=[pl.BlockSpec((tt, 128), lambda g, t: (t, g)),
                   pl.BlockSpec((None, None, 1, 2 * w), lambda g, t: (g, t, 0, 0))],
        out_shape=[jax.ShapeDtypeStruct((seq, D_MODEL), F32),
                   jax.ShapeDtypeStruct((N_GROUP_BLOCKS, nt, 1, 2 * w), F32)],
        scratch_shapes=[pltpu.VMEM((tt, 2 * w), F32), pltpu.VMEM((1, 2 * w), F32)],
        compiler_params=_cparams(("parallel", "arbitrary")),
    )(u, bbd, cbd, tab_f)


def _s5_bwd(u, dy, du_add, hs, bbd, cbd_t, tab_f, tab_b, *, tt=512):
    seq = u.shape[0]
    tt = min(tt, seq)
    nt = seq // tt
    w = SSM_LANES
    n_tiles = tt // SUBLANES

    def body(u_ref, dy_ref, dua_ref, hs_ref, b_ref, ct_ref, tabf_ref, tabb_ref,
             du_ref, db_ref, dc_ref, dar_ref, dai_ref, s_ref, l_ref, lam_ref):
        it = pl.program_id(1)

        @pl.when(it == 0)
        def _():
            lam_ref[...] = jnp.zeros_like(lam_ref)
            db_ref[...] = jnp.zeros_like(db_ref)
            dc_ref[...] = jnp.zeros_like(dc_ref)
            dar_ref[...] = jnp.zeros_like(dar_ref)
            dai_ref[...] = jnp.zeros_like(dai_ref)

        ub = u_ref[...].astype(MXU_DTYPE)
        dyb = dy_ref[...].astype(MXU_DTYPE)
        s_ref[...] = jnp.dot(ub, b_ref[...], preferred_element_type=F32)
        h_in_r = hs_ref[:, 0:w]
        h_in_i = hs_ref[:, w:2 * w]
        _scan_rows(s_ref, tabf_ref, h_in_r, h_in_i, n_tiles, False)
        dc_ref[...] += lax.dot_general(s_ref[...].astype(MXU_DTYPE), dyb, (((0,), (0,)), ((), ())),
                                       preferred_element_type=F32)
        l_ref[...] = jnp.dot(dyb, ct_ref[...], preferred_element_type=F32)
        lr, li = _scan_rows(l_ref, tabb_ref, lam_ref[:, 0:w], lam_ref[:, w:2 * w], n_tiles, True)
        lam_ref[:, 0:w] = lr
        lam_ref[:, w:2 * w] = li
        lb = l_ref[...].astype(MXU_DTYPE)
        db_ref[...] += lax.dot_general(ub, lb, (((0,), (0,)), ((), ())), preferred_element_type=F32)
        du = lax.dot_general(lb, b_ref[...], (((1,), (1,)), ((), ())), preferred_element_type=F32)
        du_ref[...] = (du + dua_ref[...].astype(F32)).astype(du_ref.dtype)

        def tile(j, carry):
            pr, pi, accr, acci = carry
            rows = pl.ds(pl.multiple_of(j * SUBLANES, SUBLANES), SUBLANES)
            sr = s_ref[rows, 0:w]
            si = s_ref[rows, w:2 * w]
            first = lax.broadcasted_iota(jnp.int32, (SUBLANES, w), 0) == 0
            sr_prev = jnp.where(first, pr, pltpu.roll(sr, 1, 0))
            si_prev = jnp.where(first, pi, pltpu.roll(si, 1, 0))
            gr = l_ref[rows, 0:w]
            gi = l_ref[rows, w:2 * w]
            accr = accr + gr * sr_prev + gi * si_prev
            acci = acci + gi * sr_prev - gr * si_prev
            return sr[SUBLANES - 1:SUBLANES], si[SUBLANES - 1:SUBLANES], accr, acci

        zero = jnp.zeros((SUBLANES, w), F32)
        _, _, accr, acci = lax.fori_loop(0, n_tiles, tile, (h_in_r, h_in_i, zero, zero))
        dar_ref[...] += accr
        dai_ref[...] += acci

    rev = lambda g, t: (nt - 1 - t, g)
    return pl.pallas_call(
        body, name="s5_bwd", grid=(N_GROUP_BLOCKS, nt),
        in_specs=[pl.BlockSpec((tt, 128), rev), pl.BlockSpec((tt, 128), rev), pl.BlockSpec((tt, 128), rev),
                  pl.BlockSpec((None, None, 1, 2 * w), lambda g, t: (g, nt - 1 - t, 0, 0)),
                  pl.BlockSpec((None, 128, 2 * w), lambda g, t: (g, 0, 0)),
                  pl.BlockSpec((None, 128, 2 * w), lambda g, t: (g, 0, 0)),
                  pl.BlockSpec((None, 8, SUBLANES, w), lambda g, t: (g, 0, 0, 0)),
                  pl.BlockSpec((None, 8, SUBLANES, w), lambda g, t: (g, 0, 0, 0))],
        out_specs=[pl.BlockSpec((tt, 128), rev),
                   pl.BlockSpec((None, 128, 2 * w), lambda g, t: (g, 0, 0)),
                   pl.BlockSpec((None, 2 * w, 128), lambda g, t: (g, 0, 0)),
                   pl.BlockSpec((None, SUBLANES, w), lambda g, t: (g, 0, 0)),
                   pl.BlockSpec((None, SUBLANES, w), lambda g, t: (g, 0, 0))],
        out_shape=[jax.ShapeDtypeStruct((seq, D_MODEL), MXU_DTYPE),
                   jax.ShapeDtypeStruct((N_GROUP_BLOCKS, 128, 2 * w), F32),
                   jax.ShapeDtypeStruct((N_GROUP_BLOCKS, 2 * w, 128), F32),
                   jax.ShapeDtypeStruct((N_GROUP_BLOCKS, SUBLANES, w), F32),
                   jax.ShapeDtypeStruct((N_GROUP_BLOCKS, SUBLANES, w), F32)],
        scratch_shapes=[pltpu.VMEM((tt, 2 * w), F32), pltpu.VMEM((tt, 2 * w), F32), pltpu.VMEM((1, 2 * w), F32)],
        compiler_params=_cparams(("parallel", "arbitrary")),
    )(u, dy, du_add, hs, bbd, cbd_t, tab_f, tab_b)


def _neg_softplus(z):
    e = jnp.exp(-jnp.abs(z))
    return -(jnp.maximum(z, 0.0) + jnp.log(1.0 + e)), e


def _tri(n, keep):
    row = lax.broadcasted_iota(jnp.int32, (n, n), 0)
    col = lax.broadcasted_iota(jnp.int32, (n, n), 1)
    return jnp.where(keep(row, col), 1.0, 0.0).astype(MXU_DTYPE)


def _head_masks():
    lane = lax.broadcasted_iota(jnp.int32, (1, HEADS_PER_BLOCK * HEAD_DIM), 1)
    return [(lane >= h * HEAD_DIM) & (lane < (h + 1) * HEAD_DIM) for h in range(HEADS_PER_BLOCK)]


def _sba_fwd(q, k, v, *, tb=256):
    seq = q.shape[0]
    tb = min(tb, seq)
    nq = seq // tb
    cw = HEADS_PER_BLOCK * HEAD_DIM
    scale = HEAD_DIM ** -0.5

    def body(q_ref, k_ref, v_ref, o_ref, t_ref, acc_ref, r_ref):
        qi = pl.program_id(1)
        masks = _head_masks()
        qf = q_ref[...].astype(F32) * scale
        q2 = jnp.concatenate([jnp.where(m, qf, 0.0) for m in masks], axis=0).astype(MXU_DTYPE)
        causal = lax.broadcasted_iota(jnp.int32, (tb, tb), 1) < lax.broadcasted_iota(jnp.int32, (tb, tb), 0)
        causal2 = jnp.concatenate([causal] * HEADS_PER_BLOCK, axis=0)
        upper = {n: _tri(n, lambda j, s: j >= s) for n in (tb, 2 * tb)}
        acc_ref[...] = jnp.zeros_like(acc_ref)
        r_ref[...] = jnp.zeros_like(r_ref)

        def block(start, width, diag):
            ks = pl.ds(pl.multiple_of(start, tb), width)
            kb = k_ref[ks, :].astype(MXU_DTYPE)
            vb = v_ref[ks, :]
            z = lax.dot_general(q2, kb, (((1,), (1,)), ((), ())), preferred_element_type=F32)
            lk, _ = _neg_softplus(z)
            if diag:
                lk = jnp.where(causal2, lk, 0.0)
            incl = jnp.dot(lk.astype(MXU_DTYPE), upper[width], preferred_element_type=F32)
            a = z + incl + r_ref[...]
            if diag:
                a = jnp.where(causal2, a, -1e30)
            wgt = jnp.exp(a).astype(MXU_DTYPE)
            wcat = jnp.concatenate([wgt[h * tb:(h + 1) * tb] for h in range(HEADS_PER_BLOCK)], axis=1)
            vcat = jnp.concatenate([jnp.where(m, vb, jnp.zeros_like(vb)) for m in masks], axis=0).astype(MXU_DTYPE)
            acc_ref[...] += jnp.dot(wcat, vcat, preferred_element_type=F32)
            r_ref[...] += incl[:, 0:1]

        block(qi * tb, tb, True)

        @pl.when(qi % 2 == 1)
        def _():
            block((qi - 1) * tb, tb, False)

        n_wide = qi // 2

        def step(i, c):
            block((n_wide - 1 - i) * (2 * tb), 2 * tb, False)
            return c

        lax.fori_loop(0, n_wide, step, 0)
        o_ref[...] = acc_ref[...]
        for h in range(HEADS_PER_BLOCK):
            t_ref[:, h:h + 1] = r_ref[h * tb:(h + 1) * tb]

    n_hp = D_MODEL // cw
    return pl.pallas_call(
        body, name="sba_fwd", grid=(n_hp, nq),
        in_specs=[pl.BlockSpec((tb, cw), lambda h, i: (i, h)),
                  pl.BlockSpec((seq, cw), lambda h, i: (0, h)),
                  pl.BlockSpec((seq, cw), lambda h, i: (0, h))],
        out_specs=[pl.BlockSpec((tb, cw), lambda h, i: (i, h)),
                   pl.BlockSpec((None, tb, HEADS_PER_BLOCK), lambda h, i: (h, i, 0))],
        out_shape=[jax.ShapeDtypeStruct((seq, D_MODEL), F32),
                   jax.ShapeDtypeStruct((n_hp, seq, HEADS_PER_BLOCK), F32)],
        scratch_shapes=[pltpu.VMEM((tb, cw), F32), pltpu.VMEM((HEADS_PER_BLOCK * tb, 1), F32)],
        compiler_params=_cparams(("parallel", "arbitrary")),
    )(q, k, v)


def _sba_bwd(q, k, v, do, tot, *, tb=256):
    seq = q.shape[0]
    tb = min(tb, seq)
    nq = seq // tb
    cw = HEADS_PER_BLOCK * HEAD_DIM
    scale = HEAD_DIM ** -0.5

    def body(q_ref, k_ref, v_ref, do_ref, t_ref, dq_ref, dk_ref, dv_ref, acc_ref, lp_ref, dp_ref):
        qi = pl.program_id(1)

        @pl.when(qi == 0)
        def _():
            dk_ref[...] = jnp.zeros_like(dk_ref)
            dv_ref[...] = jnp.zeros_like(dv_ref)

        masks = _head_masks()
        qf = q_ref[...].astype(F32) * scale
        q2 = jnp.concatenate([jnp.where(m, qf, 0.0) for m in masks], axis=0).astype(MXU_DTYPE)
        dof = do_ref[...].astype(F32)
        do2 = jnp.concatenate([jnp.where(m, dof, 0.0) for m in masks], axis=0).astype(MXU_DTYPE)
        tot2 = jnp.concatenate([t_ref[:, h:h + 1] for h in range(HEADS_PER_BLOCK)], axis=0)
        causal = lax.broadcasted_iota(jnp.int32, (tb, tb), 1) < lax.broadcasted_iota(jnp.int32, (tb, tb), 0)
        causal2 = jnp.concatenate([causal] * HEADS_PER_BLOCK, axis=0)
        upper = {n: _tri(n, lambda j, s: j >= s) for n in (tb, 2 * tb)}
        lower = {n: _tri(n, lambda j, s: j <= s) for n in (tb, 2 * tb)}
        acc_ref[...] = jnp.zeros_like(acc_ref)
        lp_ref[...] = jnp.zeros_like(lp_ref)
        dp_ref[...] = jnp.zeros_like(dp_ref)

        def block(start, width, diag):
            ks = pl.ds(pl.multiple_of(start, tb), width)
            kb = k_ref[ks, :]
            vb = v_ref[ks, :].astype(MXU_DTYPE)
            z = lax.dot_general(q2, kb.astype(MXU_DTYPE), (((1,), (1,)), ((), ())), preferred_element_type=F32)
            lk, e = _neg_softplus(z)
            if diag:
                lk = jnp.where(causal2, lk, 0.0)
            incl = jnp.dot(lk.astype(MXU_DTYPE), upper[width], preferred_element_type=F32)
            lp_next = lp_ref[...] + incl[:, 0:1]
            lp_ref[...] = lp_next
            a = z + incl + (tot2 - lp_next)
            if diag:
                a = jnp.where(causal2, a, -1e30)
            wgt = jnp.exp(a)
            dw = lax.dot_general(do2, vb, (((1,), (1,)), ((), ())), preferred_element_type=F32)
            da = dw * wgt
            pre = jnp.dot(da.astype(MXU_DTYPE), lower[width], preferred_element_type=F32) + dp_ref[...]
            dp_ref[...] = pre[:, width - 1:width]
            inv = 1.0 / (1.0 + e)
            sig = jnp.where(z >= 0, inv, e * inv)
            dz = da - pre * sig
            if diag:
                dz = jnp.where(causal2, dz, 0.0)
            dzb = dz.astype(MXU_DTYPE)
            dzcat = jnp.concatenate([dzb[h * tb:(h + 1) * tb] for h in range(HEADS_PER_BLOCK)], axis=1)
            kcat = jnp.concatenate([jnp.where(m, kb, jnp.zeros_like(kb)) for m in masks], axis=0).astype(MXU_DTYPE)
            acc_ref[...] += jnp.dot(dzcat, kcat, preferred_element_type=F32)
            dk_ref[ks, :] += lax.dot_general(dzb, q2, (((0,), (0,)), ((), ())), preferred_element_type=F32)
            dv_ref[ks, :] += lax.dot_general(wgt.astype(MXU_DTYPE), do2, (((0,), (0,)), ((), ())),
                                             preferred_element_type=F32)

        def step(i, c):
            block(i * (2 * tb), 2 * tb, False)
            return c

        lax.fori_loop(0, qi // 2, step, 0)

        @pl.when(qi % 2 == 1)
        def _():
            block((qi - 1) * tb, tb, False)

        block(qi * tb, tb, True)
        dq_ref[...] = acc_ref[...] * scale

    n_hp = D_MODEL // cw
    qspec = pl.BlockSpec((tb, cw), lambda h, i: (i, h))
    full = pl.BlockSpec((seq, cw), lambda h, i: (0, h))
    return pl.pallas_call(
        body, name="sba_bwd", grid=(n_hp, nq),
        in_specs=[qspec, full, full, qspec, pl.BlockSpec((None, tb, HEADS_PER_BLOCK), lambda h, i: (h, i, 0))],
        out_specs=[qspec, full, full],
        out_shape=[jax.ShapeDtypeStruct((seq, D_MODEL), F32)] * 3,
        scratch_shapes=[pltpu.VMEM((tb, cw), F32), pltpu.VMEM((HEADS_PER_BLOCK * tb, 1), F32),
                        pltpu.VMEM((HEADS_PER_BLOCK * tb, 1), F32)],
        compiler_params=_cparams(("parallel", "arbitrary")),
    )(q, k, v, do, tot)


def _my_place():
    return lax.axis_index("x"), lax.axis_index("y"), lax.axis_index("c")


def _other_chips(x, y):
    return [(1 - x, y), (x, 1 - y), (1 - x, 1 - y)]


def _all_gather_chips(slabs, *, name):
    n = len(slabs)

    def body(*refs):
        ins, outs = refs[:n], refs[n:2 * n]
        send_sems, recv_sems, loc_sems = refs[2 * n:]
        x, y, c = _my_place()
        me = 2 * x + y
        local = [pltpu.make_async_copy(ins[i], outs[i].at[me], loc_sems.at[i]) for i in range(n)]
        for cp in local:
            cp.start()
        remote = []
        for j, (px, py) in enumerate(_other_chips(x, y)):
            for i in range(n):
                remote.append(pltpu.make_async_remote_copy(
                    src_ref=ins[i], dst_ref=outs[i].at[me], send_sem=send_sems.at[j * n + i],
                    recv_sem=recv_sems.at[j * n + i], device_id=(px, py, c), device_id_type=MESH))
        for cp in remote:
            cp.start()
        for cp in remote:
            cp.wait_recv()
        for cp in remote:
            cp.wait_send()
        for cp in local:
            cp.wait()

    return pl.pallas_call(
        body, name=name, in_specs=[ANY] * n, out_specs=[ANY] * n,
        out_shape=[jax.ShapeDtypeStruct((4,) + s.shape, s.dtype) for s in slabs],
        scratch_shapes=[pltpu.SemaphoreType.DMA((3 * n,)), pltpu.SemaphoreType.DMA((3 * n,)),
                        pltpu.SemaphoreType.DMA((n,))],
    )(*slabs)


def _sibling_exchange(send, *, name):
    def body(s_ref, o_ref, send_sem, recv_sem):
        x, y, c = _my_place()
        cp = pltpu.make_async_remote_copy(src_ref=s_ref, dst_ref=o_ref, send_sem=send_sem, recv_sem=recv_sem,
                                          device_id=(x, y, 1 - c), device_id_type=MESH)
        cp.start()
        cp.wait_recv()
        cp.wait_send()

    return pl.pallas_call(
        body, name=name, in_specs=[ANY], out_specs=ANY,
        out_shape=jax.ShapeDtypeStruct(send.shape, send.dtype),
        scratch_shapes=[pltpu.SemaphoreType.DMA, pltpu.SemaphoreType.DMA],
    )(send)


def _all_to_all_chips(parts, *, name):
    def body(p_ref, o_ref, send_sems, recv_sems, loc_sem):
        x, y, c = _my_place()
        me = 2 * x + y
        local = pltpu.make_async_copy(p_ref.at[me], o_ref.at[me], loc_sem)
        local.start()
        remote = []
        for j, (px, py) in enumerate(_other_chips(x, y)):
            remote.append(pltpu.make_async_remote_copy(
                src_ref=p_ref.at[2 * px + py], dst_ref=o_ref.at[me], send_sem=send_sems.at[j],
                recv_sem=recv_sems.at[j], device_id=(px, py, c), device_id_type=MESH))
        for cp in remote:
            cp.start()
        for cp in remote:
            cp.wait_recv()
        for cp in remote:
            cp.wait_send()
        local.wait()

    return pl.pallas_call(
        body, name=name, in_specs=[ANY], out_specs=ANY,
        out_shape=jax.ShapeDtypeStruct(parts.shape, parts.dtype),
        scratch_shapes=[pltpu.SemaphoreType.DMA((3,)), pltpu.SemaphoreType.DMA((3,)), pltpu.SemaphoreType.DMA],
    )(parts)


_SHARDED = [("a_norm_pre", 1), ("a_norm_post", 1), ("a_d_skip", 1), ("a_b_glu", 1), ("a_w_in", 2), ("a_w_glu", 1),
            ("a_w_out", 1), ("w_kv", 1), ("b_w_in", 2), ("b_w_out", 1), ("ple_w_proj", 2), ("ple_w_gate", 1)]
_REPLICATED = ["a_lam_re", "a_lam_im", "a_log_dt", "a_b_re", "a_b_im", "a_c_re", "a_c_im", "kv_norm", "b_norm_pre",
               "b_norm_post"]
_WEIGHT_ORDER = ["a_norm_pre", "a_norm_post", "a_w_in", "a_lam_re", "a_lam_im", "a_log_dt", "a_b_re", "a_b_im",
                 "a_c_re", "a_c_im", "a_d_skip", "a_w_glu", "a_b_glu", "a_w_out", "kv_norm", "w_kv", "b_norm_pre",
                 "b_norm_post", "b_w_in", "b_w_out", "ple_w_proj", "ple_w_gate"]


def _round_up(n, m):
    return (n + m - 1) // m * m


def _pack(pieces, rows):
    flat = jnp.concatenate([p.reshape(-1).astype(F32) for p in pieces])
    flat = jnp.pad(flat, (0, rows * SLAB_W - flat.shape[0]))
    return flat.reshape(rows, SLAB_W)


def _unpack(slab, shapes):
    flat = slab.reshape(-1)
    out, off = [], 0
    for s in shapes:
        n = math.prod(s)
        out.append(flat[off:off + n].reshape(s))
        off += n
    return out


def _pick_tile(n, cap):
    return max(t for t in range(SUBLANES, cap + 1, SUBLANES) if n % t == 0)


def _size_rows(shapes, mult):
    return _round_up(_round_up(sum(math.prod(s) for s in shapes), SLAB_W) // SLAB_W, mult)


def _adamw_tile(w, g, m, v):
    m2 = ADAM_B1 * m + (1.0 - ADAM_B1) * g
    v2 = ADAM_B2 * v + (1.0 - ADAM_B2) * (g * g)
    m_hat = m2 / (1.0 - ADAM_B1 ** ADAM_STEP)
    v_hat = v2 / (1.0 - ADAM_B2 ** ADAM_STEP)
    delta = -ADAM_LR * (m_hat / (jnp.sqrt(v_hat) + ADAM_EPS) + ADAM_WD * w)
    return delta, m2, v2


def _forward_backward(x0, p0, p1, tgt, W):
    seq = x0.shape[0]
    sw = D_MODEL
    g = {}

    h1, = _ew(lambda xt, gn: (_rms(xt, gn),), [x0], [W["a_norm_pre"]], [MXU_DTYPE], 0, name="a_pre_norm")
    w_u, w_ga = W["a_w_in"][:, :sw], W["a_w_in"][:, sw:]
    u = _mm(h1, w_u, name="a_in_u")
    gate = _mm(h1, w_ga, name="a_in_gate")
    a_re, a_im, bbt_re, bbt_im, pw_re, pw_im = _s5_params_fwd(W["lam_re"], W["lam_im"], W["log_dt"], W["bt_re"],
                                                               W["bt_im"])
    nb, gpb = N_GROUP_BLOCKS, GROUPS_PER_BLOCK
    bbd = jnp.concatenate([_block_diag(bbt_re.reshape(nb, gpb, GROUP_SIZE, STATE)),
                           _block_diag(bbt_im.reshape(nb, gpb, GROUP_SIZE, STATE))], axis=2).astype(MXU_DTYPE)
    crt = W["c_re"].transpose(0, 2, 1).reshape(nb, gpb, STATE, GROUP_SIZE)
    cit = W["c_im"].transpose(0, 2, 1).reshape(nb, gpb, STATE, GROUP_SIZE)
    cbd = jnp.concatenate([_block_diag(crt), -_block_diag(cit)], axis=1).astype(MXU_DTYPE)
    cbd_t = cbd.transpose(0, 2, 1)
    tab_f, tab_b = _scan_tables(pw_re.reshape(SUBLANES, nb, SSM_LANES), pw_im.reshape(SUBLANES, nb, SSM_LANES))
    ys, hs = _s5_fwd(u, bbd, cbd, tab_f)
    gl, = _ew(lambda a, b, d: (_gelu_skip(a, b, d),), [ys, u], [W["a_d_skip"]], [F32], 0, name="a_gelu")
    t = _mm(gl, W["a_w_glu"], name="a_glu")
    y3, = _ew(lambda a, b, c, d: (_glu_gate(a, b, c, d),), [gl, t, gate], [W["a_b_glu"]], [MXU_DTYPE], 0, name="a_gate")
    y4 = _mm(y3, W["a_w_out"], name="a_out")
    x1, = _ew(lambda a, b, gn: (a + _rms(b, gn),), [x0, y4], [W["a_norm_post"]], [F32], 0, name="a_post_norm")
    gt0 = _mm(x1, W["ple_w_gate"][0], name="ple0_gate")
    pp0 = _mm(p0, W["ple_w_proj"][0], name="ple0_proj")
    x2, hk, h2 = _ew(lambda a, b, c, g1, g2: (_ple(a, b, c), _rms(_ple(a, b, c), g1), _rms(_ple(a, b, c), g2)),
                     [x1, gt0, pp0], [W["kv_norm"], W["b_norm_pre"]], [F32, MXU_DTYPE, MXU_DTYPE], 0, name="ple0_mix")

    w_k, w_v = W["w_kv"][:, :sw], W["w_kv"][:, sw:]
    k = _mm(hk, w_k, out_dtype=MXU_DTYPE, name="kv_k")
    v = _mm(hk, w_v, out_dtype=MXU_DTYPE, name="kv_v")
    w_q, w_gb = W["b_w_in"][:, :sw], W["b_w_in"][:, sw:]
    q = _mm(h2, w_q, out_dtype=MXU_DTYPE, name="b_in_q")
    gate2 = _mm(h2, w_gb, name="b_in_gate")
    o, tot = _sba_fwd(q, k, v)
    y5in, = _ew(lambda a, b: (_ogate(a, b),), [o, gate2], [], [MXU_DTYPE], 0, name="b_gate")
    y5 = _mm(y5in, W["b_w_out"], name="b_out")
    x3, = _ew(lambda a, b, gn: (a + _rms(b, gn),), [x2, y5], [W["b_norm_post"]], [F32], 0, name="b_post_norm")
    gt1 = _mm(x3, W["ple_w_gate"][1], name="ple1_gate")
    pp1 = _mm(p1, W["ple_w_proj"][1], name="ple1_proj")

    def loss_fn(xt, gt, pp, tg):
        s = jax.nn.sigmoid(gt)
        d = (xt + s * pp - tg) * (1.0 / D_MODEL)
        lsum = jnp.sum(d * d, axis=(0, 1), keepdims=True) * (0.5 * D_MODEL)
        return d, d * pp * s * (1.0 - s), d * s, lsum

    dx3a, dgt1, dpp1, loss = _ew(loss_fn, [x3, gt1, pp1, tgt], [], [F32, MXU_DTYPE, MXU_DTYPE], 1, name="loss_head")
    g_gate1 = _mm(x3, dgt1, ta=True, name="ple1_gate_dw")
    g_proj1 = _mm(p1, dpp1, ta=True, name="ple1_proj_dw")
    dx3 = _mm(dgt1, W["ple_w_gate"][1], tb=True, add=dx3a, name="ple1_gate_dx")

    def post_norm_bwd(ct, yt, gn):
        _, vjp = jax.vjp(_rms, yt, gn)
        dy, dg = vjp(ct)
        return dy, dg

    dy5, g["b_norm_post"] = _ew(post_norm_bwd, [dx3, y5], [W["b_norm_post"]], [MXU_DTYPE], 1, name="b_post_norm_bwd")
    g["b_w_out"] = _mm(y5in, dy5, ta=True, name="b_out_dw")
    dy5in = _mm(dy5, W["b_w_out"], tb=True, name="b_out_dx")

    def ogate_bwd(ct, ot, gt):
        _, vjp = jax.vjp(_ogate, ot, gt)
        return vjp(ct)

    do, dgate2 = _ew(ogate_bwd, [dy5in, o, gate2], [], [MXU_DTYPE, MXU_DTYPE], 0, name="b_gate_bwd")
    dq, dk, dv = _sba_bwd(q, k, v, do, tot)
    g["b_w_in"] = jnp.concatenate([_mm(h2, dq, ta=True, name="b_in_q_dw"),
                                   _mm(h2, dgate2, ta=True, name="b_in_gate_dw")], axis=1)
    dh2 = _mm(dgate2, w_gb, tb=True, add=_mm(dq, w_q, tb=True, name="b_in_q_dx"), name="b_in_gate_dx")
    g["w_kv"] = jnp.concatenate([_mm(hk, dk, ta=True, name="kv_k_dw"), _mm(hk, dv, ta=True, name="kv_v_dw")], axis=1)
    dhk = _mm(dv, w_v, tb=True, add=_mm(dk, w_k, tb=True, name="kv_k_dx"), name="kv_v_dx")

    def mix_bwd(ct, c2, ck, xt, gt, pp, g1, g2):
        x2v, vjp_ple = jax.vjp(_ple, xt, gt, pp)
        _, vjp_k = jax.vjp(_rms, x2v, g1)
        _, vjp_b = jax.vjp(_rms, x2v, g2)
        dxk, dg1 = vjp_k(ck)
        dxb, dg2 = vjp_b(c2)
        dx2 = ct + dxk + dxb
        dx1, dgt, dpp = vjp_ple(dx2)
        return dx1, dgt, dpp, dg1, dg2

    dx1a, dgt0, dpp0, g["kv_norm"], g["b_norm_pre"] = _ew(
        mix_bwd, [dx3, dh2, dhk, x1, gt0, pp0], [W["kv_norm"], W["b_norm_pre"]], [F32, MXU_DTYPE, MXU_DTYPE], 2,
        name="ple0_mix_bwd")
    g_gate0 = _mm(x1, dgt0, ta=True, name="ple0_gate_dw")
    g_proj0 = _mm(p0, dpp0, ta=True, name="ple0_proj_dw")
    dx1 = _mm(dgt0, W["ple_w_gate"][0], tb=True, add=dx1a, name="ple0_gate_dx")
    g["ple_w_gate"] = jnp.stack([g_gate0, g_gate1])
    g["ple_w_proj"] = jnp.stack([g_proj0, g_proj1])

    dy4, g["a_norm_post"] = _ew(post_norm_bwd, [dx1, y4], [W["a_norm_post"]], [MXU_DTYPE], 1, name="a_post_norm_bwd")
    g["a_w_out"] = _mm(y3, dy4, ta=True, name="a_out_dw")
    dy3 = _mm(dy4, W["a_w_out"], tb=True, name="a_out_dx")

    def gate_bwd(ct, gt_, tt_, gat, bg):
        _, vjp = jax.vjp(_glu_gate, gt_, tt_, gat, bg)
        dg_, dt_, dgate_, dbg = vjp(ct)
        return dg_, dt_, dgate_, dbg

    dgl_a, dt, dgate, g["a_b_glu"] = _ew(gate_bwd, [dy3, gl, t, gate], [W["a_b_glu"]], [F32, MXU_DTYPE, MXU_DTYPE], 1,
                                         name="a_gate_bwd")
    g["a_w_glu"] = _mm(gl, dt, ta=True, name="a_glu_dw")
    dgl = _mm(dt, W["a_w_glu"], tb=True, add=dgl_a, name="a_glu_dx")

    def gelu_bwd(ct, yt, ut, ds):
        _, vjp = jax.vjp(_gelu_skip, yt, ut, ds)
        return vjp(ct)

    dys, du_a, g["a_d_skip"] = _ew(gelu_bwd, [dgl, ys, u], [W["a_d_skip"]], [MXU_DTYPE, F32], 1, name="a_gelu_bwd")
    du, d_bbd, d_cbd, d_ar8, d_ai8 = _s5_bwd(u, dys, du_a, hs, bbd, cbd_t, tab_f, tab_b)
    g["a_w_in"] = jnp.concatenate([_mm(h1, du, ta=True, name="a_in_u_dw"),
                                   _mm(h1, dgate, ta=True, name="a_in_gate_dw")], axis=1)
    dh1 = _mm(dgate, w_ga, tb=True, add=_mm(du, w_u, tb=True, name="a_in_u_dx"), name="a_in_gate_dx")

    def pre_norm_bwd(ct, ch, xt, gn):
        _, vjp = jax.vjp(_rms, xt, gn)
        dx, dg = vjp(ch)
        return ct + dx, dg

    dx0, g["a_norm_pre"] = _ew(pre_norm_bwd, [dx1, dh1, x0], [W["a_norm_pre"]], [F32], 1, name="a_pre_norm_bwd")

    w = SSM_LANES
    d_bbt_re = _block_diag_extract(d_bbd[:, :, :w], GROUP_SIZE, STATE).reshape(N_GROUPS, GROUP_SIZE, STATE)
    d_bbt_im = _block_diag_extract(d_bbd[:, :, w:], GROUP_SIZE, STATE).reshape(N_GROUPS, GROUP_SIZE, STATE)
    d_crt = _block_diag_extract(d_cbd[:, :w, :], STATE, GROUP_SIZE).reshape(N_GROUPS, STATE, GROUP_SIZE)
    d_cit = -_block_diag_extract(d_cbd[:, w:, :], STATE, GROUP_SIZE).reshape(N_GROUPS, STATE, GROUP_SIZE)
    g["a_c_re"] = d_crt.transpose(0, 2, 1)[None]
    g["a_c_im"] = d_cit.transpose(0, 2, 1)[None]
    d_lr, d_li, d_ldt, d_btr, d_bti = _s5_params_bwd(
        W["lam_re"], W["lam_im"], W["log_dt"], W["bt_re"], W["bt_im"], _per_sublane(d_ar8), _per_sublane(d_ai8),
        d_bbt_re, d_bbt_im)
    g["a_lam_re"], g["a_lam_im"] = d_lr[None], d_li[None]
    g["a_log_dt"] = d_ldt.reshape(1, N_GROUPS)
    g["a_b_re"] = d_btr.transpose(0, 2, 1)[None]
    g["a_b_im"] = d_bti.transpose(0, 2, 1)[None]
    return loss, dx0, g


def kernel(x, p, a_norm_pre, a_norm_post, a_w_in, a_lam_re, a_lam_im, a_log_dt, a_b_re, a_b_im, a_c_re, a_c_im, a_d_skip, a_w_glu, a_b_glu, a_w_out, kv_norm, w_kv, b_norm_pre, b_norm_post, b_w_in, b_w_out, ple_w_proj, ple_w_gate, loss_target, m_a_norm_pre, m_a_norm_post, m_a_w_in, m_a_lam_re, m_a_lam_im, m_a_log_dt, m_a_b_re, m_a_b_im, m_a_c_re, m_a_c_im, m_a_d_skip, m_a_w_glu, m_a_b_glu, m_a_w_out, m_kv_norm, m_w_kv, m_b_norm_pre, m_b_norm_post, m_b_w_in, m_b_w_out, m_ple_w_proj, m_ple_w_gate, v_a_norm_pre, v_a_norm_post, v_a_w_in, v_a_lam_re, v_a_lam_im, v_a_log_dt, v_a_b_re, v_a_b_im, v_a_c_re, v_a_c_im, v_a_d_skip, v_a_w_glu, v_a_b_glu, v_a_w_out, v_kv_norm, v_w_kv, v_b_norm_pre, v_b_norm_post, v_b_w_in, v_b_w_out, v_ple_w_proj, v_ple_w_gate):
    loc = dict(a_norm_pre=a_norm_pre, a_norm_post=a_norm_post, a_w_in=a_w_in, a_lam_re=a_lam_re, a_lam_im=a_lam_im,
               a_log_dt=a_log_dt, a_b_re=a_b_re, a_b_im=a_b_im, a_c_re=a_c_re, a_c_im=a_c_im, a_d_skip=a_d_skip,
               a_w_glu=a_w_glu, a_b_glu=a_b_glu, a_w_out=a_w_out, kv_norm=kv_norm, w_kv=w_kv, b_norm_pre=b_norm_pre,
               b_norm_post=b_norm_post, b_w_in=b_w_in, b_w_out=b_w_out, ple_w_proj=ple_w_proj, ple_w_gate=ple_w_gate)
    mom = dict(a_norm_pre=m_a_norm_pre, a_norm_post=m_a_norm_post, a_w_in=m_a_w_in, a_lam_re=m_a_lam_re,
               a_lam_im=m_a_lam_im, a_log_dt=m_a_log_dt, a_b_re=m_a_b_re, a_b_im=m_a_b_im, a_c_re=m_a_c_re,
               a_c_im=m_a_c_im, a_d_skip=m_a_d_skip, a_w_glu=m_a_w_glu, a_b_glu=m_a_b_glu, a_w_out=m_a_w_out,
               kv_norm=m_kv_norm, w_kv=m_w_kv, b_norm_pre=m_b_norm_pre, b_norm_post=m_b_norm_post, b_w_in=m_b_w_in,
               b_w_out=m_b_w_out, ple_w_proj=m_ple_w_proj, ple_w_gate=m_ple_w_gate)
    var = dict(a_norm_pre=v_a_norm_pre, a_norm_post=v_a_norm_post, a_w_in=v_a_w_in, a_lam_re=v_a_lam_re,
               a_lam_im=v_a_lam_im, a_log_dt=v_a_log_dt, a_b_re=v_a_b_re, a_b_im=v_a_b_im, a_c_re=v_a_c_re,
               a_c_im=v_a_c_im, a_d_skip=v_a_d_skip, a_w_glu=v_a_w_glu, a_b_glu=v_a_b_glu, a_w_out=v_a_w_out,
               kv_norm=v_kv_norm, w_kv=v_w_kv, b_norm_pre=v_b_norm_pre, b_norm_post=v_b_norm_post, b_w_in=v_b_w_in,
               b_w_out=v_b_w_out, ple_w_proj=v_ple_w_proj, ple_w_gate=v_ple_w_gate)
    ci = lax.axis_index("c")

    vec_names = [n for n, _ in _SHARDED if loc[n].shape[-2] == 1 and loc[n].ndim == 2]
    mat_names = [n for n, _ in _SHARDED if n not in vec_names]
    axis_of = dict(_SHARDED)
    mat_shapes = [loc[n].shape for n in mat_names]
    vec_shapes = [loc[n].shape for n in vec_names]
    mat_rows = _size_rows(mat_shapes, 16)
    vec_rows = _size_rows(vec_shapes, 8)
    mat_slab = _pack([loc[n] for n in mat_names], mat_rows).astype(MXU_DTYPE)
    vec_slab = _pack([loc[n] for n in vec_names], vec_rows)
    mat_all, vec_all = _all_gather_chips([mat_slab, vec_slab], name="gather_weights")
    W = {}
    for names, slab_all, shapes in ((mat_names, mat_all, mat_shapes), (vec_names, vec_all, vec_shapes)):
        per_chip = [_unpack(slab_all[j], shapes) for j in range(4)]
        for i, n in enumerate(names):
            W[n] = jnp.concatenate([per_chip[j][i] for j in range(4)], axis=axis_of[n])
    for n in ("a_w_in", "a_w_glu", "a_w_out", "b_w_in", "b_w_out"):
        W[n] = W[n][0]
    W["kv_norm"] = kv_norm.reshape(1, D_MODEL)
    W["b_norm_pre"], W["b_norm_post"] = b_norm_pre, b_norm_post
    W["lam_re"], W["lam_im"] = a_lam_re[0], a_lam_im[0]
    W["log_dt"] = a_log_dt.reshape(N_GROUPS, 1)
    W["bt_re"], W["bt_im"] = a_b_re[0].transpose(0, 2, 1), a_b_im[0].transpose(0, 2, 1)
    W["c_re"], W["c_im"] = a_c_re[0], a_c_im[0]

    loss, dx0, g = _forward_backward(x[0], p[0, 0], p[1, 0], loss_target[0], W)
    for n in ("a_w_in", "a_w_glu", "a_w_out", "b_w_in", "b_w_out"):
        g[n] = g[n][None]
    g["kv_norm"] = g["kv_norm"].reshape(D_MODEL)
    loss = lax.psum(loss[0, 0], ("x", "y", "c"))

    rep_shapes = [loc[n].shape for n in _REPLICATED]
    rep_rows = _size_rows(rep_shapes, 32)
    rep_slab = _pack([g[n] for n in _REPLICATED], rep_rows).reshape(4, rep_rows // 4, SLAB_W)
    sh_names = [n for n, _ in _SHARDED]
    sh_shapes = [loc[n].shape for n in sh_names]
    sh_rows = _size_rows(sh_shapes, 8)
    rows = _round_up(sh_rows + rep_rows // 4, 1024)
    half = rows // 2
    slabs = []
    for j in range(4):
        pieces = [jnp.split(g[n], 4, axis=axis_of[n])[j] for n in sh_names]
        sh = _pack(pieces, sh_rows)
        slabs.append(jnp.concatenate([sh, rep_slab[j], jnp.zeros((rows - sh_rows - rep_rows // 4, SLAB_W), F32)]))
    gs = jnp.stack(slabs).reshape(4, 2, half, SLAB_W)
    mine = lax.dynamic_index_in_dim(gs, ci, axis=1, keepdims=False).reshape(4 * half, SLAB_W)
    theirs = lax.dynamic_index_in_dim(gs, 1 - ci, axis=1, keepdims=False).reshape(4 * half, SLAB_W)
    got = _sibling_exchange(theirs, name="reduce_cores")
    part, = _ew(lambda a, b: (a + b,), [mine, got], [], [F32], 0, name="reduce_cores_add", tr=512)
    recv = _all_to_all_chips(part.reshape(4, half, SLAB_W), name="reduce_chips")
    tot_half, = _ew(lambda a, b, c, d: (((a + b) + c) + d,), [recv[0], recv[1], recv[2], recv[3]], [], [F32], 0,
                    name="reduce_chips_add", tr=512)
    other_half = _sibling_exchange(tot_half, name="share_cores")
    lo_half = jnp.where(ci == 0, tot_half, other_half)
    hi_half = jnp.where(ci == 0, other_half, tot_half)
    gsum = jnp.concatenate([lo_half, hi_half])
    rep_q = gsum[sh_rows:sh_rows + rep_rows // 4]
    rep_all, = _all_gather_chips([rep_q], name="gather_replicated")
    gsum_full = jnp.concatenate([gsum[:sh_rows], rep_all.reshape(rep_rows, SLAB_W)])

    all_names = sh_names + _REPLICATED
    all_shapes = sh_shapes + rep_shapes

    def slab_of(d):
        return jnp.concatenate([_pack([d[n] for n in sh_names], sh_rows), _pack([d[n] for n in _REPLICATED], rep_rows)])

    n_rows = sh_rows + rep_rows
    outs = _ew(_adamw_tile, [slab_of(loc), gsum_full, slab_of(mom), slab_of(var)], [], [F32] * 3, 0, name="adamw",
               tr=_pick_tile(n_rows, 512))
    res = []
    for slab in [gsum_full] + list(outs):
        parts = _unpack(slab[:sh_rows], sh_shapes) + _unpack(slab[sh_rows:], rep_shapes)
        res.append(dict(zip(all_names, parts)))
    out = [loss, dx0[None]]
    for d in res:
        out += [d[n] for n in _WEIGHT_ORDER]
    return tuple(out)
```

```python
import functools
import math

import jax
import jax.numpy as jnp
from jax import lax
from jax.experimental import pallas as pl
from jax.experimental.pallas import tpu as pltpu

F32 = jnp.float32
MXU_DTYPE = jnp.bfloat16

D_MODEL = 1024
N_GROUPS = 64
GROUP_SIZE = 16
STATE = 64
GROUPS_PER_BLOCK = 8
SSM_LANES = GROUPS_PER_BLOCK * STATE
N_GROUP_BLOCKS = N_GROUPS // GROUPS_PER_BLOCK
HEAD_DIM = 64
HEADS_PER_BLOCK = 2
KEY_BLOCKS_PER_STEP = 4
SCAN_TILES_PER_STEP = 2
EPS = 1e-6
SLAB_W = 512
SUBLANES = 8
VMEM_LIMIT = 56 * 1024 * 1024

ADAM_LR = 0.001
ADAM_B1 = 0.9
ADAM_B2 = 0.999
ADAM_EPS = 1e-08
ADAM_WD = 0.01
ADAM_STEP = 10

MESH = pl.DeviceIdType.MESH
ANY = pl.BlockSpec(memory_space=pl.ANY)


def _cparams(sem):
    return pltpu.CompilerParams(dimension_semantics=sem, vmem_limit_bytes=VMEM_LIMIT)


def _mm(a, b, *, ta=False, tb=False, add=None, out_dtype=F32, name, tm=1024, tn=1024, tk=1024):
    if ta:
        kdim, m = a.shape
    else:
        m, kdim = a.shape
    if tb:
        n, kb = b.shape
    else:
        kb, n = b.shape
    assert kdim == kb, (a.shape, b.shape, ta, tb)
    tm, tn, tk = min(tm, m), min(tn, n), min(tk, kdim)
    assert m % tm == 0 and n % tn == 0 and kdim % tk == 0
    nk = kdim // tk
    dims = (((0 if ta else 1,), (1 if tb else 0,)), ((), ()))
    has_add = add is not None

    def body(*refs):
        if has_add:
            a_ref, b_ref, add_ref, o_ref, acc_ref = refs
        else:
            a_ref, b_ref, o_ref, acc_ref = refs
        k = pl.program_id(2)

        @pl.when(k == 0)
        def _():
            acc_ref[...] = jnp.zeros_like(acc_ref)

        acc_ref[...] += lax.dot_general(a_ref[...].astype(MXU_DTYPE), b_ref[...].astype(MXU_DTYPE), dims,
                                        preferred_element_type=F32)

        @pl.when(k == nk - 1)
        def _():
            r = acc_ref[...]
            if has_add:
                r = r + add_ref[...].astype(F32)
            o_ref[...] = r.astype(o_ref.dtype)

    a_spec = pl.BlockSpec((tk, tm), lambda i, j, k: (k, i)) if ta else pl.BlockSpec((tm, tk), lambda i, j, k: (i, k))
    b_spec = pl.BlockSpec((tn, tk), lambda i, j, k: (j, k)) if tb else pl.BlockSpec((tk, tn), lambda i, j, k: (k, j))
    o_spec = pl.BlockSpec((tm, tn), lambda i, j, k: (i, j))
    in_specs = [a_spec, b_spec] + ([o_spec] if has_add else [])
    args = (a, b) + ((add,) if has_add else ())
    return pl.pallas_call(
        body, name=name, grid=(m // tm, n // tn, nk), in_specs=in_specs, out_specs=o_spec,
        out_shape=jax.ShapeDtypeStruct((m, n), out_dtype), scratch_shapes=[pltpu.VMEM((tm, tn), F32)],
        compiler_params=_cparams(("parallel", "parallel", "arbitrary")),
    )(*args)


def _ew(fn, rows, params, row_dtypes, n_acc, *, name, tr=256):
    nrows = rows[0].shape[0]
    tr = min(tr, nrows)
    assert nrows % tr == 0
    n_in, n_par, n_row = len(rows), len(params), len(row_dtypes)
    tile_avals = [jax.ShapeDtypeStruct((tr, r.shape[1]), F32) for r in rows]
    par_avals = [jax.ShapeDtypeStruct(q.shape, F32) for q in params]
    out_avals = jax.eval_shape(fn, *tile_avals, *par_avals)
    assert len(out_avals) == n_row + n_acc

    def body(*refs):
        ins = [r[...].astype(F32) for r in refs[:n_in + n_par]]
        outs = fn(*ins)
        row_refs = refs[n_in + n_par:n_in + n_par + n_row]
        acc_refs = refs[n_in + n_par + n_row:]
        for r, o in zip(row_refs, outs[:n_row]):
            r[...] = o.astype(r.dtype)
        if n_acc:
            @pl.when(pl.program_id(0) == 0)
            def _():
                for r in acc_refs:
                    r[...] = jnp.zeros_like(r)
            for r, o in zip(acc_refs, outs[n_row:]):
                r[...] += o

    in_specs = [pl.BlockSpec((tr, r.shape[1]), lambda i: (i, 0)) for r in rows]
    in_specs += [pl.BlockSpec(q.shape, lambda i: (0, 0)) for q in params]
    out_specs = [pl.BlockSpec((tr, o.shape[1]), lambda i: (i, 0)) for o in out_avals[:n_row]]
    out_specs += [pl.BlockSpec(o.shape, lambda i: (0, 0)) for o in out_avals[n_row:]]
    out_shape = [jax.ShapeDtypeStruct((nrows, o.shape[1]), dt) for o, dt in zip(out_avals[:n_row], row_dtypes)]
    out_shape += [jax.ShapeDtypeStruct(o.shape, F32) for o in out_avals[n_row:]]
    return pl.pallas_call(
        body, name=name, grid=(nrows // tr,), in_specs=in_specs, out_specs=out_specs, out_shape=out_shape,
        compiler_params=_cparams(("arbitrary",)),
    )(*rows, *params)


def _rms(xt, g):
    r = lax.rsqrt(jnp.mean(xt * xt, axis=-1, keepdims=True) + EPS)
    return xt * r * g


def _gelu_skip(ys, u, dskip):
    return jax.nn.gelu(ys + dskip * u)


def _glu_gate(g, t, gate, bglu):
    return g * jax.nn.sigmoid(t + bglu) * jax.nn.silu(gate)


def _ogate(o, gate):
    return o * jax.nn.silu(gate)


def _ple(xt, gt, pp):
    return xt + jax.nn.sigmoid(gt) * pp


def _s5_disc(lam_re, lam_im, log_dt, bt_re, bt_im):
    lr = jnp.minimum(lam_re, -1e-4)
    li = lam_im
    dt = jnp.exp(log_dt)
    mag = jnp.exp(lr * dt)
    a_re = mag * jnp.cos(li * dt)
    a_im = mag * jnp.sin(li * dt)
    den = lr * lr + li * li
    nr = a_re - 1.0
    f_re = (nr * lr + a_im * li) / den
    f_im = (a_im * lr - nr * li) / den
    bb_re = f_re[:, None, :] * bt_re - f_im[:, None, :] * bt_im
    bb_im = f_re[:, None, :] * bt_im + f_im[:, None, :] * bt_re
    return a_re, a_im, bb_re, bb_im


def _s5_params_fwd(lam_re, lam_im, log_dt, bt_re, bt_im):
    def body(lr_ref, li_ref, dt_ref, br_ref, bi_ref, ar_ref, ai_ref, bbr_ref, bbi_ref, pr_ref, pi_ref):
        a_re, a_im, bb_re, bb_im = _s5_disc(lr_ref[...], li_ref[...], dt_ref[...], br_ref[...], bi_ref[...])
        ar_ref[...] = a_re
        ai_ref[...] = a_im
        bbr_ref[...] = bb_re
        bbi_ref[...] = bb_im
        pr, pi = a_re, a_im
        for t in range(SUBLANES):
            pr_ref[t] = pr
            pi_ref[t] = pi
            pr, pi = pr * a_re - pi * a_im, pr * a_im + pi * a_re

    g, s = lam_re.shape
    sd = jax.ShapeDtypeStruct
    return pl.pallas_call(
        body, name="s5_params_fwd",
        out_shape=[sd((g, s), F32), sd((g, s), F32), sd(bt_re.shape, F32), sd(bt_re.shape, F32),
                   sd((SUBLANES, g, s), F32), sd((SUBLANES, g, s), F32)],
    )(lam_re, lam_im, log_dt, bt_re, bt_im)


def _s5_params_bwd(lam_re, lam_im, log_dt, bt_re, bt_im, d_ar, d_ai, d_bbr, d_bbi):
    def body(lr_ref, li_ref, dt_ref, br_ref, bi_ref, c0, c1, c2, c3, o0, o1, o2, o3, o4):
        _, vjp = jax.vjp(_s5_disc, lr_ref[...], li_ref[...], dt_ref[...], br_ref[...], bi_ref[...])
        grads = vjp((jnp.sum(c0[...], axis=0), jnp.sum(c1[...], axis=0), c2[...], c3[...]))
        for o, gval in zip((o0, o1, o2, o3, o4), grads):
            o[...] = gval

    sd = jax.ShapeDtypeStruct
    return pl.pallas_call(
        body, name="s5_params_bwd",
        out_shape=[sd(lam_re.shape, F32), sd(lam_im.shape, F32), sd(log_dt.shape, F32), sd(bt_re.shape, F32),
                   sd(bt_im.shape, F32)],
    )(lam_re, lam_im, log_dt, bt_re, bt_im, d_ar, d_ai, d_bbr, d_bbi)


def _block_diag(t):
    nb, g, r, c = t.shape
    eye = jnp.eye(g, dtype=t.dtype)
    return jnp.einsum("ngrc,gh->ngrhc", t, eye).reshape(nb, g * r, g * c)


def _block_diag_extract(m, r, c):
    nb = m.shape[0]
    g = GROUPS_PER_BLOCK
    eye = jnp.eye(g, dtype=m.dtype)[None, :, None, :, None]
    return jnp.sum(m.reshape(nb, g, r, g, c) * eye, axis=3)


def _per_sublane(acc):
    return acc.reshape(N_GROUP_BLOCKS, SUBLANES, GROUPS_PER_BLOCK, STATE).transpose(1, 0, 2, 3).reshape(
        SUBLANES, N_GROUPS, STATE)


def _scan_tables(pr, pi):
    row = jnp.arange(SUBLANES)[:, None, None]

    def tables(im_sign, fwd):
        kinds = []
        for k in (1, 2, 4):
            mask = (row >= k) if fwd else (row < SUBLANES - k)
            kinds.append(jnp.where(mask, pr[k - 1][None], 0.0))
            kinds.append(jnp.where(mask, im_sign * pi[k - 1][None], 0.0))
        if fwd:
            kinds += [pr, im_sign * pi]
        else:
            kinds += [pr[::-1], im_sign * pi[::-1]]
        return jnp.stack(kinds, axis=0).transpose(2, 0, 1, 3)

    return tables(1.0, True), tables(-1.0, False)


def _scan_rows(buf_ref, tab_ref, carry_r, carry_i, n_tiles, reverse):
    w = SSM_LANES
    assert n_tiles % SCAN_TILES_PER_STEP == 0

    def local_scan(jj):
        rows = pl.ds(pl.multiple_of(jj * SUBLANES, SUBLANES), SUBLANES)
        zr = buf_ref[rows, 0:w]
        zi = buf_ref[rows, w:2 * w]
        for n, k in enumerate((1, 2, 4)):
            ar = tab_ref[2 * n]
            ai = tab_ref[2 * n + 1]
            sh = (SUBLANES - k) if reverse else k
            rr = pltpu.roll(zr, sh, 0)
            ri = pltpu.roll(zi, sh, 0)
            zr, zi = zr + ar * rr - ai * ri, zi + ar * ri + ai * rr
        return rows, zr, zi

    def step(j, carry):
        first = j * SCAN_TILES_PER_STEP
        tiles = [local_scan((n_tiles - 1 - first - d) if reverse else (first + d)) for d in range(SCAN_TILES_PER_STEP)]
        cr, ci = carry
        pr = tab_ref[6]
        pi = tab_ref[7]
        for rows, zr, zi in tiles:
            zr, zi = zr + pr * cr - pi * ci, zi + pr * ci + pi * cr
            buf_ref[rows, 0:w] = zr
            buf_ref[rows, w:2 * w] = zi
            cr, ci = (zr[0:1], zi[0:1]) if reverse else (zr[SUBLANES - 1:SUBLANES], zi[SUBLANES - 1:SUBLANES])
        return cr, ci

    return lax.fori_loop(0, n_tiles // SCAN_TILES_PER_STEP, step, (carry_r, carry_i))


def _s5_fwd(u, bbd, cbd, tab_f, *, tt=512):
    seq = u.shape[0]
    tt = min(tt, seq)
    nt = seq // tt
    w = SSM_LANES

    def body(u_ref, b_ref, c_ref, tab_ref, y_ref, hs_ref, buf_ref, h_ref):
        it = pl.program_id(1)

        @pl.when(it == 0)
        def _():
            h_ref[...] = jnp.zeros_like(h_ref)

        hs_ref[...] = h_ref[...]
        buf_ref[...] = jnp.dot(u_ref[...].astype(MXU_DTYPE), b_ref[...], preferred_element_type=F32)
        hr, hi = _scan_rows(buf_ref, tab_ref, h_ref[:, 0:w], h_ref[:, w:2 * w], tt // SUBLANES, False)
        h_ref[:, 0:w] = hr
        h_ref[:, w:2 * w] = hi
        y_ref[...] = jnp.dot(buf_ref[...].astype(MXU_DTYPE), c_ref[...], preferred_element_type=F32)

    return pl.pallas_call(
        body, name="s5_fwd", grid=(N_GROUP_BLOCKS, nt),
        in_specs=[pl.BlockSpec((tt, 128), lambda g, t: (t, g)),
                  pl.BlockSpec((None, 128, 2 * w), lambda g, t: (g, 0, 0)),
                  pl.BlockSpec((None, 2 * w, 128), lambda g, t: (g, 0, 0)),
                  pl.BlockSpec((None, 8, SUBLANES, w), lambda g, t: (g, 0, 0, 0))],
        out_specs=[pl.BlockSpec((tt, 128), lambda g, t: (t, g)),
                   pl.BlockSpec((None, None, 1, 2 * w), lambda g, t: (g, t, 0, 0))],
        out_shape=[jax.ShapeDtypeStruct((seq, D_MODEL), F32),
                   jax.ShapeDtypeStruct((N_GROUP_BLOCKS, nt, 1, 2 * w), F32)],
        scratch_shapes=[pltpu.VMEM((tt, 2 * w), F32), pltpu.VMEM((1, 2 * w), F32)],
        compiler_params=_cparams(("parallel", "arbitrary")),
    )(u, bbd, cbd, tab_f)


def _s5_bwd(u, dy, du_add, hs, bbd, cbd_t, tab_f, tab_b, *, tt=512):
    seq = u.shape[0]
    tt = min(tt, seq)
    nt = seq // tt
    w = SSM_LANES
    n_tiles = tt // SUBLANES

    def body(u_ref, dy_ref, dua_ref, hs_ref, b_ref, ct_ref, tabf_ref, tabb_ref,
             du_ref, db_ref, dc_ref, dar_ref, dai_ref, s_ref, l_ref, lam_ref):
        it = pl.program_id(1)

        @pl.when(it == 0)
        def _():
            lam_ref[...] = jnp.zeros_like(lam_ref)
            db_ref[...] = jnp.zeros_like(db_ref)
            dc_ref[...] = jnp.zeros_like(dc_ref)
            dar_ref[...] = jnp.zeros_like(dar_ref)
            dai_ref[...] = jnp.zeros_like(dai_ref)

        ub = u_ref[...].astype(MXU_DTYPE)
        dyb = dy_ref[...].astype(MXU_DTYPE)
        s_ref[...] = jnp.dot(ub, b_ref[...], preferred_element_type=F32)
        h_in_r = hs_ref[:, 0:w]
        h_in_i = hs_ref[:, w:2 * w]
        _scan_rows(s_ref, tabf_ref, h_in_r, h_in_i, n_tiles, False)
        dc_ref[...] += lax.dot_general(s_ref[...].astype(MXU_DTYPE), dyb, (((0,), (0,)), ((), ())),
                                       preferred_element_type=F32)
        l_ref[...] = jnp.dot(dyb, ct_ref[...], preferred_element_type=F32)
        lr, li = _scan_rows(l_ref, tabb_ref, lam_ref[:, 0:w], lam_ref[:, w:2 * w], n_tiles, True)
        lam_ref[:, 0:w] = lr
        lam_ref[:, w:2 * w] = li
        lb = l_ref[...].astype(MXU_DTYPE)
        db_ref[...] += lax.dot_general(ub, lb, (((0,), (0,)), ((), ())), preferred_element_type=F32)
        du = lax.dot_general(lb, b_ref[...], (((1,), (1,)), ((), ())), preferred_element_type=F32)
        du_ref[...] = (du + dua_ref[...].astype(F32)).astype(du_ref.dtype)

        def tile(j, carry):
            pr, pi, accr, acci = carry
            rows = pl.ds(pl.multiple_of(j * SUBLANES, SUBLANES), SUBLANES)
            sr = s_ref[rows, 0:w]
            si = s_ref[rows, w:2 * w]
            first = lax.broadcasted_iota(jnp.int32, (SUBLANES, w), 0) == 0
            sr_prev = jnp.where(first, pr, pltpu.roll(sr, 1, 0))
            si_prev = jnp.where(first, pi, pltpu.roll(si, 1, 0))
            gr = l_ref[rows, 0:w]
            gi = l_ref[rows, w:2 * w]
            accr = accr + gr * sr_prev + gi * si_prev
            acci = acci + gi * sr_prev - gr * si_prev
            return sr[SUBLANES - 1:SUBLANES], si[SUBLANES - 1:SUBLANES], accr, acci

        zero = jnp.zeros((SUBLANES, w), F32)
        _, _, accr, acci = lax.fori_loop(0, n_tiles, tile, (h_in_r, h_in_i, zero, zero))
        dar_ref[...] += accr
        dai_ref[...] += acci

    rev = lambda g, t: (nt - 1 - t, g)
    return pl.pallas_call(
        body, name="s5_bwd", grid=(N_GROUP_BLOCKS, nt),
        in_specs=[pl.BlockSpec((tt, 128), rev), pl.BlockSpec((tt, 128), rev), pl.BlockSpec((tt, 128), rev),
                  pl.BlockSpec((None, None, 1, 2 * w), lambda g, t: (g, nt - 1 - t, 0, 0)),
                  pl.BlockSpec((None, 128, 2 * w), lambda g, t: (g, 0, 0)),
                  pl.BlockSpec((None, 128, 2 * w), lambda g, t: (g, 0, 0)),
                  pl.BlockSpec((None, 8, SUBLANES, w), lambda g, t: (g, 0, 0, 0)),
                  pl.BlockSpec((None, 8, SUBLANES, w), lambda g, t: (g, 0, 0, 0))],
        out_specs=[pl.BlockSpec((tt, 128), rev),
                   pl.BlockSpec((None, 128, 2 * w), lambda g, t: (g, 0, 0)),
                   pl.BlockSpec((None, 2 * w, 128), lambda g, t: (g, 0, 0)),
                   pl.BlockSpec((None, SUBLANES, w), lambda g, t: (g, 0, 0)),
                   pl.BlockSpec((None, SUBLANES, w), lambda g, t: (g, 0, 0))],
        out_shape=[jax.ShapeDtypeStruct((seq, D_MODEL), MXU_DTYPE),
                   jax.ShapeDtypeStruct((N_GROUP_BLOCKS, 128, 2 * w), F32),
                   jax.ShapeDtypeStruct((N_GROUP_BLOCKS, 2 * w, 128), F32),
                   jax.ShapeDtypeStruct((N_GROUP_BLOCKS, SUBLANES, w), F32),
                   jax.ShapeDtypeStruct((N_GROUP_BLOCKS, SUBLANES, w), F32)],
        scratch_shapes=[pltpu.VMEM((tt, 2 * w), F32), pltpu.VMEM((tt, 2 * w), F32), pltpu.VMEM((1, 2 * w), F32)],
        compiler_params=_cparams(("parallel", "arbitrary")),
    )(u, dy, du_add, hs, bbd, cbd_t, tab_f, tab_b)


LOG2E = 1.4426950408889634
LN2 = 0.6931471805599453
SOFTPLUS2_LINEAR = 28.0


def _softplus2(z):
    u = 1.0 + jnp.exp2(z)
    return jnp.where(z > SOFTPLUS2_LINEAR, z, jnp.log2(u)), u


def _tri(n, keep, value=1.0):
    row = lax.broadcasted_iota(jnp.int32, (n, n), 0)
    col = lax.broadcasted_iota(jnp.int32, (n, n), 1)
    return jnp.where(keep(row, col), value, 0.0).astype(MXU_DTYPE)


def _suffix_sums(x, tri):
    c = tri.shape[0]
    outs, carry = [], None
    for i in reversed(range(x.shape[1] // c)):
        part = jnp.dot(x[:, i * c:(i + 1) * c].astype(MXU_DTYPE), tri, preferred_element_type=F32)
        if carry is not None:
            part = part + carry
        carry = part[:, 0:1]
        outs.append(part)
    return jnp.concatenate(outs[::-1], axis=1) if len(outs) > 1 else outs[0]


def _prefix_sums(x, tri, carry):
    c = tri.shape[0]
    outs = []
    for i in range(x.shape[1] // c):
        part = jnp.dot(x[:, i * c:(i + 1) * c].astype(MXU_DTYPE), tri, preferred_element_type=F32) + carry
        carry = part[:, c - 1:c]
        outs.append(part)
    return jnp.concatenate(outs, axis=1) if len(outs) > 1 else outs[0]


def _head_masks():
    lane = lax.broadcasted_iota(jnp.int32, (1, HEADS_PER_BLOCK * HEAD_DIM), 1)
    return [(lane >= h * HEAD_DIM) & (lane < (h + 1) * HEAD_DIM) for h in range(HEADS_PER_BLOCK)]


def _sba_fwd(q, k, v, *, tb=256):
    seq = q.shape[0]
    tb = min(tb, seq)
    nq = seq // tb
    cw = HEADS_PER_BLOCK * HEAD_DIM
    scale = HEAD_DIM ** -0.5

    def body(q_ref, k_ref, v_ref, o_ref, t_ref, acc_ref, r_ref):
        qi = pl.program_id(1)
        masks = _head_masks()
        qf = q_ref[...].astype(F32) * (scale * LOG2E)
        q2 = jnp.concatenate([jnp.where(m, qf, 0.0) for m in masks], axis=0).astype(MXU_DTYPE)
        causal = lax.broadcasted_iota(jnp.int32, (tb, tb), 1) < lax.broadcasted_iota(jnp.int32, (tb, tb), 0)
        causal2 = jnp.concatenate([causal] * HEADS_PER_BLOCK, axis=0)
        neg_upper = _tri(tb, lambda j, s: j >= s, -1.0)
        acc_ref[...] = jnp.zeros_like(acc_ref)
        r_ref[...] = jnp.zeros_like(r_ref)

        def block(start, width, diag):
            ks = pl.ds(pl.multiple_of(start, tb), width)
            kb = k_ref[ks, :].astype(MXU_DTYPE)
            vb = v_ref[ks, :]
            z = lax.dot_general(q2, kb, (((1,), (1,)), ((), ())), preferred_element_type=F32)
            sp, _ = _softplus2(z)
            if diag:
                sp = jnp.where(causal2, sp, 0.0)
            incl = _suffix_sums(sp, neg_upper)
            a = z + incl + r_ref[...]
            if diag:
                a = jnp.where(causal2, a, -1e30)
            wgt = jnp.exp2(a).astype(MXU_DTYPE)
            wcat = jnp.concatenate([wgt[h * tb:(h + 1) * tb] for h in range(HEADS_PER_BLOCK)], axis=1)
            vcat = jnp.concatenate([jnp.where(m, vb, jnp.zeros_like(vb)) for m in masks], axis=0).astype(MXU_DTYPE)
            acc_ref[...] += jnp.dot(wcat, vcat, preferred_element_type=F32)
            r_ref[...] += incl[:, 0:1]

        block(qi * tb, tb, True)
        n_wide = qi // KEY_BLOCKS_PER_STEP

        def single(i, c):
            block((qi - 1 - i) * tb, tb, False)
            return c

        lax.fori_loop(0, qi - n_wide * KEY_BLOCKS_PER_STEP, single, 0)

        def step(i, c):
            block((n_wide - 1 - i) * (KEY_BLOCKS_PER_STEP * tb), KEY_BLOCKS_PER_STEP * tb, False)
            return c

        lax.fori_loop(0, n_wide, step, 0)
        o_ref[...] = acc_ref[...]
        for h in range(HEADS_PER_BLOCK):
            t_ref[:, h:h + 1] = r_ref[h * tb:(h + 1) * tb]

    n_hp = D_MODEL // cw
    return pl.pallas_call(
        body, name="sba_fwd", grid=(n_hp, nq),
        in_specs=[pl.BlockSpec((tb, cw), lambda h, i: (i, h)),
                  pl.BlockSpec((seq, cw), lambda h, i: (0, h)),
                  pl.BlockSpec((seq, cw), lambda h, i: (0, h))],
        out_specs=[pl.BlockSpec((tb, cw), lambda h, i: (i, h)),
                   pl.BlockSpec((None, tb, HEADS_PER_BLOCK), lambda h, i: (h, i, 0))],
        out_shape=[jax.ShapeDtypeStruct((seq, D_MODEL), F32),
                   jax.ShapeDtypeStruct((n_hp, seq, HEADS_PER_BLOCK), F32)],
        scratch_shapes=[pltpu.VMEM((tb, cw), F32), pltpu.VMEM((HEADS_PER_BLOCK * tb, 1), F32)],
        compiler_params=_cparams(("parallel", "arbitrary")),
    )(q, k, v)


def _sba_bwd(q, k, v, do, tot, *, tb=256):
    seq = q.shape[0]
    tb = min(tb, seq)
    nq = seq // tb
    cw = HEADS_PER_BLOCK * HEAD_DIM
    scale = HEAD_DIM ** -0.5

    def body(q_ref, k_ref, v_ref, do_ref, t_ref, dq_ref, dk_ref, dv_ref, acc_ref, lp_ref, dp_ref):
        qi = pl.program_id(1)

        @pl.when(qi == 0)
        def _():
            dk_ref[...] = jnp.zeros_like(dk_ref)
            dv_ref[...] = jnp.zeros_like(dv_ref)

        masks = _head_masks()
        qf = q_ref[...].astype(F32) * (scale * LOG2E)
        q2 = jnp.concatenate([jnp.where(m, qf, 0.0) for m in masks], axis=0).astype(MXU_DTYPE)
        dof = do_ref[...].astype(F32)
        do2 = jnp.concatenate([jnp.where(m, dof, 0.0) for m in masks], axis=0).astype(MXU_DTYPE)
        tot2 = jnp.concatenate([t_ref[:, h:h + 1] for h in range(HEADS_PER_BLOCK)], axis=0)
        causal = lax.broadcasted_iota(jnp.int32, (tb, tb), 1) < lax.broadcasted_iota(jnp.int32, (tb, tb), 0)
        causal2 = jnp.concatenate([causal] * HEADS_PER_BLOCK, axis=0)
        neg_upper = _tri(tb, lambda j, s: j >= s, -1.0)
        lower = _tri(tb, lambda j, s: j <= s)
        acc_ref[...] = jnp.zeros_like(acc_ref)
        lp_ref[...] = jnp.zeros_like(lp_ref)
        dp_ref[...] = jnp.zeros_like(dp_ref)

        def block(start, width, diag):
            ks = pl.ds(pl.multiple_of(start, tb), width)
            kb = k_ref[ks, :]
            vb = v_ref[ks, :].astype(MXU_DTYPE)
            z = lax.dot_general(q2, kb.astype(MXU_DTYPE), (((1,), (1,)), ((), ())), preferred_element_type=F32)
            sp, u = _softplus2(z)
            if diag:
                sp = jnp.where(causal2, sp, 0.0)
            incl = _suffix_sums(sp, neg_upper)
            lp_next = lp_ref[...] + incl[:, 0:1]
            lp_ref[...] = lp_next
            a = z + incl + (tot2 - lp_next)
            if diag:
                a = jnp.where(causal2, a, -1e30)
            wgt = jnp.exp2(a)
            dw = lax.dot_general(do2, vb, (((1,), (1,)), ((), ())), preferred_element_type=F32)
            da = dw * wgt
            pre = _prefix_sums(da, lower, dp_ref[...])
            dp_ref[...] = pre[:, width - 1:width]
            sig = 1.0 - pl.reciprocal(u, approx=True)
            dz = da - pre * sig
            if diag:
                dz = jnp.where(causal2, dz, 0.0)
            dzb = dz.astype(MXU_DTYPE)
            dzcat = jnp.concatenate([dzb[h * tb:(h + 1) * tb] for h in range(HEADS_PER_BLOCK)], axis=1)
            kcat = jnp.concatenate([jnp.where(m, kb, jnp.zeros_like(kb)) for m in masks], axis=0).astype(MXU_DTYPE)
            acc_ref[...] += jnp.dot(dzcat, kcat, preferred_element_type=F32)
            dk_ref[ks, :] += lax.dot_general(dzb, q2, (((0,), (0,)), ((), ())), preferred_element_type=F32) * LN2
            dv_ref[ks, :] += lax.dot_general(wgt.astype(MXU_DTYPE), do2, (((0,), (0,)), ((), ())),
                                             preferred_element_type=F32)

        n_wide = qi // KEY_BLOCKS_PER_STEP

        def step(i, c):
            block(i * (KEY_BLOCKS_PER_STEP * tb), KEY_BLOCKS_PER_STEP * tb, False)
            return c

        lax.fori_loop(0, n_wide, step, 0)

        def single(i, c):
            block((n_wide * KEY_BLOCKS_PER_STEP + i) * tb, tb, False)
            return c

        lax.fori_loop(0, qi - n_wide * KEY_BLOCKS_PER_STEP, single, 0)
        block(qi * tb, tb, True)
        dq_ref[...] = acc_ref[...] * scale

    n_hp = D_MODEL // cw
    qspec = pl.BlockSpec((tb, cw), lambda h, i: (i, h))
    full = pl.BlockSpec((seq, cw), lambda h, i: (0, h))
    return pl.pallas_call(
        body, name="sba_bwd", grid=(n_hp, nq),
        in_specs=[qspec, full, full, qspec, pl.BlockSpec((None, tb, HEADS_PER_BLOCK), lambda h, i: (h, i, 0))],
        out_specs=[qspec, full, full],
        out_shape=[jax.ShapeDtypeStruct((seq, D_MODEL), F32)] * 3,
        scratch_shapes=[pltpu.VMEM((tb, cw), F32), pltpu.VMEM((HEADS_PER_BLOCK * tb, 1), F32),
                        pltpu.VMEM((HEADS_PER_BLOCK * tb, 1), F32)],
        compiler_params=_cparams(("parallel", "arbitrary")),
    )(q, k, v, do, tot)


def _my_place():
    return lax.axis_index("x"), lax.axis_index("y"), lax.axis_index("c")


def _other_chips(x, y):
    return [(1 - x, y), (x, 1 - y), (1 - x, 1 - y)]


def _all_gather_chips(slabs, *, name):
    n = len(slabs)

    def body(*refs):
        ins, outs = refs[:n], refs[n:2 * n]
        send_sems, recv_sems, loc_sems = refs[2 * n:]
        x, y, c = _my_place()
        me = 2 * x + y
        local = [pltpu.make_async_copy(ins[i], outs[i].at[me], loc_sems.at[i]) for i in range(n)]
        for cp in local:
            cp.start()
        remote = []
        for j, (px, py) in enumerate(_other_chips(x, y)):
            for i in range(n):
                remote.append(pltpu.make_async_remote_copy(
                    src_ref=ins[i], dst_ref=outs[i].at[me], send_sem=send_sems.at[j * n + i],
                    recv_sem=recv_sems.at[j * n + i], device_id=(px, py, c), device_id_type=MESH))
        for cp in remote:
            cp.start()
        for cp in remote:
            cp.wait_recv()
        for cp in remote:
            cp.wait_send()
        for cp in local:
            cp.wait()

    return pl.pallas_call(
        body, name=name, in_specs=[ANY] * n, out_specs=[ANY] * n,
        out_shape=[jax.ShapeDtypeStruct((4,) + s.shape, s.dtype) for s in slabs],
        scratch_shapes=[pltpu.SemaphoreType.DMA((3 * n,)), pltpu.SemaphoreType.DMA((3 * n,)),
                        pltpu.SemaphoreType.DMA((n,))],
    )(*slabs)


def _sibling_exchange(send, *, name):
    def body(s_ref, o_ref, send_sem, recv_sem):
        x, y, c = _my_place()
        cp = pltpu.make_async_remote_copy(src_ref=s_ref, dst_ref=o_ref, send_sem=send_sem, recv_sem=recv_sem,
                                          device_id=(x, y, 1 - c), device_id_type=MESH)
        cp.start()
        cp.wait_recv()
        cp.wait_send()

    return pl.pallas_call(
        body, name=name, in_specs=[ANY], out_specs=ANY,
        out_shape=jax.ShapeDtypeStruct(send.shape, send.dtype),
        scratch_shapes=[pltpu.SemaphoreType.DMA, pltpu.SemaphoreType.DMA],
    )(send)


def _all_to_all_chips(parts, *, name):
    def body(p_ref, o_ref, send_sems, recv_sems, loc_sem):
        x, y, c = _my_place()
        me = 2 * x + y
        local = pltpu.make_async_copy(p_ref.at[me], o_ref.at[me], loc_sem)
        local.start()
        remote = []
        for j, (px, py) in enumerate(_other_chips(x, y)):
            remote.append(pltpu.make_async_remote_copy(
                src_ref=p_ref.at[2 * px + py], dst_ref=o_ref.at[me], send_sem=send_sems.at[j],
                recv_sem=recv_sems.at[j], device_id=(px, py, c), device_id_type=MESH))
        for cp in remote:
            cp.start()
        for cp in remote:
            cp.wait_recv()
        for cp in remote:
            cp.wait_send()
        local.wait()

    return pl.pallas_call(
        body, name=name, in_specs=[ANY], out_specs=ANY,
        out_shape=jax.ShapeDtypeStruct(parts.shape, parts.dtype),
        scratch_shapes=[pltpu.SemaphoreType.DMA((3,)), pltpu.SemaphoreType.DMA((3,)), pltpu.SemaphoreType.DMA],
    )(parts)


_SHARDED = [("a_norm_pre", 1), ("a_norm_post", 1), ("a_d_skip", 1), ("a_b_glu", 1), ("a_w_in", 2), ("a_w_glu", 1),
            ("a_w_out", 1), ("w_kv", 1), ("b_w_in", 2), ("b_w_out", 1), ("ple_w_proj", 2), ("ple_w_gate", 1)]
_REPLICATED = ["a_lam_re", "a_lam_im", "a_log_dt", "a_b_re", "a_b_im", "a_c_re", "a_c_im", "kv_norm", "b_norm_pre",
               "b_norm_post"]
_WEIGHT_ORDER = ["a_norm_pre", "a_norm_post", "a_w_in", "a_lam_re", "a_lam_im", "a_log_dt", "a_b_re", "a_b_im",
                 "a_c_re", "a_c_im", "a_d_skip", "a_w_glu", "a_b_glu", "a_w_out", "kv_norm", "w_kv", "b_norm_pre",
                 "b_norm_post", "b_w_in", "b_w_out", "ple_w_proj", "ple_w_gate"]


def _round_up(n, m):
    return (n + m - 1) // m * m


def _pack(pieces, rows):
    flat = jnp.concatenate([p.reshape(-1).astype(F32) for p in pieces])
    flat = jnp.pad(flat, (0, rows * SLAB_W - flat.shape[0]))
    return flat.reshape(rows, SLAB_W)


def _unpack(slab, shapes):
    flat = slab.reshape(-1)
    out, off = [], 0
    for s in shapes:
        n = math.prod(s)
        out.append(flat[off:off + n].reshape(s))
        off += n
    return out


def _pick_tile(n, cap):
    return max(t for t in range(SUBLANES, cap + 1, SUBLANES) if n % t == 0)


def _size_rows(shapes, mult):
    return _round_up(_round_up(sum(math.prod(s) for s in shapes), SLAB_W) // SLAB_W, mult)


def _adamw_tile(w, g, m, v):
    m2 = ADAM_B1 * m + (1.0 - ADAM_B1) * g
    v2 = ADAM_B2 * v + (1.0 - ADAM_B2) * (g * g)
    m_hat = m2 / (1.0 - ADAM_B1 ** ADAM_STEP)
    v_hat = v2 / (1.0 - ADAM_B2 ** ADAM_STEP)
    delta = -ADAM_LR * (m_hat / (jnp.sqrt(v_hat) + ADAM_EPS) + ADAM_WD * w)
    return delta, m2, v2


def _forward_backward(x0, p0, p1, tgt, W):
    seq = x0.shape[0]
    sw = D_MODEL
    g = {}

    h1, = _ew(lambda xt, gn: (_rms(xt, gn),), [x0], [W["a_norm_pre"]], [MXU_DTYPE], 0, name="a_pre_norm")
    w_u, w_ga = W["a_w_in"][:, :sw], W["a_w_in"][:, sw:]
    u = _mm(h1, w_u, name="a_in_u")
    gate = _mm(h1, w_ga, name="a_in_gate")
    a_re, a_im, bbt_re, bbt_im, pw_re, pw_im = _s5_params_fwd(W["lam_re"], W["lam_im"], W["log_dt"], W["bt_re"],
                                                               W["bt_im"])
    nb, gpb = N_GROUP_BLOCKS, GROUPS_PER_BLOCK
    bbd = jnp.concatenate([_block_diag(bbt_re.reshape(nb, gpb, GROUP_SIZE, STATE)),
                           _block_diag(bbt_im.reshape(nb, gpb, GROUP_SIZE, STATE))], axis=2).astype(MXU_DTYPE)
    crt = W["c_re"].transpose(0, 2, 1).reshape(nb, gpb, STATE, GROUP_SIZE)
    cit = W["c_im"].transpose(0, 2, 1).reshape(nb, gpb, STATE, GROUP_SIZE)
    cbd = jnp.concatenate([_block_diag(crt), -_block_diag(cit)], axis=1).astype(MXU_DTYPE)
    cbd_t = cbd.transpose(0, 2, 1)
    tab_f, tab_b = _scan_tables(pw_re.reshape(SUBLANES, nb, SSM_LANES), pw_im.reshape(SUBLANES, nb, SSM_LANES))
    ys, hs = _s5_fwd(u, bbd, cbd, tab_f)
    gl, = _ew(lambda a, b, d: (_gelu_skip(a, b, d),), [ys, u], [W["a_d_skip"]], [F32], 0, name="a_gelu")
    t = _mm(gl, W["a_w_glu"], name="a_glu")
    y3, = _ew(lambda a, b, c, d: (_glu_gate(a, b, c, d),), [gl, t, gate], [W["a_b_glu"]], [MXU_DTYPE], 0, name="a_gate")
    y4 = _mm(y3, W["a_w_out"], name="a_out")
    x1, = _ew(lambda a, b, gn: (a + _rms(b, gn),), [x0, y4], [W["a_norm_post"]], [F32], 0, name="a_post_norm")
    gt0 = _mm(x1, W["ple_w_gate"][0], name="ple0_gate")
    pp0 = _mm(p0, W["ple_w_proj"][0], name="ple0_proj")
    x2, hk, h2 = _ew(lambda a, b, c, g1, g2: (_ple(a, b, c), _rms(_ple(a, b, c), g1), _rms(_ple(a, b, c), g2)),
                     [x1, gt0, pp0], [W["kv_norm"], W["b_norm_pre"]], [F32, MXU_DTYPE, MXU_DTYPE], 0, name="ple0_mix")

    w_k, w_v = W["w_kv"][:, :sw], W["w_kv"][:, sw:]
    k = _mm(hk, w_k, out_dtype=MXU_DTYPE, name="kv_k")
    v = _mm(hk, w_v, out_dtype=MXU_DTYPE, name="kv_v")
    w_q, w_gb = W["b_w_in"][:, :sw], W["b_w_in"][:, sw:]
    q = _mm(h2, w_q, out_dtype=MXU_DTYPE, name="b_in_q")
    gate2 = _mm(h2, w_gb, name="b_in_gate")
    o, tot = _sba_fwd(q, k, v)
    y5in, = _ew(lambda a, b: (_ogate(a, b),), [o, gate2], [], [MXU_DTYPE], 0, name="b_gate")
    y5 = _mm(y5in, W["b_w_out"], name="b_out")
    x3, = _ew(lambda a, b, gn: (a + _rms(b, gn),), [x2, y5], [W["b_norm_post"]], [F32], 0, name="b_post_norm")
    gt1 = _mm(x3, W["ple_w_gate"][1], name="ple1_gate")
    pp1 = _mm(p1, W["ple_w_proj"][1], name="ple1_proj")

    def loss_fn(xt, gt, pp, tg):
        s = jax.nn.sigmoid(gt)
        d = (xt + s * pp - tg) * (1.0 / D_MODEL)
        lsum = jnp.sum(d * d, axis=(0, 1), keepdims=True) * (0.5 * D_MODEL)
        return d, d * pp * s * (1.0 - s), d * s, lsum

    dx3a, dgt1, dpp1, loss = _ew(loss_fn, [x3, gt1, pp1, tgt], [], [F32, MXU_DTYPE, MXU_DTYPE], 1, name="loss_head")
    g_gate1 = _mm(x3, dgt1, ta=True, name="ple1_gate_dw")
    g_proj1 = _mm(p1, dpp1, ta=True, name="ple1_proj_dw")
    dx3 = _mm(dgt1, W["ple_w_gate"][1], tb=True, add=dx3a, name="ple1_gate_dx")

    def post_norm_bwd(ct, yt, gn):
        _, vjp = jax.vjp(_rms, yt, gn)
        dy, dg = vjp(ct)
        return dy, dg

    dy5, g["b_norm_post"] = _ew(post_norm_bwd, [dx3, y5], [W["b_norm_post"]], [MXU_DTYPE], 1, name="b_post_norm_bwd")
    g["b_w_out"] = _mm(y5in, dy5, ta=True, name="b_out_dw")
    dy5in = _mm(dy5, W["b_w_out"], tb=True, name="b_out_dx")

    def ogate_bwd(ct, ot, gt):
        _, vjp = jax.vjp(_ogate, ot, gt)
        return vjp(ct)

    do, dgate2 = _ew(ogate_bwd, [dy5in, o, gate2], [], [MXU_DTYPE, MXU_DTYPE], 0, name="b_gate_bwd")
    dq, dk, dv = _sba_bwd(q, k, v, do, tot)
    g["b_w_in"] = jnp.concatenate([_mm(h2, dq, ta=True, name="b_in_q_dw"),
                                   _mm(h2, dgate2, ta=True, name="b_in_gate_dw")], axis=1)
    dh2 = _mm(dgate2, w_gb, tb=True, add=_mm(dq, w_q, tb=True, name="b_in_q_dx"), name="b_in_gate_dx")
    g["w_kv"] = jnp.concatenate([_mm(hk, dk, ta=True, name="kv_k_dw"), _mm(hk, dv, ta=True, name="kv_v_dw")], axis=1)
    dhk = _mm(dv, w_v, tb=True, add=_mm(dk, w_k, tb=True, name="kv_k_dx"), name="kv_v_dx")

    def mix_bwd(ct, c2, ck, xt, gt, pp, g1, g2):
        x2v, vjp_ple = jax.vjp(_ple, xt, gt, pp)
        _, vjp_k = jax.vjp(_rms, x2v, g1)
        _, vjp_b = jax.vjp(_rms, x2v, g2)
        dxk, dg1 = vjp_k(ck)
        dxb, dg2 = vjp_b(c2)
        dx2 = ct + dxk + dxb
        dx1, dgt, dpp = vjp_ple(dx2)
        return dx1, dgt, dpp, dg1, dg2

    dx1a, dgt0, dpp0, g["kv_norm"], g["b_norm_pre"] = _ew(
        mix_bwd, [dx3, dh2, dhk, x1, gt0, pp0], [W["kv_norm"], W["b_norm_pre"]], [F32, MXU_DTYPE, MXU_DTYPE], 2,
        name="ple0_mix_bwd")
    g_gate0 = _mm(x1, dgt0, ta=True, name="ple0_gate_dw")
    g_proj0 = _mm(p0, dpp0, ta=True, name="ple0_proj_dw")
    dx1 = _mm(dgt0, W["ple_w_gate"][0], tb=True, add=dx1a, name="ple0_gate_dx")
    g["ple_w_gate"] = jnp.stack([g_gate0, g_gate1])
    g["ple_w_proj"] = jnp.stack([g_proj0, g_proj1])

    dy4, g["a_norm_post"] = _ew(post_norm_bwd, [dx1, y4], [W["a_norm_post"]], [MXU_DTYPE], 1, name="a_post_norm_bwd")
    g["a_w_out"] = _mm(y3, dy4, ta=True, name="a_out_dw")
    dy3 = _mm(dy4, W["a_w_out"], tb=True, name="a_out_dx")

    def gate_bwd(ct, gt_, tt_, gat, bg):
        _, vjp = jax.vjp(_glu_gate, gt_, tt_, gat, bg)
        dg_, dt_, dgate_, dbg = vjp(ct)
        return dg_, dt_, dgate_, dbg

    dgl_a, dt, dgate, g["a_b_glu"] = _ew(gate_bwd, [dy3, gl, t, gate], [W["a_b_glu"]], [F32, MXU_DTYPE, MXU_DTYPE], 1,
                                         name="a_gate_bwd")
    g["a_w_glu"] = _mm(gl, dt, ta=True, name="a_glu_dw")
    dgl = _mm(dt, W["a_w_glu"], tb=True, add=dgl_a, name="a_glu_dx")

    def gelu_bwd(ct, yt, ut, ds):
        _, vjp = jax.vjp(_gelu_skip, yt, ut, ds)
        return vjp(ct)

    dys, du_a, g["a_d_skip"] = _ew(gelu_bwd, [dgl, ys, u], [W["a_d_skip"]], [MXU_DTYPE, F32], 1, name="a_gelu_bwd")
    du, d_bbd, d_cbd, d_ar8, d_ai8 = _s5_bwd(u, dys, du_a, hs, bbd, cbd_t, tab_f, tab_b)
    g["a_w_in"] = jnp.concatenate([_mm(h1, du, ta=True, name="a_in_u_dw"),
                                   _mm(h1, dgate, ta=True, name="a_in_gate_dw")], axis=1)
    dh1 = _mm(dgate, w_ga, tb=True, add=_mm(du, w_u, tb=True, name="a_in_u_dx"), name="a_in_gate_dx")

    def pre_norm_bwd(ct, ch, xt, gn):
        _, vjp = jax.vjp(_rms, xt, gn)
        dx, dg = vjp(ch)
        return ct + dx, dg

    dx0, g["a_norm_pre"] = _ew(pre_norm_bwd, [dx1, dh1, x0], [W["a_norm_pre"]], [F32], 1, name="a_pre_norm_bwd")

    w = SSM_LANES
    d_bbt_re = _block_diag_extract(d_bbd[:, :, :w], GROUP_SIZE, STATE).reshape(N_GROUPS, GROUP_SIZE, STATE)
    d_bbt_im = _block_diag_extract(d_bbd[:, :, w:], GROUP_SIZE, STATE).reshape(N_GROUPS, GROUP_SIZE, STATE)
    d_crt = _block_diag_extract(d_cbd[:, :w, :], STATE, GROUP_SIZE).reshape(N_GROUPS, STATE, GROUP_SIZE)
    d_cit = -_block_diag_extract(d_cbd[:, w:, :], STATE, GROUP_SIZE).reshape(N_GROUPS, STATE, GROUP_SIZE)
    g["a_c_re"] = d_crt.transpose(0, 2, 1)[None]
    g["a_c_im"] = d_cit.transpose(0, 2, 1)[None]
    d_lr, d_li, d_ldt, d_btr, d_bti = _s5_params_bwd(
        W["lam_re"], W["lam_im"], W["log_dt"], W["bt_re"], W["bt_im"], _per_sublane(d_ar8), _per_sublane(d_ai8),
        d_bbt_re, d_bbt_im)
    g["a_lam_re"], g["a_lam_im"] = d_lr[None], d_li[None]
    g["a_log_dt"] = d_ldt.reshape(1, N_GROUPS)
    g["a_b_re"] = d_btr.transpose(0, 2, 1)[None]
    g["a_b_im"] = d_bti.transpose(0, 2, 1)[None]
    return loss, dx0, g


def kernel(x, p, a_norm_pre, a_norm_post, a_w_in, a_lam_re, a_lam_im, a_log_dt, a_b_re, a_b_im, a_c_re, a_c_im, a_d_skip, a_w_glu, a_b_glu, a_w_out, kv_norm, w_kv, b_norm_pre, b_norm_post, b_w_in, b_w_out, ple_w_proj, ple_w_gate, loss_target, m_a_norm_pre, m_a_norm_post, m_a_w_in, m_a_lam_re, m_a_lam_im, m_a_log_dt, m_a_b_re, m_a_b_im, m_a_c_re, m_a_c_im, m_a_d_skip, m_a_w_glu, m_a_b_glu, m_a_w_out, m_kv_norm, m_w_kv, m_b_norm_pre, m_b_norm_post, m_b_w_in, m_b_w_out, m_ple_w_proj, m_ple_w_gate, v_a_norm_pre, v_a_norm_post, v_a_w_in, v_a_lam_re, v_a_lam_im, v_a_log_dt, v_a_b_re, v_a_b_im, v_a_c_re, v_a_c_im, v_a_d_skip, v_a_w_glu, v_a_b_glu, v_a_w_out, v_kv_norm, v_w_kv, v_b_norm_pre, v_b_norm_post, v_b_w_in, v_b_w_out, v_ple_w_proj, v_ple_w_gate):
    loc = dict(a_norm_pre=a_norm_pre, a_norm_post=a_norm_post, a_w_in=a_w_in, a_lam_re=a_lam_re, a_lam_im=a_lam_im,
               a_log_dt=a_log_dt, a_b_re=a_b_re, a_b_im=a_b_im, a_c_re=a_c_re, a_c_im=a_c_im, a_d_skip=a_d_skip,
               a_w_glu=a_w_glu, a_b_glu=a_b_glu, a_w_out=a_w_out, kv_norm=kv_norm, w_kv=w_kv, b_norm_pre=b_norm_pre,
               b_norm_post=b_norm_post, b_w_in=b_w_in, b_w_out=b_w_out, ple_w_proj=ple_w_proj, ple_w_gate=ple_w_gate)
    mom = dict(a_norm_pre=m_a_norm_pre, a_norm_post=m_a_norm_post, a_w_in=m_a_w_in, a_lam_re=m_a_lam_re,
               a_lam_im=m_a_lam_im, a_log_dt=m_a_log_dt, a_b_re=m_a_b_re, a_b_im=m_a_b_im, a_c_re=m_a_c_re,
               a_c_im=m_a_c_im, a_d_skip=m_a_d_skip, a_w_glu=m_a_w_glu, a_b_glu=m_a_b_glu, a_w_out=m_a_w_out,
               kv_norm=m_kv_norm, w_kv=m_w_kv, b_norm_pre=m_b_norm_pre, b_norm_post=m_b_norm_post, b_w_in=m_b_w_in,
               b_w_out=m_b_w_out, ple_w_proj=m_ple_w_proj, ple_w_gate=m_ple_w_gate)
    var = dict(a_norm_pre=v_a_norm_pre, a_norm_post=v_a_norm_post, a_w_in=v_a_w_in, a_lam_re=v_a_lam_re,
               a_lam_im=v_a_lam_im, a_log_dt=v_a_log_dt, a_b_re=v_a_b_re, a_b_im=v_a_b_im, a_c_re=v_a_c_re,
               a_c_im=v_a_c_im, a_d_skip=v_a_d_skip, a_w_glu=v_a_w_glu, a_b_glu=v_a_b_glu, a_w_out=v_a_w_out,
               kv_norm=v_kv_norm, w_kv=v_w_kv, b_norm_pre=v_b_norm_pre, b_norm_post=v_b_norm_post, b_w_in=v_b_w_in,
               b_w_out=v_b_w_out, ple_w_proj=v_ple_w_proj, ple_w_gate=v_ple_w_gate)
    ci = lax.axis_index("c")

    vec_names = [n for n, _ in _SHARDED if loc[n].shape[-2] == 1 and loc[n].ndim == 2]
    mat_names = [n for n, _ in _SHARDED if n not in vec_names]
    axis_of = dict(_SHARDED)
    mat_shapes = [loc[n].shape for n in mat_names]
    vec_shapes = [loc[n].shape for n in vec_names]
    mat_rows = _size_rows(mat_shapes, 16)
    vec_rows = _size_rows(vec_shapes, 8)
    mat_slab = _pack([loc[n] for n in mat_names], mat_rows).astype(MXU_DTYPE)
    vec_slab = _pack([loc[n] for n in vec_names], vec_rows)
    mat_all, vec_all = _all_gather_chips([mat_slab, vec_slab], name="gather_weights")
    W = {}
    for names, slab_all, shapes in ((mat_names, mat_all, mat_shapes), (vec_names, vec_all, vec_shapes)):
        per_chip = [_unpack(slab_all[j], shapes) for j in range(4)]
        for i, n in enumerate(names):
            W[n] = jnp.concatenate([per_chip[j][i] for j in range(4)], axis=axis_of[n])
    for n in ("a_w_in", "a_w_glu", "a_w_out", "b_w_in", "b_w_out"):
        W[n] = W[n][0]
    W["kv_norm"] = kv_norm.reshape(1, D_MODEL)
    W["b_norm_pre"], W["b_norm_post"] = b_norm_pre, b_norm_post
    W["lam_re"], W["lam_im"] = a_lam_re[0], a_lam_im[0]
    W["log_dt"] = a_log_dt.reshape(N_GROUPS, 1)
    W["bt_re"], W["bt_im"] = a_b_re[0].transpose(0, 2, 1), a_b_im[0].transpose(0, 2, 1)
    W["c_re"], W["c_im"] = a_c_re[0], a_c_im[0]

    loss, dx0, g = _forward_backward(x[0], p[0, 0], p[1, 0], loss_target[0], W)
    for n in ("a_w_in", "a_w_glu", "a_w_out", "b_w_in", "b_w_out"):
        g[n] = g[n][None]
    g["kv_norm"] = g["kv_norm"].reshape(D_MODEL)
    loss = lax.psum(loss[0, 0], ("x", "y", "c"))

    rep_shapes = [loc[n].shape for n in _REPLICATED]
    rep_rows = _size_rows(rep_shapes, 32)
    rep_slab = _pack([g[n] for n in _REPLICATED], rep_rows).reshape(4, rep_rows // 4, SLAB_W)
    sh_names = [n for n, _ in _SHARDED]
    sh_shapes = [loc[n].shape for n in sh_names]
    sh_rows = _size_rows(sh_shapes, 8)
    rows = _round_up(sh_rows + rep_rows // 4, 1024)
    half = rows // 2
    slabs = []
    for j in range(4):
        pieces = [jnp.split(g[n], 4, axis=axis_of[n])[j] for n in sh_names]
        sh = _pack(pieces, sh_rows)
        slabs.append(jnp.concatenate([sh, rep_slab[j], jnp.zeros((rows - sh_rows - rep_rows // 4, SLAB_W), F32)]))
    gs = jnp.stack(slabs).reshape(4, 2, half, SLAB_W)
    mine = lax.dynamic_index_in_dim(gs, ci, axis=1, keepdims=False).reshape(4 * half, SLAB_W)
    theirs = lax.dynamic_index_in_dim(gs, 1 - ci, axis=1, keepdims=False).reshape(4 * half, SLAB_W)
    got = _sibling_exchange(theirs, name="reduce_cores")
    part, = _ew(lambda a, b: (a + b,), [mine, got], [], [F32], 0, name="reduce_cores_add", tr=512)
    recv = _all_to_all_chips(part.reshape(4, half, SLAB_W), name="reduce_chips")
    tot_half, = _ew(lambda a, b, c, d: (((a + b) + c) + d,), [recv[0], recv[1], recv[2], recv[3]], [], [F32], 0,
                    name="reduce_chips_add", tr=512)
    other_half = _sibling_exchange(tot_half, name="share_cores")
    lo_half = jnp.where(ci == 0, tot_half, other_half)
    hi_half = jnp.where(ci == 0, other_half, tot_half)
    gsum = jnp.concatenate([lo_half, hi_half])
    rep_q = gsum[sh_rows:sh_rows + rep_rows // 4]
    rep_all, = _all_gather_chips([rep_q], name="gather_replicated")
    gsum_full = jnp.concatenate([gsum[:sh_rows], rep_all.reshape(rep_rows, SLAB_W)])

    all_names = sh_names + _REPLICATED
    all_shapes = sh_shapes + rep_shapes

    def slab_of(d):
        return jnp.concatenate([_pack([d[n] for n in sh_names], sh_rows), _pack([d[n] for n in _REPLICATED], rep_rows)])

    n_rows = sh_rows + rep_rows
    outs = _ew(_adamw_tile, [slab_of(loc), gsum_full, slab_of(mom), slab_of(var)], [], [F32] * 3, 0, name="adamw",
               tr=_pick_tile(n_rows, 512))
    res = []
    for slab in [gsum_full] + list(outs):
        parts = _unpack(slab[:sh_rows], sh_shapes) + _unpack(slab[sh_rows:], rep_shapes)
        res.append(dict(zip(all_names, parts)))
    out = [loss, dx0[None]]
    for d in res:
        out += [d[n] for n in _WEIGHT_ORDER]
    return tuple(out)
```

```python
import functools
import math

import jax
import jax.numpy as jnp
from jax import lax
from jax.experimental import pallas as pl
from jax.experimental.pallas import tpu as pltpu

F32 = jnp.float32
MXU_DTYPE = jnp.bfloat16

D_MODEL = 1024
N_GROUPS = 64
GROUP_SIZE = 16
STATE = 64
GROUPS_PER_BLOCK = 8
SSM_LANES = GROUPS_PER_BLOCK * STATE
N_GROUP_BLOCKS = N_GROUPS // GROUPS_PER_BLOCK
HEAD_DIM = 64
HEADS_PER_BLOCK = 2
KEY_BLOCKS_PER_STEP = 4
SCAN_TILES_PER_STEP = 2
EPS = 1e-6
SLAB_W = 512
SUBLANES = 8
VMEM_LIMIT = 56 * 1024 * 1024

ADAM_LR = 0.001
ADAM_B1 = 0.9
ADAM_B2 = 0.999
ADAM_EPS = 1e-08
ADAM_WD = 0.01
ADAM_STEP = 10

MESH = pl.DeviceIdType.MESH
ANY = pl.BlockSpec(memory_space=pl.ANY)


def _cparams(sem):
    return pltpu.CompilerParams(dimension_semantics=sem, vmem_limit_bytes=VMEM_LIMIT)


def _mm(a, b, *, ta=False, tb=False, add=None, out_dtype=F32, name, tm=1024, tn=1024, tk=1024):
    if ta:
        kdim, m = a.shape
    else:
        m, kdim = a.shape
    if tb:
        n, kb = b.shape
    else:
        kb, n = b.shape
    assert kdim == kb, (a.shape, b.shape, ta, tb)
    tm, tn, tk = min(tm, m), min(tn, n), min(tk, kdim)
    assert m % tm == 0 and n % tn == 0 and kdim % tk == 0
    nk = kdim // tk
    dims = (((0 if ta else 1,), (1 if tb else 0,)), ((), ()))
    has_add = add is not None

    def body(*refs):
        if has_add:
            a_ref, b_ref, add_ref, o_ref, acc_ref = refs
        else:
            a_ref, b_ref, o_ref, acc_ref = refs
        k = pl.program_id(2)

        @pl.when(k == 0)
        def _():
            acc_ref[...] = jnp.zeros_like(acc_ref)

        acc_ref[...] += lax.dot_general(a_ref[...].astype(MXU_DTYPE), b_ref[...].astype(MXU_DTYPE), dims,
                                        preferred_element_type=F32)

        @pl.when(k == nk - 1)
        def _():
            r = acc_ref[...]
            if has_add:
                r = r + add_ref[...].astype(F32)
            o_ref[...] = r.astype(o_ref.dtype)

    a_spec = pl.BlockSpec((tk, tm), lambda i, j, k: (k, i)) if ta else pl.BlockSpec((tm, tk), lambda i, j, k: (i, k))
    b_spec = pl.BlockSpec((tn, tk), lambda i, j, k: (j, k)) if tb else pl.BlockSpec((tk, tn), lambda i, j, k: (k, j))
    o_spec = pl.BlockSpec((tm, tn), lambda i, j, k: (i, j))
    in_specs = [a_spec, b_spec] + ([o_spec] if has_add else [])
    args = (a, b) + ((add,) if has_add else ())
    return pl.pallas_call(
        body, name=name, grid=(m // tm, n // tn, nk), in_specs=in_specs, out_specs=o_spec,
        out_shape=jax.ShapeDtypeStruct((m, n), out_dtype), scratch_shapes=[pltpu.VMEM((tm, tn), F32)],
        compiler_params=_cparams(("parallel", "parallel", "arbitrary")),
    )(*args)


def _ew(fn, rows, params, row_dtypes, n_acc, *, name, tr=256):
    nrows = rows[0].shape[0]
    tr = min(tr, nrows)
    assert nrows % tr == 0
    n_in, n_par, n_row = len(rows), len(params), len(row_dtypes)
    tile_avals = [jax.ShapeDtypeStruct((tr, r.shape[1]), F32) for r in rows]
    par_avals = [jax.ShapeDtypeStruct(q.shape, F32) for q in params]
    out_avals = jax.eval_shape(fn, *tile_avals, *par_avals)
    assert len(out_avals) == n_row + n_acc

    def body(*refs):
        ins = [r[...].astype(F32) for r in refs[:n_in + n_par]]
        outs = fn(*ins)
        row_refs = refs[n_in + n_par:n_in + n_par + n_row]
        acc_refs = refs[n_in + n_par + n_row:]
        for r, o in zip(row_refs, outs[:n_row]):
            r[...] = o.astype(r.dtype)
        if n_acc:
            @pl.when(pl.program_id(0) == 0)
            def _():
                for r in acc_refs:
                    r[...] = jnp.zeros_like(r)
            for r, o in zip(acc_refs, outs[n_row:]):
                r[...] += o

    in_specs = [pl.BlockSpec((tr, r.shape[1]), lambda i: (i, 0)) for r in rows]
    in_specs += [pl.BlockSpec(q.shape, lambda i: (0, 0)) for q in params]
    out_specs = [pl.BlockSpec((tr, o.shape[1]), lambda i: (i, 0)) for o in out_avals[:n_row]]
    out_specs += [pl.BlockSpec(o.shape, lambda i: (0, 0)) for o in out_avals[n_row:]]
    out_shape = [jax.ShapeDtypeStruct((nrows, o.shape[1]), dt) for o, dt in zip(out_avals[:n_row], row_dtypes)]
    out_shape += [jax.ShapeDtypeStruct(o.shape, F32) for o in out_avals[n_row:]]
    return pl.pallas_call(
        body, name=name, grid=(nrows // tr,), in_specs=in_specs, out_specs=out_specs, out_shape=out_shape,
        compiler_params=_cparams(("arbitrary",)),
    )(*rows, *params)


def _rms(xt, g):
    r = lax.rsqrt(jnp.mean(xt * xt, axis=-1, keepdims=True) + EPS)
    return xt * r * g


def _gelu_skip(ys, u, dskip):
    return jax.nn.gelu(ys + dskip * u)


def _glu_gate(g, t, gate, bglu):
    return g * jax.nn.sigmoid(t + bglu) * jax.nn.silu(gate)


def _ogate(o, gate):
    return o * jax.nn.silu(gate)


def _ple(xt, gt, pp):
    return xt + jax.nn.sigmoid(gt) * pp


def _s5_disc(lam_re, lam_im, log_dt, bt_re, bt_im):
    lr = jnp.minimum(lam_re, -1e-4)
    li = lam_im
    dt = jnp.exp(log_dt)
    mag = jnp.exp(lr * dt)
    a_re = mag * jnp.cos(li * dt)
    a_im = mag * jnp.sin(li * dt)
    den = lr * lr + li * li
    nr = a_re - 1.0
    f_re = (nr * lr + a_im * li) / den
    f_im = (a_im * lr - nr * li) / den
    bb_re = f_re[:, None, :] * bt_re - f_im[:, None, :] * bt_im
    bb_im = f_re[:, None, :] * bt_im + f_im[:, None, :] * bt_re
    return a_re, a_im, bb_re, bb_im


def _s5_params_fwd(lam_re, lam_im, log_dt, bt_re, bt_im):
    def body(lr_ref, li_ref, dt_ref, br_ref, bi_ref, ar_ref, ai_ref, bbr_ref, bbi_ref, pr_ref, pi_ref):
        a_re, a_im, bb_re, bb_im = _s5_disc(lr_ref[...], li_ref[...], dt_ref[...], br_ref[...], bi_ref[...])
        ar_ref[...] = a_re
        ai_ref[...] = a_im
        bbr_ref[...] = bb_re
        bbi_ref[...] = bb_im
        pr, pi = a_re, a_im
        for t in range(SUBLANES):
            pr_ref[t] = pr
            pi_ref[t] = pi
            pr, pi = pr * a_re - pi * a_im, pr * a_im + pi * a_re

    g, s = lam_re.shape
    sd = jax.ShapeDtypeStruct
    return pl.pallas_call(
        body, name="s5_params_fwd",
        out_shape=[sd((g, s), F32), sd((g, s), F32), sd(bt_re.shape, F32), sd(bt_re.shape, F32),
                   sd((SUBLANES, g, s), F32), sd((SUBLANES, g, s), F32)],
    )(lam_re, lam_im, log_dt, bt_re, bt_im)


def _s5_params_bwd(lam_re, lam_im, log_dt, bt_re, bt_im, d_ar, d_ai, d_bbr, d_bbi):
    def body(lr_ref, li_ref, dt_ref, br_ref, bi_ref, c0, c1, c2, c3, o0, o1, o2, o3, o4):
        _, vjp = jax.vjp(_s5_disc, lr_ref[...], li_ref[...], dt_ref[...], br_ref[...], bi_ref[...])
        grads = vjp((jnp.sum(c0[...], axis=0), jnp.sum(c1[...], axis=0), c2[...], c3[...]))
        for o, gval in zip((o0, o1, o2, o3, o4), grads):
            o[...] = gval

    sd = jax.ShapeDtypeStruct
    return pl.pallas_call(
        body, name="s5_params_bwd",
        out_shape=[sd(lam_re.shape, F32), sd(lam_im.shape, F32), sd(log_dt.shape, F32), sd(bt_re.shape, F32),
                   sd(bt_im.shape, F32)],
    )(lam_re, lam_im, log_dt, bt_re, bt_im, d_ar, d_ai, d_bbr, d_bbi)


def _block_diag(t):
    nb, g, r, c = t.shape
    eye = jnp.eye(g, dtype=t.dtype)
    return jnp.einsum("ngrc,gh->ngrhc", t, eye).reshape(nb, g * r, g * c)


def _block_diag_extract(m, r, c):
    nb = m.shape[0]
    g = GROUPS_PER_BLOCK
    eye = jnp.eye(g, dtype=m.dtype)[None, :, None, :, None]
    return jnp.sum(m.reshape(nb, g, r, g, c) * eye, axis=3)


def _per_sublane(acc):
    return acc.reshape(N_GROUP_BLOCKS, SUBLANES, GROUPS_PER_BLOCK, STATE).transpose(1, 0, 2, 3).reshape(
        SUBLANES, N_GROUPS, STATE)


def _scan_tables(pr, pi):
    row = jnp.arange(SUBLANES)[:, None, None]

    def tables(im_sign, fwd):
        kinds = []
        for k in (1, 2, 4):
            mask = (row >= k) if fwd else (row < SUBLANES - k)
            kinds.append(jnp.where(mask, pr[k - 1][None], 0.0))
            kinds.append(jnp.where(mask, im_sign * pi[k - 1][None], 0.0))
        if fwd:
            kinds += [pr, im_sign * pi]
        else:
            kinds += [pr[::-1], im_sign * pi[::-1]]
        return jnp.stack(kinds, axis=0).transpose(2, 0, 1, 3)

    return tables(1.0, True), tables(-1.0, False)


def _scan_rows(buf_ref, tab_ref, carry_r, carry_i, n_tiles, reverse):
    w = SSM_LANES
    assert n_tiles % SCAN_TILES_PER_STEP == 0

    def local_scan(jj):
        rows = pl.ds(pl.multiple_of(jj * SUBLANES, SUBLANES), SUBLANES)
        zr = buf_ref[rows, 0:w]
        zi = buf_ref[rows, w:2 * w]
        for n, k in enumerate((1, 2, 4)):
            ar = tab_ref[2 * n]
            ai = tab_ref[2 * n + 1]
            sh = (SUBLANES - k) if reverse else k
            rr = pltpu.roll(zr, sh, 0)
            ri = pltpu.roll(zi, sh, 0)
            zr, zi = zr + ar * rr - ai * ri, zi + ar * ri + ai * rr
        return rows, zr, zi

    def step(j, carry):
        first = j * SCAN_TILES_PER_STEP
        tiles = [local_scan((n_tiles - 1 - first - d) if reverse else (first + d)) for d in range(SCAN_TILES_PER_STEP)]
        cr, ci = carry
        pr = tab_ref[6]
        pi = tab_ref[7]
        for rows, zr, zi in tiles:
            zr, zi = zr + pr * cr - pi * ci, zi + pr * ci + pi * cr
            buf_ref[rows, 0:w] = zr
            buf_ref[rows, w:2 * w] = zi
            cr, ci = (zr[0:1], zi[0:1]) if reverse else (zr[SUBLANES - 1:SUBLANES], zi[SUBLANES - 1:SUBLANES])
        return cr, ci

    return lax.fori_loop(0, n_tiles // SCAN_TILES_PER_STEP, step, (carry_r, carry_i))


def _s5_fwd(u, bbd, cbd, tab_f, *, tt=512):
    seq = u.shape[0]
    tt = min(tt, seq)
    nt = seq // tt
    w = SSM_LANES

    def body(u_ref, b_ref, c_ref, tab_ref, y_ref, hs_ref, buf_ref, h_ref):
        it = pl.program_id(1)

        @pl.when(it == 0)
        def _():
            h_ref[...] = jnp.zeros_like(h_ref)

        hs_ref[...] = h_ref[...]
        buf_ref[...] = jnp.dot(u_ref[...].astype(MXU_DTYPE), b_ref[...], preferred_element_type=F32)
        hr, hi = _scan_rows(buf_ref, tab_ref, h_ref[:, 0:w], h_ref[:, w:2 * w], tt // SUBLANES, False)
        h_ref[:, 0:w] = hr
        h_ref[:, w:2 * w] = hi
        y_ref[...] = jnp.dot(buf_ref[...].astype(MXU_DTYPE), c_ref[...], preferred_element_type=F32)

    return pl.pallas_call(
        body, name="s5_fwd", grid=(N_GROUP_BLOCKS, nt),
        in_specs=[pl.BlockSpec((tt, 128), lambda g, t: (t, g)),
                  pl.BlockSpec((None, 128, 2 * w), lambda g, t: (g, 0, 0)),
                  pl.BlockSpec((None, 2 * w, 128), lambda g, t: (g, 0, 0)),
                  pl.BlockSpec((None, 8, SUBLANES, w), lambda g, t: (g, 0, 0, 0))],
        out_specs=[pl.BlockSpec((tt, 128), lambda g, t: (t, g)),
                   pl.BlockSpec((None, None, 1, 2 * w), lambda g, t: (g, t, 0, 0))],
        out_shape=[jax.ShapeDtypeStruct((seq, D_MODEL), F32),
                   jax.ShapeDtypeStruct((N_GROUP_BLOCKS, nt, 1, 2 * w), F32)],
        scratch_shapes=[pltpu.VMEM((tt, 2 * w), F32), pltpu.VMEM((1, 2 * w), F32)],
        compiler_params=_cparams(("parallel", "arbitrary")),
    )(u, bbd, cbd, tab_f)


def _s5_bwd(u, dy, du_add, hs, bbd, cbd_t, tab_f, tab_b, *, tt=512):
    seq = u.shape[0]
    tt = min(tt, seq)
    nt = seq // tt
    w = SSM_LANES
    n_tiles = tt // SUBLANES

    def body(u_ref, dy_ref, dua_ref, hs_ref, b_ref, ct_ref, tabf_ref, tabb_ref,
             du_ref, db_ref, dc_ref, dar_ref, dai_ref, s_ref, l_ref, lam_ref):
        it = pl.program_id(1)

        @pl.when(it == 0)
        def _():
            lam_ref[...] = jnp.zeros_like(lam_ref)
            db_ref[...] = jnp.zeros_like(db_ref)
            dc_ref[...] = jnp.zeros_like(dc_ref)
            dar_ref[...] = jnp.zeros_like(dar_ref)
            dai_ref[...] = jnp.zeros_like(dai_ref)

        ub = u_ref[...].astype(MXU_DTYPE)
        dyb = dy_ref[...].astype(MXU_DTYPE)
        s_ref[...] = jnp.dot(ub, b_ref[...], preferred_element_type=F32)
        h_in_r = hs_ref[:, 0:w]
        h_in_i = hs_ref[:, w:2 * w]
        _scan_rows(s_ref, tabf_ref, h_in_r, h_in_i, n_tiles, False)
        dc_ref[...] += lax.dot_general(s_ref[...].astype(MXU_DTYPE), dyb, (((0,), (0,)), ((), ())),
                                       preferred_element_type=F32)
        l_ref[...] = jnp.dot(dyb, ct_ref[...], preferred_element_type=F32)
        lr, li = _scan_rows(l_ref, tabb_ref, lam_ref[:, 0:w], lam_ref[:, w:2 * w], n_tiles, True)
        lam_ref[:, 0:w] = lr
        lam_ref[:, w:2 * w] = li
        lb = l_ref[...].astype(MXU_DTYPE)
        db_ref[...] += lax.dot_general(ub, lb, (((0,), (0,)), ((), ())), preferred_element_type=F32)
        du = lax.dot_general(lb, b_ref[...], (((1,), (1,)), ((), ())), preferred_element_type=F32)
        du_ref[...] = (du + dua_ref[...].astype(F32)).astype(du_ref.dtype)

        def tile(j, carry):
            pr, pi, accr, acci = carry
            rows = pl.ds(pl.multiple_of(j * SUBLANES, SUBLANES), SUBLANES)
            sr = s_ref[rows, 0:w]
            si = s_ref[rows, w:2 * w]
            first = lax.broadcasted_iota(jnp.int32, (SUBLANES, w), 0) == 0
            sr_prev = jnp.where(first, pr, pltpu.roll(sr, 1, 0))
            si_prev = jnp.where(first, pi, pltpu.roll(si, 1, 0))
            gr = l_ref[rows, 0:w]
            gi = l_ref[rows, w:2 * w]
            accr = accr + gr * sr_prev + gi * si_prev
            acci = acci + gi * sr_prev - gr * si_prev
            return sr[SUBLANES - 1:SUBLANES], si[SUBLANES - 1:SUBLANES], accr, acci

        zero = jnp.zeros((SUBLANES, w), F32)
        _, _, accr, acci = lax.fori_loop(0, n_tiles, tile, (h_in_r, h_in_i, zero, zero))
        dar_ref[...] += accr
        dai_ref[...] += acci

    rev = lambda g, t: (nt - 1 - t, g)
    return pl.pallas_call(
        body, name="s5_bwd", grid=(N_GROUP_BLOCKS, nt),
        in_specs=[pl.BlockSpec((tt, 128), rev), pl.BlockSpec((tt, 128), rev), pl.BlockSpec((tt, 128), rev),
                  pl.BlockSpec((None, None, 1, 2 * w), lambda g, t: (g, nt - 1 - t, 0, 0)),
                  pl.BlockSpec((None, 128, 2 * w), lambda g, t: (g, 0, 0)),
                  pl.BlockSpec((None, 128, 2 * w), lambda g, t: (g, 0, 0)),
                  pl.BlockSpec((None, 8, SUBLANES, w), lambda g, t: (g, 0, 0, 0)),
                  pl.BlockSpec((None, 8, SUBLANES, w), lambda g, t: (g, 0, 0, 0))],
        out_specs=[pl.BlockSpec((tt, 128), rev),
                   pl.BlockSpec((None, 128, 2 * w), lambda g, t: (g, 0, 0)),
                   pl.BlockSpec((None, 2 * w, 128), lambda g, t: (g, 0, 0)),
                   pl.BlockSpec((None, SUBLANES, w), lambda g, t: (g, 0, 0)),
                   pl.BlockSpec((None, SUBLANES, w), lambda g, t: (g, 0, 0))],
        out_shape=[jax.ShapeDtypeStruct((seq, D_MODEL), MXU_DTYPE),
                   jax.ShapeDtypeStruct((N_GROUP_BLOCKS, 128, 2 * w), F32),
                   jax.ShapeDtypeStruct((N_GROUP_BLOCKS, 2 * w, 128), F32),
                   jax.ShapeDtypeStruct((N_GROUP_BLOCKS, SUBLANES, w), F32),
                   jax.ShapeDtypeStruct((N_GROUP_BLOCKS, SUBLANES, w), F32)],
        scratch_shapes=[pltpu.VMEM((tt, 2 * w), F32), pltpu.VMEM((tt, 2 * w), F32), pltpu.VMEM((1, 2 * w), F32)],
        compiler_params=_cparams(("parallel", "arbitrary")),
    )(u, dy, du_add, hs, bbd, cbd_t, tab_f, tab_b)


LOG2E = 1.4426950408889634
LN2 = 0.6931471805599453
SOFTPLUS2_LINEAR = 28.0


def _softplus2(z):
    u = 1.0 + jnp.exp2(z)
    return jnp.where(z > SOFTPLUS2_LINEAR, z, jnp.log2(u)), u


def _tri(n, keep, value=1.0):
    row = lax.broadcasted_iota(jnp.int32, (n, n), 0)
    col = lax.broadcasted_iota(jnp.int32, (n, n), 1)
    return jnp.where(keep(row, col), value, 0.0).astype(MXU_DTYPE)


def _suffix_sums(x, tri):
    c = tri.shape[0]
    outs, carry = [], None
    for i in reversed(range(x.shape[1] // c)):
        part = jnp.dot(x[:, i * c:(i + 1) * c].astype(MXU_DTYPE), tri, preferred_element_type=F32)
        if carry is not None:
            part = part + carry
        carry = part[:, 0:1]
        outs.append(part)
    return jnp.concatenate(outs[::-1], axis=1) if len(outs) > 1 else outs[0]


def _prefix_sums(x, tri, carry):
    c = tri.shape[0]
    outs = []
    for i in range(x.shape[1] // c):
        part = jnp.dot(x[:, i * c:(i + 1) * c].astype(MXU_DTYPE), tri, preferred_element_type=F32) + carry
        carry = part[:, c - 1:c]
        outs.append(part)
    return jnp.concatenate(outs, axis=1) if len(outs) > 1 else outs[0]


def _head_masks():
    lane = lax.broadcasted_iota(jnp.int32, (1, HEADS_PER_BLOCK * HEAD_DIM), 1)
    return [(lane >= h * HEAD_DIM) & (lane < (h + 1) * HEAD_DIM) for h in range(HEADS_PER_BLOCK)]


def _sba_fwd(q, k, v, *, tb=256):
    seq = q.shape[0]
    tb = min(tb, seq)
    nq = seq // tb
    cw = HEADS_PER_BLOCK * HEAD_DIM
    scale = HEAD_DIM ** -0.5

    def body(q_ref, k_ref, v_ref, o_ref, t_ref, acc_ref, r_ref):
        qi = pl.program_id(1)
        masks = _head_masks()
        qf = q_ref[...].astype(F32) * (scale * LOG2E)
        q2 = jnp.concatenate([jnp.where(m, qf, 0.0) for m in masks], axis=0).astype(MXU_DTYPE)
        causal = lax.broadcasted_iota(jnp.int32, (tb, tb), 1) < lax.broadcasted_iota(jnp.int32, (tb, tb), 0)
        causal2 = jnp.concatenate([causal] * HEADS_PER_BLOCK, axis=0)
        neg_upper = _tri(tb, lambda j, s: j >= s, -1.0)
        acc_ref[...] = jnp.zeros_like(acc_ref)
        r_ref[...] = jnp.zeros_like(r_ref)

        def block(start, width, diag):
            ks = pl.ds(pl.multiple_of(start, tb), width)
            kb = k_ref[ks, :].astype(MXU_DTYPE)
            vb = v_ref[ks, :]
            z = lax.dot_general(q2, kb, (((1,), (1,)), ((), ())), preferred_element_type=F32)
            sp, _ = _softplus2(z)
            if diag:
                sp = jnp.where(causal2, sp, 0.0)
            incl = _suffix_sums(sp, neg_upper)
            a = z + incl + r_ref[...]
            if diag:
                a = jnp.where(causal2, a, -1e30)
            wgt = jnp.exp2(a).astype(MXU_DTYPE)
            wcat = jnp.concatenate([wgt[h * tb:(h + 1) * tb] for h in range(HEADS_PER_BLOCK)], axis=1)
            vcat = jnp.concatenate([jnp.where(m, vb, jnp.zeros_like(vb)) for m in masks], axis=0).astype(MXU_DTYPE)
            acc_ref[...] += jnp.dot(wcat, vcat, preferred_element_type=F32)
            r_ref[...] += incl[:, 0:1]

        block(qi * tb, tb, True)
        n_wide = qi // KEY_BLOCKS_PER_STEP

        def single(i, c):
            block((qi - 1 - i) * tb, tb, False)
            return c

        lax.fori_loop(0, qi - n_wide * KEY_BLOCKS_PER_STEP, single, 0)

        def step(i, c):
            block((n_wide - 1 - i) * (KEY_BLOCKS_PER_STEP * tb), KEY_BLOCKS_PER_STEP * tb, False)
            return c

        lax.fori_loop(0, n_wide, step, 0)
        o_ref[...] = acc_ref[...].astype(o_ref.dtype)
        for h in range(HEADS_PER_BLOCK):
            t_ref[:, h:h + 1] = r_ref[h * tb:(h + 1) * tb]

    n_hp = D_MODEL // cw
    return pl.pallas_call(
        body, name="sba_fwd", grid=(n_hp, nq),
        in_specs=[pl.BlockSpec((tb, cw), lambda h, i: (i, h)),
                  pl.BlockSpec((seq, cw), lambda h, i: (0, h)),
                  pl.BlockSpec((seq, cw), lambda h, i: (0, h))],
        out_specs=[pl.BlockSpec((tb, cw), lambda h, i: (i, h)),
                   pl.BlockSpec((None, tb, HEADS_PER_BLOCK), lambda h, i: (h, i, 0))],
        out_shape=[jax.ShapeDtypeStruct((seq, D_MODEL), MXU_DTYPE),
                   jax.ShapeDtypeStruct((n_hp, seq, HEADS_PER_BLOCK), F32)],
        scratch_shapes=[pltpu.VMEM((tb, cw), F32), pltpu.VMEM((HEADS_PER_BLOCK * tb, 1), F32)],
        compiler_params=_cparams(("parallel", "arbitrary")),
    )(q, k, v)


def _sba_bwd(q, k, v, do, tot, *, tb=256):
    seq = q.shape[0]
    tb = min(tb, seq)
    nq = seq // tb
    cw = HEADS_PER_BLOCK * HEAD_DIM
    scale = HEAD_DIM ** -0.5

    def body(q_ref, k_ref, v_ref, do_ref, t_ref, dq_ref, dk_ref, dv_ref, acc_ref, lp_ref, dp_ref):
        qi = pl.program_id(1)

        @pl.when(qi == 0)
        def _():
            dk_ref[...] = jnp.zeros_like(dk_ref)
            dv_ref[...] = jnp.zeros_like(dv_ref)

        masks = _head_masks()
        qf = q_ref[...].astype(F32) * (scale * LOG2E)
        q2 = jnp.concatenate([jnp.where(m, qf, 0.0) for m in masks], axis=0).astype(MXU_DTYPE)
        dof = do_ref[...].astype(F32)
        do2 = jnp.concatenate([jnp.where(m, dof, 0.0) for m in masks], axis=0).astype(MXU_DTYPE)
        tot2 = jnp.concatenate([t_ref[:, h:h + 1] for h in range(HEADS_PER_BLOCK)], axis=0)
        causal = lax.broadcasted_iota(jnp.int32, (tb, tb), 1) < lax.broadcasted_iota(jnp.int32, (tb, tb), 0)
        causal2 = jnp.concatenate([causal] * HEADS_PER_BLOCK, axis=0)
        neg_upper = _tri(tb, lambda j, s: j >= s, -1.0)
        lower = _tri(tb, lambda j, s: j <= s)
        acc_ref[...] = jnp.zeros_like(acc_ref)
        lp_ref[...] = jnp.zeros_like(lp_ref)
        dp_ref[...] = jnp.zeros_like(dp_ref)

        def block(start, width, diag):
            ks = pl.ds(pl.multiple_of(start, tb), width)
            kb = k_ref[ks, :]
            vb = v_ref[ks, :].astype(MXU_DTYPE)
            z = lax.dot_general(q2, kb.astype(MXU_DTYPE), (((1,), (1,)), ((), ())), preferred_element_type=F32)
            sp, u = _softplus2(z)
            if diag:
                sp = jnp.where(causal2, sp, 0.0)
            incl = _suffix_sums(sp, neg_upper)
            lp_next = lp_ref[...] + incl[:, 0:1]
            lp_ref[...] = lp_next
            a = z + incl + (tot2 - lp_next)
            if diag:
                a = jnp.where(causal2, a, -1e30)
            wgt = jnp.exp2(a)
            dw = lax.dot_general(do2, vb, (((1,), (1,)), ((), ())), preferred_element_type=F32)
            da = dw * wgt
            pre = _prefix_sums(da, lower, dp_ref[...])
            dp_ref[...] = pre[:, width - 1:width]
            sig = 1.0 - pl.reciprocal(u, approx=True)
            dz = da - pre * sig
            if diag:
                dz = jnp.where(causal2, dz, 0.0)
            dzb = dz.astype(MXU_DTYPE)
            dzcat = jnp.concatenate([dzb[h * tb:(h + 1) * tb] for h in range(HEADS_PER_BLOCK)], axis=1)
            kcat = jnp.concatenate([jnp.where(m, kb, jnp.zeros_like(kb)) for m in masks], axis=0).astype(MXU_DTYPE)
            acc_ref[...] += jnp.dot(dzcat, kcat, preferred_element_type=F32)
            dk_ref[ks, :] += lax.dot_general(dzb, q2, (((0,), (0,)), ((), ())), preferred_element_type=F32) * LN2
            dv_ref[ks, :] += lax.dot_general(wgt.astype(MXU_DTYPE), do2, (((0,), (0,)), ((), ())),
                                             preferred_element_type=F32)

        n_wide = qi // KEY_BLOCKS_PER_STEP

        def step(i, c):
            block(i * (KEY_BLOCKS_PER_STEP * tb), KEY_BLOCKS_PER_STEP * tb, False)
            return c

        lax.fori_loop(0, n_wide, step, 0)

        def single(i, c):
            block((n_wide * KEY_BLOCKS_PER_STEP + i) * tb, tb, False)
            return c

        lax.fori_loop(0, qi - n_wide * KEY_BLOCKS_PER_STEP, single, 0)
        block(qi * tb, tb, True)
        dq_ref[...] = acc_ref[...] * scale

    n_hp = D_MODEL // cw
    qspec = pl.BlockSpec((tb, cw), lambda h, i: (i, h))
    full = pl.BlockSpec((seq, cw), lambda h, i: (0, h))
    return pl.pallas_call(
        body, name="sba_bwd", grid=(n_hp, nq),
        in_specs=[qspec, full, full, qspec, pl.BlockSpec((None, tb, HEADS_PER_BLOCK), lambda h, i: (h, i, 0))],
        out_specs=[qspec, full, full],
        out_shape=[jax.ShapeDtypeStruct((seq, D_MODEL), F32)] * 3,
        scratch_shapes=[pltpu.VMEM((tb, cw), F32), pltpu.VMEM((HEADS_PER_BLOCK * tb, 1), F32),
                        pltpu.VMEM((HEADS_PER_BLOCK * tb, 1), F32)],
        compiler_params=_cparams(("parallel", "arbitrary")),
    )(q, k, v, do, tot)


def _my_place():
    return lax.axis_index("x"), lax.axis_index("y"), lax.axis_index("c")


def _other_chips(x, y):
    return [(1 - x, y), (x, 1 - y), (1 - x, 1 - y)]


def _all_gather_chips(slabs, *, name):
    n = len(slabs)

    def body(*refs):
        ins, outs = refs[:n], refs[n:2 * n]
        send_sems, recv_sems, loc_sems = refs[2 * n:]
        x, y, c = _my_place()
        me = 2 * x + y
        local = [pltpu.make_async_copy(ins[i], outs[i].at[me], loc_sems.at[i]) for i in range(n)]
        for cp in local:
            cp.start()
        remote = []
        for j, (px, py) in enumerate(_other_chips(x, y)):
            for i in range(n):
                remote.append(pltpu.make_async_remote_copy(
                    src_ref=ins[i], dst_ref=outs[i].at[me], send_sem=send_sems.at[j * n + i],
                    recv_sem=recv_sems.at[j * n + i], device_id=(px, py, c), device_id_type=MESH))
        for cp in remote:
            cp.start()
        for cp in remote:
            cp.wait_recv()
        for cp in remote:
            cp.wait_send()
        for cp in local:
            cp.wait()

    return pl.pallas_call(
        body, name=name, in_specs=[ANY] * n, out_specs=[ANY] * n,
        out_shape=[jax.ShapeDtypeStruct((4,) + s.shape, s.dtype) for s in slabs],
        scratch_shapes=[pltpu.SemaphoreType.DMA((3 * n,)), pltpu.SemaphoreType.DMA((3 * n,)),
                        pltpu.SemaphoreType.DMA((n,))],
    )(*slabs)


def _all_gather_chips_two_level(slabs, *, name):
    n = len(slabs)
    halves = [s.shape[0] // 2 for s in slabs]

    def body(*refs):
        ins, outs = refs[:n], refs[n:2 * n]
        ici_send, ici_recv, d2d_send, d2d_recv, loc_sems = refs[2 * n:]
        x, y, c = _my_place()
        me = 2 * x + y
        chips = _other_chips(x, y)
        mine = [pl.ds(c * h, h) for h in halves]
        theirs = [pl.ds((1 - c) * h, h) for h in halves]
        local = [pltpu.make_async_copy(ins[i], outs[i].at[me], loc_sems.at[i]) for i in range(n)]
        for cp in local:
            cp.start()

        def over_ici(j, i, chip):
            px, py = chips[j]
            slot = 2 * chip[0] + chip[1]
            return pltpu.make_async_remote_copy(
                src_ref=ins[i].at[mine[i]], dst_ref=outs[i].at[slot, mine[i]], send_sem=ici_send.at[j * n + i],
                recv_sem=ici_recv.at[j * n + i], device_id=(px, py, c), device_id_type=MESH)

        def over_d2d(j, i, rows):
            px, py = chips[j]
            block = outs[i].at[2 * px + py, rows]
            return pltpu.make_async_remote_copy(
                src_ref=block, dst_ref=block, send_sem=d2d_send.at[j * n + i], recv_sem=d2d_recv.at[j * n + i],
                device_id=(x, y, 1 - c), device_id_type=MESH)

        sent = [over_ici(j, i, (x, y)) for j in range(3) for i in range(n)]
        for cp in sent:
            cp.start()
        passed = []
        for j in range(3):
            for i in range(n):
                over_ici(j, i, chips[j]).wait_recv()
                cp = over_d2d(j, i, mine[i])
                cp.start()
                passed.append(cp)
        for j in range(3):
            for i in range(n):
                over_d2d(j, i, theirs[i]).wait_recv()
        for cp in sent + passed:
            cp.wait_send()
        for cp in local:
            cp.wait()

    return pl.pallas_call(
        body, name=name, in_specs=[ANY] * n, out_specs=[ANY] * n,
        out_shape=[jax.ShapeDtypeStruct((4,) + s.shape, s.dtype) for s in slabs],
        scratch_shapes=[pltpu.SemaphoreType.DMA((3 * n,)), pltpu.SemaphoreType.DMA((3 * n,)),
                        pltpu.SemaphoreType.DMA((3 * n,)), pltpu.SemaphoreType.DMA((3 * n,)),
                        pltpu.SemaphoreType.DMA((n,))],
    )(*slabs)


def _split_between_cores(gs, *, name):
    def body(g_ref, mine_ref, got_ref, send_sem, recv_sem, loc_sem):
        x, y, c = _my_place()
        keep = pltpu.make_async_copy(g_ref.at[:, c], mine_ref, loc_sem)
        keep.start()
        cp = pltpu.make_async_remote_copy(src_ref=g_ref.at[:, 1 - c], dst_ref=got_ref, send_sem=send_sem,
                                          recv_sem=recv_sem, device_id=(x, y, 1 - c), device_id_type=MESH)
        cp.start()
        cp.wait_recv()
        cp.wait_send()
        keep.wait()

    shape = jax.ShapeDtypeStruct((gs.shape[0],) + gs.shape[2:], gs.dtype)
    return pl.pallas_call(
        body, name=name, in_specs=[ANY], out_specs=[ANY, ANY], out_shape=[shape, shape],
        scratch_shapes=[pltpu.SemaphoreType.DMA, pltpu.SemaphoreType.DMA, pltpu.SemaphoreType.DMA],
    )(gs)


def _join_between_cores(half, *, name):
    def body(h_ref, o_ref, send_sem, recv_sem, loc_sem):
        x, y, c = _my_place()
        keep = pltpu.make_async_copy(h_ref, o_ref.at[c], loc_sem)
        keep.start()
        cp = pltpu.make_async_remote_copy(src_ref=h_ref, dst_ref=o_ref.at[c], send_sem=send_sem, recv_sem=recv_sem,
                                          device_id=(x, y, 1 - c), device_id_type=MESH)
        cp.start()
        pltpu.make_async_remote_copy(src_ref=h_ref, dst_ref=o_ref.at[1 - c], send_sem=send_sem, recv_sem=recv_sem,
                                     device_id=(x, y, 1 - c), device_id_type=MESH).wait_recv()
        cp.wait_send()
        keep.wait()

    return pl.pallas_call(
        body, name=name, in_specs=[ANY], out_specs=ANY,
        out_shape=jax.ShapeDtypeStruct((2,) + half.shape, half.dtype),
        scratch_shapes=[pltpu.SemaphoreType.DMA, pltpu.SemaphoreType.DMA, pltpu.SemaphoreType.DMA],
    )(half)


def _all_to_all_chips(parts, *, name):
    def body(p_ref, o_ref, send_sems, recv_sems, loc_sem):
        x, y, c = _my_place()
        me = 2 * x + y
        local = pltpu.make_async_copy(p_ref.at[me], o_ref.at[me], loc_sem)
        local.start()
        remote = []
        for j, (px, py) in enumerate(_other_chips(x, y)):
            remote.append(pltpu.make_async_remote_copy(
                src_ref=p_ref.at[2 * px + py], dst_ref=o_ref.at[me], send_sem=send_sems.at[j],
                recv_sem=recv_sems.at[j], device_id=(px, py, c), device_id_type=MESH))
        for cp in remote:
            cp.start()
        for cp in remote:
            cp.wait_recv()
        for cp in remote:
            cp.wait_send()
        local.wait()

    return pl.pallas_call(
        body, name=name, in_specs=[ANY], out_specs=ANY,
        out_shape=jax.ShapeDtypeStruct(parts.shape, parts.dtype),
        scratch_shapes=[pltpu.SemaphoreType.DMA((3,)), pltpu.SemaphoreType.DMA((3,)), pltpu.SemaphoreType.DMA],
    )(parts)


_SHARDED = [("a_norm_pre", 1), ("a_norm_post", 1), ("a_d_skip", 1), ("a_b_glu", 1), ("a_w_in", 2), ("a_w_glu", 1),
            ("a_w_out", 1), ("w_kv", 1), ("b_w_in", 2), ("b_w_out", 1), ("ple_w_proj", 2), ("ple_w_gate", 1)]
_REPLICATED = ["a_lam_re", "a_lam_im", "a_log_dt", "a_b_re", "a_b_im", "a_c_re", "a_c_im", "kv_norm", "b_norm_pre",
               "b_norm_post"]
_WEIGHT_ORDER = ["a_norm_pre", "a_norm_post", "a_w_in", "a_lam_re", "a_lam_im", "a_log_dt", "a_b_re", "a_b_im",
                 "a_c_re", "a_c_im", "a_d_skip", "a_w_glu", "a_b_glu", "a_w_out", "kv_norm", "w_kv", "b_norm_pre",
                 "b_norm_post", "b_w_in", "b_w_out", "ple_w_proj", "ple_w_gate"]


def _round_up(n, m):
    return (n + m - 1) // m * m


def _pack(pieces, rows):
    flat = jnp.concatenate([p.reshape(-1).astype(F32) for p in pieces])
    flat = jnp.pad(flat, (0, rows * SLAB_W - flat.shape[0]))
    return flat.reshape(rows, SLAB_W)


def _unpack(slab, shapes):
    flat = slab.reshape(-1)
    out, off = [], 0
    for s in shapes:
        n = math.prod(s)
        out.append(flat[off:off + n].reshape(s))
        off += n
    return out


def _pick_tile(n, cap):
    return max(t for t in range(SUBLANES, cap + 1, SUBLANES) if n % t == 0)


def _size_rows(shapes, mult):
    return _round_up(_round_up(sum(math.prod(s) for s in shapes), SLAB_W) // SLAB_W, mult)


def _adamw_tile(w, g, m, v):
    m2 = ADAM_B1 * m + (1.0 - ADAM_B1) * g
    v2 = ADAM_B2 * v + (1.0 - ADAM_B2) * (g * g)
    m_hat = m2 / (1.0 - ADAM_B1 ** ADAM_STEP)
    v_hat = v2 / (1.0 - ADAM_B2 ** ADAM_STEP)
    delta = -ADAM_LR * (m_hat / (jnp.sqrt(v_hat) + ADAM_EPS) + ADAM_WD * w)
    return delta, m2, v2


def _forward_backward(x0, p0, p1, tgt, W):
    seq = x0.shape[0]
    sw = D_MODEL
    g = {}

    h1, = _ew(lambda xt, gn: (_rms(xt, gn),), [x0], [W["a_norm_pre"]], [MXU_DTYPE], 0, name="a_pre_norm")
    w_u, w_ga = W["a_w_in"][:, :sw], W["a_w_in"][:, sw:]
    u = _mm(h1, w_u, name="a_in_u")
    gate = _mm(h1, w_ga, out_dtype=MXU_DTYPE, name="a_in_gate")
    a_re, a_im, bbt_re, bbt_im, pw_re, pw_im = _s5_params_fwd(W["lam_re"], W["lam_im"], W["log_dt"], W["bt_re"],
                                                               W["bt_im"])
    nb, gpb = N_GROUP_BLOCKS, GROUPS_PER_BLOCK
    bbd = jnp.concatenate([_block_diag(bbt_re.reshape(nb, gpb, GROUP_SIZE, STATE)),
                           _block_diag(bbt_im.reshape(nb, gpb, GROUP_SIZE, STATE))], axis=2).astype(MXU_DTYPE)
    crt = W["c_re"].transpose(0, 2, 1).reshape(nb, gpb, STATE, GROUP_SIZE)
    cit = W["c_im"].transpose(0, 2, 1).reshape(nb, gpb, STATE, GROUP_SIZE)
    cbd = jnp.concatenate([_block_diag(crt), -_block_diag(cit)], axis=1).astype(MXU_DTYPE)
    cbd_t = cbd.transpose(0, 2, 1)
    tab_f, tab_b = _scan_tables(pw_re.reshape(SUBLANES, nb, SSM_LANES), pw_im.reshape(SUBLANES, nb, SSM_LANES))
    ys, hs = _s5_fwd(u, bbd, cbd, tab_f)
    gl, = _ew(lambda a, b, d: (_gelu_skip(a, b, d),), [ys, u], [W["a_d_skip"]], [MXU_DTYPE], 0, name="a_gelu")
    t = _mm(gl, W["a_w_glu"], out_dtype=MXU_DTYPE, name="a_glu")
    y3, = _ew(lambda a, b, c, d: (_glu_gate(a, b, c, d),), [gl, t, gate], [W["a_b_glu"]], [MXU_DTYPE], 0, name="a_gate")
    y4 = _mm(y3, W["a_w_out"], out_dtype=MXU_DTYPE, name="a_out")
    x1, = _ew(lambda a, b, gn: (a + _rms(b, gn),), [x0, y4], [W["a_norm_post"]], [F32], 0, name="a_post_norm")
    gt0 = _mm(x1, W["ple_w_gate"][0], out_dtype=MXU_DTYPE, name="ple0_gate")
    pp0 = _mm(p0, W["ple_w_proj"][0], out_dtype=MXU_DTYPE, name="ple0_proj")
    x2, hk, h2 = _ew(lambda a, b, c, g1, g2: (_ple(a, b, c), _rms(_ple(a, b, c), g1), _rms(_ple(a, b, c), g2)),
                     [x1, gt0, pp0], [W["kv_norm"], W["b_norm_pre"]], [F32, MXU_DTYPE, MXU_DTYPE], 0, name="ple0_mix")

    w_k, w_v = W["w_kv"][:, :sw], W["w_kv"][:, sw:]
    k = _mm(hk, w_k, out_dtype=MXU_DTYPE, name="kv_k")
    v = _mm(hk, w_v, out_dtype=MXU_DTYPE, name="kv_v")
    w_q, w_gb = W["b_w_in"][:, :sw], W["b_w_in"][:, sw:]
    q = _mm(h2, w_q, out_dtype=MXU_DTYPE, name="b_in_q")
    gate2 = _mm(h2, w_gb, out_dtype=MXU_DTYPE, name="b_in_gate")
    o, tot = _sba_fwd(q, k, v)
    y5in, = _ew(lambda a, b: (_ogate(a, b),), [o, gate2], [], [MXU_DTYPE], 0, name="b_gate")
    y5 = _mm(y5in, W["b_w_out"], out_dtype=MXU_DTYPE, name="b_out")
    x3, = _ew(lambda a, b, gn: (a + _rms(b, gn),), [x2, y5], [W["b_norm_post"]], [F32], 0, name="b_post_norm")
    gt1 = _mm(x3, W["ple_w_gate"][1], out_dtype=MXU_DTYPE, name="ple1_gate")
    pp1 = _mm(p1, W["ple_w_proj"][1], out_dtype=MXU_DTYPE, name="ple1_proj")

    def loss_fn(xt, gt, pp, tg):
        s = jax.nn.sigmoid(gt)
        d = (xt + s * pp - tg) * (1.0 / D_MODEL)
        lsum = jnp.sum(d * d, axis=(0, 1), keepdims=True) * (0.5 * D_MODEL)
        return d, d * pp * s * (1.0 - s), d * s, lsum

    dx3a, dgt1, dpp1, loss = _ew(loss_fn, [x3, gt1, pp1, tgt], [], [F32, MXU_DTYPE, MXU_DTYPE], 1, name="loss_head")
    g_gate1 = _mm(x3, dgt1, ta=True, name="ple1_gate_dw")
    g_proj1 = _mm(p1, dpp1, ta=True, name="ple1_proj_dw")
    dx3 = _mm(dgt1, W["ple_w_gate"][1], tb=True, add=dx3a, name="ple1_gate_dx")

    def post_norm_bwd(ct, yt, gn):
        _, vjp = jax.vjp(_rms, yt, gn)
        dy, dg = vjp(ct)
        return dy, dg

    dy5, g["b_norm_post"] = _ew(post_norm_bwd, [dx3, y5], [W["b_norm_post"]], [MXU_DTYPE], 1, name="b_post_norm_bwd")
    g["b_w_out"] = _mm(y5in, dy5, ta=True, name="b_out_dw")
    dy5in = _mm(dy5, W["b_w_out"], tb=True, out_dtype=MXU_DTYPE, name="b_out_dx")

    def ogate_bwd(ct, ot, gt):
        _, vjp = jax.vjp(_ogate, ot, gt)
        return vjp(ct)

    do, dgate2 = _ew(ogate_bwd, [dy5in, o, gate2], [], [MXU_DTYPE, MXU_DTYPE], 0, name="b_gate_bwd")
    dq, dk, dv = _sba_bwd(q, k, v, do, tot)
    g["b_w_in"] = jnp.concatenate([_mm(h2, dq, ta=True, name="b_in_q_dw"),
                                   _mm(h2, dgate2, ta=True, name="b_in_gate_dw")], axis=1)
    dh2 = _mm(dgate2, w_gb, tb=True, add=_mm(dq, w_q, tb=True, name="b_in_q_dx"), out_dtype=MXU_DTYPE,
              name="b_in_gate_dx")
    g["w_kv"] = jnp.concatenate([_mm(hk, dk, ta=True, name="kv_k_dw"), _mm(hk, dv, ta=True, name="kv_v_dw")], axis=1)
    dhk = _mm(dv, w_v, tb=True, add=_mm(dk, w_k, tb=True, name="kv_k_dx"), out_dtype=MXU_DTYPE, name="kv_v_dx")

    def mix_bwd(ct, c2, ck, xt, gt, pp, g1, g2):
        x2v, vjp_ple = jax.vjp(_ple, xt, gt, pp)
        _, vjp_k = jax.vjp(_rms, x2v, g1)
        _, vjp_b = jax.vjp(_rms, x2v, g2)
        dxk, dg1 = vjp_k(ck)
        dxb, dg2 = vjp_b(c2)
        dx2 = ct + dxk + dxb
        dx1, dgt, dpp = vjp_ple(dx2)
        return dx1, dgt, dpp, dg1, dg2

    dx1a, dgt0, dpp0, g["kv_norm"], g["b_norm_pre"] = _ew(
        mix_bwd, [dx3, dh2, dhk, x1, gt0, pp0], [W["kv_norm"], W["b_norm_pre"]], [F32, MXU_DTYPE, MXU_DTYPE], 2,
        name="ple0_mix_bwd")
    g_gate0 = _mm(x1, dgt0, ta=True, name="ple0_gate_dw")
    g_proj0 = _mm(p0, dpp0, ta=True, name="ple0_proj_dw")
    dx1 = _mm(dgt0, W["ple_w_gate"][0], tb=True, add=dx1a, name="ple0_gate_dx")
    g["ple_w_gate"] = jnp.stack([g_gate0, g_gate1])
    g["ple_w_proj"] = jnp.stack([g_proj0, g_proj1])

    dy4, g["a_norm_post"] = _ew(post_norm_bwd, [dx1, y4], [W["a_norm_post"]], [MXU_DTYPE], 1, name="a_post_norm_bwd")
    g["a_w_out"] = _mm(y3, dy4, ta=True, name="a_out_dw")
    dy3 = _mm(dy4, W["a_w_out"], tb=True, out_dtype=MXU_DTYPE, name="a_out_dx")

    def gate_bwd(ct, gt_, tt_, gat, bg):
        _, vjp = jax.vjp(_glu_gate, gt_, tt_, gat, bg)
        dg_, dt_, dgate_, dbg = vjp(ct)
        return dg_, dt_, dgate_, dbg

    dgl_a, dt, dgate, g["a_b_glu"] = _ew(gate_bwd, [dy3, gl, t, gate], [W["a_b_glu"]], [F32, MXU_DTYPE, MXU_DTYPE], 1,
                                         name="a_gate_bwd")
    g["a_w_glu"] = _mm(gl, dt, ta=True, name="a_glu_dw")
    dgl = _mm(dt, W["a_w_glu"], tb=True, add=dgl_a, name="a_glu_dx")

    def gelu_bwd(ct, yt, ut, ds):
        _, vjp = jax.vjp(_gelu_skip, yt, ut, ds)
        return vjp(ct)

    dys, du_a, g["a_d_skip"] = _ew(gelu_bwd, [dgl, ys, u], [W["a_d_skip"]], [MXU_DTYPE, F32], 1, name="a_gelu_bwd")
    du, d_bbd, d_cbd, d_ar8, d_ai8 = _s5_bwd(u, dys, du_a, hs, bbd, cbd_t, tab_f, tab_b)
    g["a_w_in"] = jnp.concatenate([_mm(h1, du, ta=True, name="a_in_u_dw"),
                                   _mm(h1, dgate, ta=True, name="a_in_gate_dw")], axis=1)
    dh1 = _mm(dgate, w_ga, tb=True, add=_mm(du, w_u, tb=True, name="a_in_u_dx"), out_dtype=MXU_DTYPE,
              name="a_in_gate_dx")

    def pre_norm_bwd(ct, ch, xt, gn):
        _, vjp = jax.vjp(_rms, xt, gn)
        dx, dg = vjp(ch)
        return ct + dx, dg

    dx0, g["a_norm_pre"] = _ew(pre_norm_bwd, [dx1, dh1, x0], [W["a_norm_pre"]], [F32], 1, name="a_pre_norm_bwd")

    w = SSM_LANES
    d_bbt_re = _block_diag_extract(d_bbd[:, :, :w], GROUP_SIZE, STATE).reshape(N_GROUPS, GROUP_SIZE, STATE)
    d_bbt_im = _block_diag_extract(d_bbd[:, :, w:], GROUP_SIZE, STATE).reshape(N_GROUPS, GROUP_SIZE, STATE)
    d_crt = _block_diag_extract(d_cbd[:, :w, :], STATE, GROUP_SIZE).reshape(N_GROUPS, STATE, GROUP_SIZE)
    d_cit = -_block_diag_extract(d_cbd[:, w:, :], STATE, GROUP_SIZE).reshape(N_GROUPS, STATE, GROUP_SIZE)
    g["a_c_re"] = d_crt.transpose(0, 2, 1)[None]
    g["a_c_im"] = d_cit.transpose(0, 2, 1)[None]
    d_lr, d_li, d_ldt, d_btr, d_bti = _s5_params_bwd(
        W["lam_re"], W["lam_im"], W["log_dt"], W["bt_re"], W["bt_im"], _per_sublane(d_ar8), _per_sublane(d_ai8),
        d_bbt_re, d_bbt_im)
    g["a_lam_re"], g["a_lam_im"] = d_lr[None], d_li[None]
    g["a_log_dt"] = d_ldt.reshape(1, N_GROUPS)
    g["a_b_re"] = d_btr.transpose(0, 2, 1)[None]
    g["a_b_im"] = d_bti.transpose(0, 2, 1)[None]
    return loss, dx0, g


def kernel(x, p, a_norm_pre, a_norm_post, a_w_in, a_lam_re, a_lam_im, a_log_dt, a_b_re, a_b_im, a_c_re, a_c_im, a_d_skip, a_w_glu, a_b_glu, a_w_out, kv_norm, w_kv, b_norm_pre, b_norm_post, b_w_in, b_w_out, ple_w_proj, ple_w_gate, loss_target, m_a_norm_pre, m_a_norm_post, m_a_w_in, m_a_lam_re, m_a_lam_im, m_a_log_dt, m_a_b_re, m_a_b_im, m_a_c_re, m_a_c_im, m_a_d_skip, m_a_w_glu, m_a_b_glu, m_a_w_out, m_kv_norm, m_w_kv, m_b_norm_pre, m_b_norm_post, m_b_w_in, m_b_w_out, m_ple_w_proj, m_ple_w_gate, v_a_norm_pre, v_a_norm_post, v_a_w_in, v_a_lam_re, v_a_lam_im, v_a_log_dt, v_a_b_re, v_a_b_im, v_a_c_re, v_a_c_im, v_a_d_skip, v_a_w_glu, v_a_b_glu, v_a_w_out, v_kv_norm, v_w_kv, v_b_norm_pre, v_b_norm_post, v_b_w_in, v_b_w_out, v_ple_w_proj, v_ple_w_gate):
    loc = dict(a_norm_pre=a_norm_pre, a_norm_post=a_norm_post, a_w_in=a_w_in, a_lam_re=a_lam_re, a_lam_im=a_lam_im,
               a_log_dt=a_log_dt, a_b_re=a_b_re, a_b_im=a_b_im, a_c_re=a_c_re, a_c_im=a_c_im, a_d_skip=a_d_skip,
               a_w_glu=a_w_glu, a_b_glu=a_b_glu, a_w_out=a_w_out, kv_norm=kv_norm, w_kv=w_kv, b_norm_pre=b_norm_pre,
               b_norm_post=b_norm_post, b_w_in=b_w_in, b_w_out=b_w_out, ple_w_proj=ple_w_proj, ple_w_gate=ple_w_gate)
    mom = dict(a_norm_pre=m_a_norm_pre, a_norm_post=m_a_norm_post, a_w_in=m_a_w_in, a_lam_re=m_a_lam_re,
               a_lam_im=m_a_lam_im, a_log_dt=m_a_log_dt, a_b_re=m_a_b_re, a_b_im=m_a_b_im, a_c_re=m_a_c_re,
               a_c_im=m_a_c_im, a_d_skip=m_a_d_skip, a_w_glu=m_a_w_glu, a_b_glu=m_a_b_glu, a_w_out=m_a_w_out,
               kv_norm=m_kv_norm, w_kv=m_w_kv, b_norm_pre=m_b_norm_pre, b_norm_post=m_b_norm_post, b_w_in=m_b_w_in,
               b_w_out=m_b_w_out, ple_w_proj=m_ple_w_proj, ple_w_gate=m_ple_w_gate)
    var = dict(a_norm_pre=v_a_norm_pre, a_norm_post=v_a_norm_post, a_w_in=v_a_w_in, a_lam_re=v_a_lam_re,
               a_lam_im=v_a_lam_im, a_log_dt=v_a_log_dt, a_b_re=v_a_b_re, a_b_im=v_a_b_im, a_c_re=v_a_c_re,
               a_c_im=v_a_c_im, a_d_skip=v_a_d_skip, a_w_glu=v_a_w_glu, a_b_glu=v_a_b_glu, a_w_out=v_a_w_out,
               kv_norm=v_kv_norm, w_kv=v_w_kv, b_norm_pre=v_b_norm_pre, b_norm_post=v_b_norm_post, b_w_in=v_b_w_in,
               b_w_out=v_b_w_out, ple_w_proj=v_ple_w_proj, ple_w_gate=v_ple_w_gate)
    vec_names = [n for n, _ in _SHARDED if loc[n].shape[-2] == 1 and loc[n].ndim == 2]
    mat_names = [n for n, _ in _SHARDED if n not in vec_names]
    axis_of = dict(_SHARDED)
    mat_shapes = [loc[n].shape for n in mat_names]
    vec_shapes = [loc[n].shape for n in vec_names]
    mat_rows = _size_rows(mat_shapes, 32)
    vec_rows = _size_rows(vec_shapes, 16)
    mat_slab = _pack([loc[n] for n in mat_names], mat_rows).astype(MXU_DTYPE)
    vec_slab = _pack([loc[n] for n in vec_names], vec_rows)
    mat_all, vec_all = _all_gather_chips_two_level([mat_slab, vec_slab], name="gather_weights")
    W = {}
    for names, slab_all, shapes in ((mat_names, mat_all, mat_shapes), (vec_names, vec_all, vec_shapes)):
        per_chip = [_unpack(slab_all[j], shapes) for j in range(4)]
        for i, n in enumerate(names):
            W[n] = jnp.concatenate([per_chip[j][i] for j in range(4)], axis=axis_of[n])
    for n in ("a_w_in", "a_w_glu", "a_w_out", "b_w_in", "b_w_out"):
        W[n] = W[n][0]
    W["kv_norm"] = kv_norm.reshape(1, D_MODEL)
    W["b_norm_pre"], W["b_norm_post"] = b_norm_pre, b_norm_post
    W["lam_re"], W["lam_im"] = a_lam_re[0], a_lam_im[0]
    W["log_dt"] = a_log_dt.reshape(N_GROUPS, 1)
    W["bt_re"], W["bt_im"] = a_b_re[0].transpose(0, 2, 1), a_b_im[0].transpose(0, 2, 1)
    W["c_re"], W["c_im"] = a_c_re[0], a_c_im[0]

    loss, dx0, g = _forward_backward(x[0], p[0, 0], p[1, 0], loss_target[0], W)
    for n in ("a_w_in", "a_w_glu", "a_w_out", "b_w_in", "b_w_out"):
        g[n] = g[n][None]
    g["kv_norm"] = g["kv_norm"].reshape(D_MODEL)
    loss = lax.psum(loss[0, 0], ("x", "y", "c"))

    rep_shapes = [loc[n].shape for n in _REPLICATED]
    rep_rows = _size_rows(rep_shapes, 32)
    rep_slab = _pack([g[n] for n in _REPLICATED], rep_rows).reshape(4, rep_rows // 4, SLAB_W)
    sh_names = [n for n, _ in _SHARDED]
    sh_shapes = [loc[n].shape for n in sh_names]
    sh_rows = _size_rows(sh_shapes, 8)
    rows = _round_up(sh_rows + rep_rows // 4, 1024)
    half = rows // 2
    slabs = []
    for j in range(4):
        pieces = [jnp.split(g[n], 4, axis=axis_of[n])[j] for n in sh_names]
        sh = _pack(pieces, sh_rows)
        slabs.append(jnp.concatenate([sh, rep_slab[j], jnp.zeros((rows - sh_rows - rep_rows // 4, SLAB_W), F32)]))
    gs = jnp.stack(slabs).reshape(4, 2, half, SLAB_W)
    mine, got = _split_between_cores(gs, name="reduce_cores")
    part, = _ew(lambda a, b: (a + b,), [mine.reshape(4 * half, SLAB_W), got.reshape(4 * half, SLAB_W)], [],
                [MXU_DTYPE], 0, name="reduce_cores_add", tr=512)
    recv = _all_to_all_chips(part.reshape(4, half, SLAB_W), name="reduce_chips")
    tot_half, = _ew(lambda a, b, c, d: (((a + b) + c) + d,), [recv[0], recv[1], recv[2], recv[3]], [], [F32], 0,
                    name="reduce_chips_add", tr=512)
    gsum = _join_between_cores(tot_half, name="share_cores").reshape(rows, SLAB_W)
    rep_q = gsum[sh_rows:sh_rows + rep_rows // 4]
    rep_all, = _all_gather_chips([rep_q], name="gather_replicated")
    gsum_full = jnp.concatenate([gsum[:sh_rows], rep_all.reshape(rep_rows, SLAB_W)])

    all_names = sh_names + _REPLICATED
    all_shapes = sh_shapes + rep_shapes

    def slab_of(d):
        return jnp.concatenate([_pack([d[n] for n in sh_names], sh_rows), _pack([d[n] for n in _REPLICATED], rep_rows)])

    n_rows = sh_rows + rep_rows
    outs = _ew(_adamw_tile, [slab_of(loc), gsum_full, slab_of(mom), slab_of(var)], [], [F32] * 3, 0, name="adamw",
               tr=_pick_tile(n_rows, 512))
    res = []
    for slab in [gsum_full] + list(outs):
        parts = _unpack(slab[:sh_rows], sh_shapes) + _unpack(slab[sh_rows:], rep_shapes)
        res.append(dict(zip(all_names, parts)))
    out = [loss, dx0[None]]
    for d in res:
        out += [d[n] for n in _WEIGHT_ORDER]
    return tuple(out)
```

```python
import functools
import math

import jax
import jax.numpy as jnp
from jax import lax
from jax.experimental import pallas as pl
from jax.experimental.pallas import tpu as pltpu

F32 = jnp.float32
MXU_DTYPE = jnp.bfloat16

D_MODEL = 1024
N_GROUPS = 64
GROUP_SIZE = 16
STATE = 64
GROUPS_PER_BLOCK = 8
SSM_LANES = GROUPS_PER_BLOCK * STATE
N_GROUP_BLOCKS = N_GROUPS // GROUPS_PER_BLOCK
HEAD_DIM = 64
HEADS_PER_BLOCK = 2
KEY_BLOCKS_PER_STEP = 4
SCAN_TILES_PER_STEP = 2
EPS = 1e-6
SLAB_W = 512
SUBLANES = 8
VMEM_LIMIT = 56 * 1024 * 1024

ADAM_LR = 0.001
ADAM_B1 = 0.9
ADAM_B2 = 0.999
ADAM_EPS = 1e-08
ADAM_WD = 0.01
ADAM_STEP = 10

MESH = pl.DeviceIdType.MESH
ANY = pl.BlockSpec(memory_space=pl.ANY)


def _cparams(sem):
    return pltpu.CompilerParams(dimension_semantics=sem, vmem_limit_bytes=VMEM_LIMIT)


def _mm(a, b, *, ta=False, tb=False, add=None, out_dtype=F32, name, tm=1024, tn=1024, tk=1024):
    if ta:
        kdim, m = a.shape
    else:
        m, kdim = a.shape
    if tb:
        n, kb = b.shape
    else:
        kb, n = b.shape
    assert kdim == kb, (a.shape, b.shape, ta, tb)
    tm, tn, tk = min(tm, m), min(tn, n), min(tk, kdim)
    assert m % tm == 0 and n % tn == 0 and kdim % tk == 0
    nk = kdim // tk
    dims = (((0 if ta else 1,), (1 if tb else 0,)), ((), ()))
    has_add = add is not None

    def body(*refs):
        if has_add:
            a_ref, b_ref, add_ref, o_ref, acc_ref = refs
        else:
            a_ref, b_ref, o_ref, acc_ref = refs
        k = pl.program_id(2)

        @pl.when(k == 0)
        def _():
            acc_ref[...] = jnp.zeros_like(acc_ref)

        acc_ref[...] += lax.dot_general(a_ref[...].astype(MXU_DTYPE), b_ref[...].astype(MXU_DTYPE), dims,
                                        preferred_element_type=F32)

        @pl.when(k == nk - 1)
        def _():
            r = acc_ref[...]
            if has_add:
                r = r + add_ref[...].astype(F32)
            o_ref[...] = r.astype(o_ref.dtype)

    a_spec = pl.BlockSpec((tk, tm), lambda i, j, k: (k, i)) if ta else pl.BlockSpec((tm, tk), lambda i, j, k: (i, k))
    b_spec = pl.BlockSpec((tn, tk), lambda i, j, k: (j, k)) if tb else pl.BlockSpec((tk, tn), lambda i, j, k: (k, j))
    o_spec = pl.BlockSpec((tm, tn), lambda i, j, k: (i, j))
    in_specs = [a_spec, b_spec] + ([o_spec] if has_add else [])
    args = (a, b) + ((add,) if has_add else ())
    return pl.pallas_call(
        body, name=name, grid=(m // tm, n // tn, nk), in_specs=in_specs, out_specs=o_spec,
        out_shape=jax.ShapeDtypeStruct((m, n), out_dtype), scratch_shapes=[pltpu.VMEM((tm, tn), F32)],
        compiler_params=_cparams(("parallel", "parallel", "arbitrary")),
    )(*args)


def _ew(fn, rows, params, row_dtypes, n_acc, *, name, tr=256):
    nrows = rows[0].shape[0]
    tr = min(tr, nrows)
    assert nrows % tr == 0
    n_in, n_par, n_row = len(rows), len(params), len(row_dtypes)
    tile_avals = [jax.ShapeDtypeStruct((tr, r.shape[1]), F32) for r in rows]
    par_avals = [jax.ShapeDtypeStruct(q.shape, F32) for q in params]
    out_avals = jax.eval_shape(fn, *tile_avals, *par_avals)
    assert len(out_avals) == n_row + n_acc

    def body(*refs):
        ins = [r[...].astype(F32) for r in refs[:n_in + n_par]]
        outs = fn(*ins)
        row_refs = refs[n_in + n_par:n_in + n_par + n_row]
        acc_refs = refs[n_in + n_par + n_row:]
        for r, o in zip(row_refs, outs[:n_row]):
            r[...] = o.astype(r.dtype)
        if n_acc:
            @pl.when(pl.program_id(0) == 0)
            def _():
                for r in acc_refs:
                    r[...] = jnp.zeros_like(r)
            for r, o in zip(acc_refs, outs[n_row:]):
                r[...] += o

    in_specs = [pl.BlockSpec((tr, r.shape[1]), lambda i: (i, 0)) for r in rows]
    in_specs += [pl.BlockSpec(q.shape, lambda i: (0, 0)) for q in params]
    out_specs = [pl.BlockSpec((tr, o.shape[1]), lambda i: (i, 0)) for o in out_avals[:n_row]]
    out_specs += [pl.BlockSpec(o.shape, lambda i: (0, 0)) for o in out_avals[n_row:]]
    out_shape = [jax.ShapeDtypeStruct((nrows, o.shape[1]), dt) for o, dt in zip(out_avals[:n_row], row_dtypes)]
    out_shape += [jax.ShapeDtypeStruct(o.shape, F32) for o in out_avals[n_row:]]
    return pl.pallas_call(
        body, name=name, grid=(nrows // tr,), in_specs=in_specs, out_specs=out_specs, out_shape=out_shape,
        compiler_params=_cparams(("arbitrary",)),
    )(*rows, *params)


def _rms(xt, g):
    r = lax.rsqrt(jnp.mean(xt * xt, axis=-1, keepdims=True) + EPS)
    return xt * r * g


def _gelu_skip(ys, u, dskip):
    return jax.nn.gelu(ys + dskip * u)


def _glu_gate(g, t, gate, bglu):
    return g * jax.nn.sigmoid(t + bglu) * jax.nn.silu(gate)


def _ogate(o, gate):
    return o * jax.nn.silu(gate)


def _ple(xt, gt, pp):
    return xt + jax.nn.sigmoid(gt) * pp


def _s5_disc(lam_re, lam_im, log_dt, bt_re, bt_im):
    lr = jnp.minimum(lam_re, -1e-4)
    li = lam_im
    dt = jnp.exp(log_dt)
    mag = jnp.exp(lr * dt)
    a_re = mag * jnp.cos(li * dt)
    a_im = mag * jnp.sin(li * dt)
    den = lr * lr + li * li
    nr = a_re - 1.0
    f_re = (nr * lr + a_im * li) / den
    f_im = (a_im * lr - nr * li) / den
    bb_re = f_re[:, None, :] * bt_re - f_im[:, None, :] * bt_im
    bb_im = f_re[:, None, :] * bt_im + f_im[:, None, :] * bt_re
    return a_re, a_im, bb_re, bb_im


def _s5_params_fwd(lam_re, lam_im, log_dt, bt_re, bt_im):
    def body(lr_ref, li_ref, dt_ref, br_ref, bi_ref, ar_ref, ai_ref, bbr_ref, bbi_ref, pr_ref, pi_ref):
        a_re, a_im, bb_re, bb_im = _s5_disc(lr_ref[...], li_ref[...], dt_ref[...], br_ref[...], bi_ref[...])
        ar_ref[...] = a_re
        ai_ref[...] = a_im
        bbr_ref[...] = bb_re
        bbi_ref[...] = bb_im
        pr, pi = a_re, a_im
        for t in range(SUBLANES):
            pr_ref[t] = pr
            pi_ref[t] = pi
            pr, pi = pr * a_re - pi * a_im, pr * a_im + pi * a_re

    g, s = lam_re.shape
    sd = jax.ShapeDtypeStruct
    return pl.pallas_call(
        body, name="s5_params_fwd",
        out_shape=[sd((g, s), F32), sd((g, s), F32), sd(bt_re.shape, F32), sd(bt_re.shape, F32),
                   sd((SUBLANES, g, s), F32), sd((SUBLANES, g, s), F32)],
    )(lam_re, lam_im, log_dt, bt_re, bt_im)


def _s5_params_bwd(lam_re, lam_im, log_dt, bt_re, bt_im, d_ar, d_ai, d_bbr, d_bbi):
    def body(lr_ref, li_ref, dt_ref, br_ref, bi_ref, c0, c1, c2, c3, o0, o1, o2, o3, o4):
        _, vjp = jax.vjp(_s5_disc, lr_ref[...], li_ref[...], dt_ref[...], br_ref[...], bi_ref[...])
        grads = vjp((jnp.sum(c0[...], axis=0), jnp.sum(c1[...], axis=0), c2[...], c3[...]))
        for o, gval in zip((o0, o1, o2, o3, o4), grads):
            o[...] = gval

    sd = jax.ShapeDtypeStruct
    return pl.pallas_call(
        body, name="s5_params_bwd",
        out_shape=[sd(lam_re.shape, F32), sd(lam_im.shape, F32), sd(log_dt.shape, F32), sd(bt_re.shape, F32),
                   sd(bt_im.shape, F32)],
    )(lam_re, lam_im, log_dt, bt_re, bt_im, d_ar, d_ai, d_bbr, d_bbi)


def _block_diag(t):
    nb, g, r, c = t.shape
    eye = jnp.eye(g, dtype=t.dtype)
    return jnp.einsum("ngrc,gh->ngrhc", t, eye).reshape(nb, g * r, g * c)


def _block_diag_extract(m, r, c):
    nb = m.shape[0]
    g = GROUPS_PER_BLOCK
    eye = jnp.eye(g, dtype=m.dtype)[None, :, None, :, None]
    return jnp.sum(m.reshape(nb, g, r, g, c) * eye, axis=3)


def _per_sublane(acc):
    return acc.reshape(N_GROUP_BLOCKS, SUBLANES, GROUPS_PER_BLOCK, STATE).transpose(1, 0, 2, 3).reshape(
        SUBLANES, N_GROUPS, STATE)


def _scan_tables(pr, pi):
    row = jnp.arange(SUBLANES)[:, None, None]

    def tables(im_sign, fwd):
        kinds = []
        for k in (1, 2, 4):
            mask = (row >= k) if fwd else (row < SUBLANES - k)
            kinds.append(jnp.where(mask, pr[k - 1][None], 0.0))
            kinds.append(jnp.where(mask, im_sign * pi[k - 1][None], 0.0))
        if fwd:
            kinds += [pr, im_sign * pi]
        else:
            kinds += [pr[::-1], im_sign * pi[::-1]]
        return jnp.stack(kinds, axis=0).transpose(2, 0, 1, 3)

    return tables(1.0, True), tables(-1.0, False)


def _scan_rows(buf_ref, tab_ref, carry_r, carry_i, n_tiles, reverse):
    w = SSM_LANES
    assert n_tiles % SCAN_TILES_PER_STEP == 0

    def local_scan(jj):
        rows = pl.ds(pl.multiple_of(jj * SUBLANES, SUBLANES), SUBLANES)
        zr = buf_ref[rows, 0:w]
        zi = buf_ref[rows, w:2 * w]
        for n, k in enumerate((1, 2, 4)):
            ar = tab_ref[2 * n]
            ai = tab_ref[2 * n + 1]
            sh = (SUBLANES - k) if reverse else k
            rr = pltpu.roll(zr, sh, 0)
            ri = pltpu.roll(zi, sh, 0)
            zr, zi = zr + ar * rr - ai * ri, zi + ar * ri + ai * rr
        return rows, zr, zi

    def step(j, carry):
        first = j * SCAN_TILES_PER_STEP
        tiles = [local_scan((n_tiles - 1 - first - d) if reverse else (first + d)) for d in range(SCAN_TILES_PER_STEP)]
        cr, ci = carry
        pr = tab_ref[6]
        pi = tab_ref[7]
        for rows, zr, zi in tiles:
            zr, zi = zr + pr * cr - pi * ci, zi + pr * ci + pi * cr
            buf_ref[rows, 0:w] = zr
            buf_ref[rows, w:2 * w] = zi
            cr, ci = (zr[0:1], zi[0:1]) if reverse else (zr[SUBLANES - 1:SUBLANES], zi[SUBLANES - 1:SUBLANES])
        return cr, ci

    return lax.fori_loop(0, n_tiles // SCAN_TILES_PER_STEP, step, (carry_r, carry_i))


def _s5_fwd(u, bbd, cbd, tab_f, *, tt=512):
    seq = u.shape[0]
    tt = min(tt, seq)
    nt = seq // tt
    w = SSM_LANES

    def body(u_ref, b_ref, c_ref, tab_ref, y_ref, hs_ref, buf_ref, h_ref):
        it = pl.program_id(1)

        @pl.when(it == 0)
        def _():
            h_ref[...] = jnp.zeros_like(h_ref)

        hs_ref[...] = h_ref[...]
        buf_ref[...] = jnp.dot(u_ref[...].astype(MXU_DTYPE), b_ref[...], preferred_element_type=F32)
        hr, hi = _scan_rows(buf_ref, tab_ref, h_ref[:, 0:w], h_ref[:, w:2 * w], tt // SUBLANES, False)
        h_ref[:, 0:w] = hr
        h_ref[:, w:2 * w] = hi
        y_ref[...] = jnp.dot(buf_ref[...].astype(MXU_DTYPE), c_ref[...], preferred_element_type=F32)

    return pl.pallas_call(
        body, name="s5_fwd", grid=(N_GROUP_BLOCKS, nt),
        in_specs=[pl.BlockSpec((tt, 128), lambda g, t: (t, g)),
                  pl.BlockSpec((None, 128, 2 * w), lambda g, t: (g, 0, 0)),
                  pl.BlockSpec((None, 2 * w, 128), lambda g, t: (g, 0, 0)),
                  pl.BlockSpec((None, 8, SUBLANES, w), lambda g, t: (g, 0, 0, 0))],
        out_specs=[pl.BlockSpec((tt, 128), lambda g, t: (t, g)),
                   pl.BlockSpec((None, None, 1, 2 * w), lambda g, t: (g, t, 0, 0))],
        out_shape=[jax.ShapeDtypeStruct((seq, D_MODEL), F32),
                   jax.ShapeDtypeStruct((N_GROUP_BLOCKS, nt, 1, 2 * w), F32)],
        scratch_shapes=[pltpu.VMEM((tt, 2 * w), F32), pltpu.VMEM((1, 2 * w), F32)],
        compiler_params=_cparams(("parallel", "arbitrary")),
    )(u, bbd, cbd, tab_f)


def _s5_bwd(u, dy, du_add, hs, bbd, cbd_t, tab_f, tab_b, *, tt=512):
    seq = u.shape[0]
    tt = min(tt, seq)
    nt = seq // tt
    w = SSM_LANES
    n_tiles = tt // SUBLANES

    def body(u_ref, dy_ref, dua_ref, hs_ref, b_ref, ct_ref, tabf_ref, tabb_ref,
             du_ref, db_ref, dc_ref, dar_ref, dai_ref, s_ref, l_ref, lam_ref):
        it = pl.program_id(1)

        @pl.when(it == 0)
        def _():
            lam_ref[...] = jnp.zeros_like(lam_ref)
            db_ref[...] = jnp.zeros_like(db_ref)
            dc_ref[...] = jnp.zeros_like(dc_ref)
            dar_ref[...] = jnp.zeros_like(dar_ref)
            dai_ref[...] = jnp.zeros_like(dai_ref)

        ub = u_ref[...].astype(MXU_DTYPE)
        dyb = dy_ref[...].astype(MXU_DTYPE)
        s_ref[...] = jnp.dot(ub, b_ref[...], preferred_element_type=F32)
        h_in_r = hs_ref[:, 0:w]
        h_in_i = hs_ref[:, w:2 * w]
        _scan_rows(s_ref, tabf_ref, h_in_r, h_in_i, n_tiles, False)
        dc_ref[...] += lax.dot_general(s_ref[...].astype(MXU_DTYPE), dyb, (((0,), (0,)), ((), ())),
                                       preferred_element_type=F32)
        l_ref[...] = jnp.dot(dyb, ct_ref[...], preferred_element_type=F32)
        lr, li = _scan_rows(l_ref, tabb_ref, lam_ref[:, 0:w], lam_ref[:, w:2 * w], n_tiles, True)
        lam_ref[:, 0:w] = lr
        lam_ref[:, w:2 * w] = li
        lb = l_ref[...].astype(MXU_DTYPE)
        db_ref[...] += lax.dot_general(ub, lb, (((0,), (0,)), ((), ())), preferred_element_type=F32)
        du = lax.dot_general(lb, b_ref[...], (((1,), (1,)), ((), ())), preferred_element_type=F32)
        du_ref[...] = (du + dua_ref[...].astype(F32)).astype(du_ref.dtype)

        def tile(j, carry):
            pr, pi, accr, acci = carry
            rows = pl.ds(pl.multiple_of(j * SUBLANES, SUBLANES), SUBLANES)
            sr = s_ref[rows, 0:w]
            si = s_ref[rows, w:2 * w]
            first = lax.broadcasted_iota(jnp.int32, (SUBLANES, w), 0) == 0
            sr_prev = jnp.where(first, pr, pltpu.roll(sr, 1, 0))
            si_prev = jnp.where(first, pi, pltpu.roll(si, 1, 0))
            gr = l_ref[rows, 0:w]
            gi = l_ref[rows, w:2 * w]
            accr = accr + gr * sr_prev + gi * si_prev
            acci = acci + gi * sr_prev - gr * si_prev
            return sr[SUBLANES - 1:SUBLANES], si[SUBLANES - 1:SUBLANES], accr, acci

        zero = jnp.zeros((SUBLANES, w), F32)
        _, _, accr, acci = lax.fori_loop(0, n_tiles, tile, (h_in_r, h_in_i, zero, zero))
        dar_ref[...] += accr
        dai_ref[...] += acci

    rev = lambda g, t: (nt - 1 - t, g)
    return pl.pallas_call(
        body, name="s5_bwd", grid=(N_GROUP_BLOCKS, nt),
        in_specs=[pl.BlockSpec((tt, 128), rev), pl.BlockSpec((tt, 128), rev), pl.BlockSpec((tt, 128), rev),
                  pl.BlockSpec((None, None, 1, 2 * w), lambda g, t: (g, nt - 1 - t, 0, 0)),
                  pl.BlockSpec((None, 128, 2 * w), lambda g, t: (g, 0, 0)),
                  pl.BlockSpec((None, 128, 2 * w), lambda g, t: (g, 0, 0)),
                  pl.BlockSpec((None, 8, SUBLANES, w), lambda g, t: (g, 0, 0, 0)),
                  pl.BlockSpec((None, 8, SUBLANES, w), lambda g, t: (g, 0, 0, 0))],
        out_specs=[pl.BlockSpec((tt, 128), rev),
                   pl.BlockSpec((None, 128, 2 * w), lambda g, t: (g, 0, 0)),
                   pl.BlockSpec((None, 2 * w, 128), lambda g, t: (g, 0, 0)),
                   pl.BlockSpec((None, SUBLANES, w), lambda g, t: (g, 0, 0)),
                   pl.BlockSpec((None, SUBLANES, w), lambda g, t: (g, 0, 0))],
        out_shape=[jax.ShapeDtypeStruct((seq, D_MODEL), MXU_DTYPE),
                   jax.ShapeDtypeStruct((N_GROUP_BLOCKS, 128, 2 * w), F32),
                   jax.ShapeDtypeStruct((N_GROUP_BLOCKS, 2 * w, 128), F32),
                   jax.ShapeDtypeStruct((N_GROUP_BLOCKS, SUBLANES, w), F32),
                   jax.ShapeDtypeStruct((N_GROUP_BLOCKS, SUBLANES, w), F32)],
        scratch_shapes=[pltpu.VMEM((tt, 2 * w), F32), pltpu.VMEM((tt, 2 * w), F32), pltpu.VMEM((1, 2 * w), F32)],
        compiler_params=_cparams(("parallel", "arbitrary")),
    )(u, dy, du_add, hs, bbd, cbd_t, tab_f, tab_b)


LOG2E = 1.4426950408889634
LN2 = 0.6931471805599453
SOFTPLUS2_LINEAR = 28.0


def _softplus2(z):
    u = 1.0 + jnp.exp2(z)
    return jnp.where(z > SOFTPLUS2_LINEAR, z, jnp.log2(u)), u


def _tri(n, keep, value=1.0):
    row = lax.broadcasted_iota(jnp.int32, (n, n), 0)
    col = lax.broadcasted_iota(jnp.int32, (n, n), 1)
    return jnp.where(keep(row, col), value, 0.0).astype(MXU_DTYPE)


def _suffix_sums(x, tri):
    c = tri.shape[0]
    outs, carry = [], None
    for i in reversed(range(x.shape[1] // c)):
        part = jnp.dot(x[:, i * c:(i + 1) * c].astype(MXU_DTYPE), tri, preferred_element_type=F32)
        if carry is not None:
            part = part + carry
        carry = part[:, 0:1]
        outs.append(part)
    return jnp.concatenate(outs[::-1], axis=1) if len(outs) > 1 else outs[0]


def _prefix_sums(x, tri, carry):
    c = tri.shape[0]
    outs = []
    for i in range(x.shape[1] // c):
        part = jnp.dot(x[:, i * c:(i + 1) * c].astype(MXU_DTYPE), tri, preferred_element_type=F32) + carry
        carry = part[:, c - 1:c]
        outs.append(part)
    return jnp.concatenate(outs, axis=1) if len(outs) > 1 else outs[0]


def _head_masks():
    lane = lax.broadcasted_iota(jnp.int32, (1, HEADS_PER_BLOCK * HEAD_DIM), 1)
    return [(lane >= h * HEAD_DIM) & (lane < (h + 1) * HEAD_DIM) for h in range(HEADS_PER_BLOCK)]


def _sba_fwd(q, k, v, *, tb=256):
    seq = q.shape[0]
    tb = min(tb, seq)
    nq = seq // tb
    cw = HEADS_PER_BLOCK * HEAD_DIM
    scale = HEAD_DIM ** -0.5

    def body(q_ref, k_ref, v_ref, o_ref, t_ref, acc_ref, r_ref):
        qi = pl.program_id(1)
        masks = _head_masks()
        qf = q_ref[...].astype(F32) * (scale * LOG2E)
        q2 = jnp.concatenate([jnp.where(m, qf, 0.0) for m in masks], axis=0).astype(MXU_DTYPE)
        causal = lax.broadcasted_iota(jnp.int32, (tb, tb), 1) < lax.broadcasted_iota(jnp.int32, (tb, tb), 0)
        causal2 = jnp.concatenate([causal] * HEADS_PER_BLOCK, axis=0)
        neg_upper = _tri(tb, lambda j, s: j >= s, -1.0)
        acc_ref[...] = jnp.zeros_like(acc_ref)
        r_ref[...] = jnp.zeros_like(r_ref)

        def block(start, width, diag):
            ks = pl.ds(pl.multiple_of(start, tb), width)
            kb = k_ref[ks, :].astype(MXU_DTYPE)
            vb = v_ref[ks, :]
            z = lax.dot_general(q2, kb, (((1,), (1,)), ((), ())), preferred_element_type=F32)
            sp, _ = _softplus2(z)
            if diag:
                sp = jnp.where(causal2, sp, 0.0)
            incl = _suffix_sums(sp, neg_upper)
            a = z + incl + r_ref[...]
            if diag:
                a = jnp.where(causal2, a, -1e30)
            wgt = jnp.exp2(a).astype(MXU_DTYPE)
            wcat = jnp.concatenate([wgt[h * tb:(h + 1) * tb] for h in range(HEADS_PER_BLOCK)], axis=1)
            vcat = jnp.concatenate([jnp.where(m, vb, jnp.zeros_like(vb)) for m in masks], axis=0).astype(MXU_DTYPE)
            acc_ref[...] += jnp.dot(wcat, vcat, preferred_element_type=F32)
            r_ref[...] += incl[:, 0:1]

        block(qi * tb, tb, True)
        n_wide = qi // KEY_BLOCKS_PER_STEP

        def single(i, c):
            block((qi - 1 - i) * tb, tb, False)
            return c

        lax.fori_loop(0, qi - n_wide * KEY_BLOCKS_PER_STEP, single, 0)

        def step(i, c):
            block((n_wide - 1 - i) * (KEY_BLOCKS_PER_STEP * tb), KEY_BLOCKS_PER_STEP * tb, False)
            return c

        lax.fori_loop(0, n_wide, step, 0)
        o_ref[...] = acc_ref[...].astype(o_ref.dtype)
        for h in range(HEADS_PER_BLOCK):
            t_ref[:, h:h + 1] = r_ref[h * tb:(h + 1) * tb]

    n_hp = D_MODEL // cw
    return pl.pallas_call(
        body, name="sba_fwd", grid=(n_hp, nq),
        in_specs=[pl.BlockSpec((tb, cw), lambda h, i: (i, h)),
                  pl.BlockSpec((seq, cw), lambda h, i: (0, h)),
                  pl.BlockSpec((seq, cw), lambda h, i: (0, h))],
        out_specs=[pl.BlockSpec((tb, cw), lambda h, i: (i, h)),
                   pl.BlockSpec((None, tb, HEADS_PER_BLOCK), lambda h, i: (h, i, 0))],
        out_shape=[jax.ShapeDtypeStruct((seq, D_MODEL), MXU_DTYPE),
                   jax.ShapeDtypeStruct((n_hp, seq, HEADS_PER_BLOCK), F32)],
        scratch_shapes=[pltpu.VMEM((tb, cw), F32), pltpu.VMEM((HEADS_PER_BLOCK * tb, 1), F32)],
        compiler_params=_cparams(("parallel", "arbitrary")),
    )(q, k, v)


def _sba_bwd(q, k, v, do, tot, *, tb=256):
    seq = q.shape[0]
    tb = min(tb, seq)
    nq = seq // tb
    cw = HEADS_PER_BLOCK * HEAD_DIM
    scale = HEAD_DIM ** -0.5

    def body(q_ref, k_ref, v_ref, do_ref, t_ref, dq_ref, dk_ref, dv_ref, acc_ref, lp_ref, dp_ref):
        qi = pl.program_id(1)

        @pl.when(qi == 0)
        def _():
            dk_ref[...] = jnp.zeros_like(dk_ref)
            dv_ref[...] = jnp.zeros_like(dv_ref)

        masks = _head_masks()
        qf = q_ref[...].astype(F32) * (scale * LOG2E)
        q2 = jnp.concatenate([jnp.where(m, qf, 0.0) for m in masks], axis=0).astype(MXU_DTYPE)
        dof = do_ref[...].astype(F32)
        do2 = jnp.concatenate([jnp.where(m, dof, 0.0) for m in masks], axis=0).astype(MXU_DTYPE)
        tot2 = jnp.concatenate([t_ref[:, h:h + 1] for h in range(HEADS_PER_BLOCK)], axis=0)
        causal = lax.broadcasted_iota(jnp.int32, (tb, tb), 1) < lax.broadcasted_iota(jnp.int32, (tb, tb), 0)
        causal2 = jnp.concatenate([causal] * HEADS_PER_BLOCK, axis=0)
        neg_upper = _tri(tb, lambda j, s: j >= s, -1.0)
        lower = _tri(tb, lambda j, s: j <= s)
        acc_ref[...] = jnp.zeros_like(acc_ref)
        lp_ref[...] = jnp.zeros_like(lp_ref)
        dp_ref[...] = jnp.zeros_like(dp_ref)

        def block(start, width, diag):
            ks = pl.ds(pl.multiple_of(start, tb), width)
            kb = k_ref[ks, :]
            vb = v_ref[ks, :].astype(MXU_DTYPE)
            z = lax.dot_general(q2, kb.astype(MXU_DTYPE), (((1,), (1,)), ((), ())), preferred_element_type=F32)
            sp, u = _softplus2(z)
            if diag:
                sp = jnp.where(causal2, sp, 0.0)
            incl = _suffix_sums(sp, neg_upper)
            lp_next = lp_ref[...] + incl[:, 0:1]
            lp_ref[...] = lp_next
            a = z + incl + (tot2 - lp_next)
            if diag:
                a = jnp.where(causal2, a, -1e30)
            wgt = jnp.exp2(a)
            dw = lax.dot_general(do2, vb, (((1,), (1,)), ((), ())), preferred_element_type=F32)
            da = dw * wgt
            pre = _prefix_sums(da, lower, dp_ref[...])
            dp_ref[...] = pre[:, width - 1:width]
            sig = 1.0 - pl.reciprocal(u, approx=True)
            dz = da - pre * sig
            if diag:
                dz = jnp.where(causal2, dz, 0.0)
            dzb = dz.astype(MXU_DTYPE)
            dzcat = jnp.concatenate([dzb[h * tb:(h + 1) * tb] for h in range(HEADS_PER_BLOCK)], axis=1)
            kcat = jnp.concatenate([jnp.where(m, kb, jnp.zeros_like(kb)) for m in masks], axis=0).astype(MXU_DTYPE)
            acc_ref[...] += jnp.dot(dzcat, kcat, preferred_element_type=F32)
            dk_ref[ks, :] += lax.dot_general(dzb, q2, (((0,), (0,)), ((), ())), preferred_element_type=F32) * LN2
            dv_ref[ks, :] += lax.dot_general(wgt.astype(MXU_DTYPE), do2, (((0,), (0,)), ((), ())),
                                             preferred_element_type=F32)

        n_wide = qi // KEY_BLOCKS_PER_STEP

        def step(i, c):
            block(i * (KEY_BLOCKS_PER_STEP * tb), KEY_BLOCKS_PER_STEP * tb, False)
            return c

        lax.fori_loop(0, n_wide, step, 0)

        def single(i, c):
            block((n_wide * KEY_BLOCKS_PER_STEP + i) * tb, tb, False)
            return c

        lax.fori_loop(0, qi - n_wide * KEY_BLOCKS_PER_STEP, single, 0)
        block(qi * tb, tb, True)
        dq_ref[...] = acc_ref[...] * scale

    n_hp = D_MODEL // cw
    qspec = pl.BlockSpec((tb, cw), lambda h, i: (i, h))
    full = pl.BlockSpec((seq, cw), lambda h, i: (0, h))
    return pl.pallas_call(
        body, name="sba_bwd", grid=(n_hp, nq),
        in_specs=[qspec, full, full, qspec, pl.BlockSpec((None, tb, HEADS_PER_BLOCK), lambda h, i: (h, i, 0))],
        out_specs=[qspec, full, full],
        out_shape=[jax.ShapeDtypeStruct((seq, D_MODEL), F32)] * 3,
        scratch_shapes=[pltpu.VMEM((tb, cw), F32), pltpu.VMEM((HEADS_PER_BLOCK * tb, 1), F32),
                        pltpu.VMEM((HEADS_PER_BLOCK * tb, 1), F32)],
        compiler_params=_cparams(("parallel", "arbitrary")),
    )(q, k, v, do, tot)


def _my_place():
    return lax.axis_index("x"), lax.axis_index("y"), lax.axis_index("c")


def _other_chips(x, y):
    return [(1 - x, y), (x, 1 - y), (1 - x, 1 - y)]


def _all_gather_chips(slabs, *, name):
    n = len(slabs)

    def body(*refs):
        ins, outs = refs[:n], refs[n:2 * n]
        send_sems, recv_sems, loc_sems = refs[2 * n:]
        x, y, c = _my_place()
        me = 2 * x + y
        local = [pltpu.make_async_copy(ins[i], outs[i].at[me], loc_sems.at[i]) for i in range(n)]
        for cp in local:
            cp.start()
        remote = []
        for j, (px, py) in enumerate(_other_chips(x, y)):
            for i in range(n):
                remote.append(pltpu.make_async_remote_copy(
                    src_ref=ins[i], dst_ref=outs[i].at[me], send_sem=send_sems.at[j * n + i],
                    recv_sem=recv_sems.at[j * n + i], device_id=(px, py, c), device_id_type=MESH))
        for cp in remote:
            cp.start()
        for cp in remote:
            cp.wait_recv()
        for cp in remote:
            cp.wait_send()
        for cp in local:
            cp.wait()

    return pl.pallas_call(
        body, name=name, in_specs=[ANY] * n, out_specs=[ANY] * n,
        out_shape=[jax.ShapeDtypeStruct((4,) + s.shape, s.dtype) for s in slabs],
        scratch_shapes=[pltpu.SemaphoreType.DMA((3 * n,)), pltpu.SemaphoreType.DMA((3 * n,)),
                        pltpu.SemaphoreType.DMA((n,))],
    )(*slabs)


def _all_gather_chips_two_level(slabs, *, name):
    n = len(slabs)
    halves = [s.shape[0] // 2 for s in slabs]

    def body(*refs):
        ins, outs = refs[:n], refs[n:2 * n]
        ici_send, ici_recv, d2d_send, d2d_recv = refs[2 * n:]
        x, y, c = _my_place()
        chips = _other_chips(x, y)
        mine = [pl.ds(c * h, h) for h in halves]
        theirs = [pl.ds((1 - c) * h, h) for h in halves]

        def over_ici(j, i, chip):
            px, py = chips[j]
            slot = 2 * chip[0] + chip[1]
            return pltpu.make_async_remote_copy(
                src_ref=ins[i].at[mine[i]], dst_ref=outs[i].at[slot, mine[i]], send_sem=ici_send.at[j * n + i],
                recv_sem=ici_recv.at[j * n + i], device_id=(px, py, c), device_id_type=MESH)

        def over_d2d(j, i, rows):
            px, py = chips[j]
            block = outs[i].at[2 * px + py, rows]
            return pltpu.make_async_remote_copy(
                src_ref=block, dst_ref=block, send_sem=d2d_send.at[j * n + i], recv_sem=d2d_recv.at[j * n + i],
                device_id=(x, y, 1 - c), device_id_type=MESH)

        sent = [over_ici(j, i, (x, y)) for j in range(3) for i in range(n)]
        for cp in sent:
            cp.start()
        passed = []
        for j in range(3):
            for i in range(n):
                over_ici(j, i, chips[j]).wait_recv()
                cp = over_d2d(j, i, mine[i])
                cp.start()
                passed.append(cp)
        for j in range(3):
            for i in range(n):
                over_d2d(j, i, theirs[i]).wait_recv()
        for cp in sent + passed:
            cp.wait_send()

    outs = pl.pallas_call(
        body, name=name, in_specs=[ANY] * n, out_specs=[ANY] * n,
        out_shape=[jax.ShapeDtypeStruct((4,) + s.shape, s.dtype) for s in slabs],
        scratch_shapes=[pltpu.SemaphoreType.DMA((3 * n,)), pltpu.SemaphoreType.DMA((3 * n,)),
                        pltpu.SemaphoreType.DMA((3 * n,)), pltpu.SemaphoreType.DMA((3 * n,))],
    )(*slabs)
    me = 2 * lax.axis_index("x") + lax.axis_index("y")
    return [[jnp.where(me == j, s, o[j]) for j in range(4)] for o, s in zip(outs, slabs)]


def _sibling_exchange(send, *, name):
    def body(s_ref, o_ref, send_sem, recv_sem):
        x, y, c = _my_place()
        cp = pltpu.make_async_remote_copy(src_ref=s_ref, dst_ref=o_ref, send_sem=send_sem, recv_sem=recv_sem,
                                          device_id=(x, y, 1 - c), device_id_type=MESH)
        cp.start()
        cp.wait_recv()
        cp.wait_send()

    return pl.pallas_call(
        body, name=name, in_specs=[ANY], out_specs=ANY,
        out_shape=jax.ShapeDtypeStruct(send.shape, send.dtype),
        scratch_shapes=[pltpu.SemaphoreType.DMA, pltpu.SemaphoreType.DMA],
    )(send)


def _all_to_all_chips(parts, *, name):
    def body(p_ref, o_ref, send_sems, recv_sems):
        x, y, c = _my_place()
        me = 2 * x + y
        remote = []
        for j, (px, py) in enumerate(_other_chips(x, y)):
            remote.append(pltpu.make_async_remote_copy(
                src_ref=p_ref.at[2 * px + py], dst_ref=o_ref.at[me], send_sem=send_sems.at[j],
                recv_sem=recv_sems.at[j], device_id=(px, py, c), device_id_type=MESH))
        for cp in remote:
            cp.start()
        for cp in remote:
            cp.wait_recv()
        for cp in remote:
            cp.wait_send()

    out = pl.pallas_call(
        body, name=name, in_specs=[ANY], out_specs=ANY,
        out_shape=jax.ShapeDtypeStruct(parts.shape, parts.dtype),
        scratch_shapes=[pltpu.SemaphoreType.DMA((3,)), pltpu.SemaphoreType.DMA((3,))],
    )(parts)
    me = 2 * lax.axis_index("x") + lax.axis_index("y")
    return [jnp.where(me == j, parts[j], out[j]) for j in range(4)]


_SHARDED = [("a_norm_pre", 1), ("a_norm_post", 1), ("a_d_skip", 1), ("a_b_glu", 1), ("a_w_in", 2), ("a_w_glu", 1),
            ("a_w_out", 1), ("w_kv", 1), ("b_w_in", 2), ("b_w_out", 1), ("ple_w_proj", 2), ("ple_w_gate", 1)]
_REPLICATED = ["a_lam_re", "a_lam_im", "a_log_dt", "a_b_re", "a_b_im", "a_c_re", "a_c_im", "kv_norm", "b_norm_pre",
               "b_norm_post"]
_WEIGHT_ORDER = ["a_norm_pre", "a_norm_post", "a_w_in", "a_lam_re", "a_lam_im", "a_log_dt", "a_b_re", "a_b_im",
                 "a_c_re", "a_c_im", "a_d_skip", "a_w_glu", "a_b_glu", "a_w_out", "kv_norm", "w_kv", "b_norm_pre",
                 "b_norm_post", "b_w_in", "b_w_out", "ple_w_proj", "ple_w_gate"]


def _round_up(n, m):
    return (n + m - 1) // m * m


def _pack(pieces, rows):
    flat = jnp.concatenate([p.reshape(-1).astype(F32) for p in pieces])
    flat = jnp.pad(flat, (0, rows * SLAB_W - flat.shape[0]))
    return flat.reshape(rows, SLAB_W)


def _unpack(slab, shapes):
    flat = slab.reshape(-1)
    out, off = [], 0
    for s in shapes:
        n = math.prod(s)
        out.append(flat[off:off + n].reshape(s))
        off += n
    return out


def _pick_tile(n, cap):
    return max(t for t in range(SUBLANES, cap + 1, SUBLANES) if n % t == 0)


def _size_rows(shapes, mult):
    return _round_up(_round_up(sum(math.prod(s) for s in shapes), SLAB_W) // SLAB_W, mult)


def _adamw_tile(w, g, m, v):
    m2 = ADAM_B1 * m + (1.0 - ADAM_B1) * g
    v2 = ADAM_B2 * v + (1.0 - ADAM_B2) * (g * g)
    m_hat = m2 / (1.0 - ADAM_B1 ** ADAM_STEP)
    v_hat = v2 / (1.0 - ADAM_B2 ** ADAM_STEP)
    delta = -ADAM_LR * (m_hat / (jnp.sqrt(v_hat) + ADAM_EPS) + ADAM_WD * w)
    return delta, m2, v2


def _forward_backward(x0, p0, p1, tgt, W):
    seq = x0.shape[0]
    sw = D_MODEL
    g = {}

    h1, = _ew(lambda xt, gn: (_rms(xt, gn),), [x0], [W["a_norm_pre"]], [MXU_DTYPE], 0, name="a_pre_norm")
    w_u, w_ga = W["a_w_in"][:, :sw], W["a_w_in"][:, sw:]
    u = _mm(h1, w_u, name="a_in_u")
    gate = _mm(h1, w_ga, out_dtype=MXU_DTYPE, name="a_in_gate")
    a_re, a_im, bbt_re, bbt_im, pw_re, pw_im = _s5_params_fwd(W["lam_re"], W["lam_im"], W["log_dt"], W["bt_re"],
                                                               W["bt_im"])
    nb, gpb = N_GROUP_BLOCKS, GROUPS_PER_BLOCK
    bbd = jnp.concatenate([_block_diag(bbt_re.reshape(nb, gpb, GROUP_SIZE, STATE)),
                           _block_diag(bbt_im.reshape(nb, gpb, GROUP_SIZE, STATE))], axis=2).astype(MXU_DTYPE)
    crt = W["c_re"].transpose(0, 2, 1).reshape(nb, gpb, STATE, GROUP_SIZE)
    cit = W["c_im"].transpose(0, 2, 1).reshape(nb, gpb, STATE, GROUP_SIZE)
    cbd = jnp.concatenate([_block_diag(crt), -_block_diag(cit)], axis=1).astype(MXU_DTYPE)
    cbd_t = cbd.transpose(0, 2, 1)
    tab_f, tab_b = _scan_tables(pw_re.reshape(SUBLANES, nb, SSM_LANES), pw_im.reshape(SUBLANES, nb, SSM_LANES))
    ys, hs = _s5_fwd(u, bbd, cbd, tab_f)
    gl, = _ew(lambda a, b, d: (_gelu_skip(a, b, d),), [ys, u], [W["a_d_skip"]], [MXU_DTYPE], 0, name="a_gelu")
    t = _mm(gl, W["a_w_glu"], out_dtype=MXU_DTYPE, name="a_glu")
    y3, = _ew(lambda a, b, c, d: (_glu_gate(a, b, c, d),), [gl, t, gate], [W["a_b_glu"]], [MXU_DTYPE], 0, name="a_gate")
    y4 = _mm(y3, W["a_w_out"], out_dtype=MXU_DTYPE, name="a_out")
    x1, = _ew(lambda a, b, gn: (a + _rms(b, gn),), [x0, y4], [W["a_norm_post"]], [F32], 0, name="a_post_norm")
    gt0 = _mm(x1, W["ple_w_gate"][0], out_dtype=MXU_DTYPE, name="ple0_gate")
    pp0 = _mm(p0, W["ple_w_proj"][0], out_dtype=MXU_DTYPE, name="ple0_proj")
    x2, hk, h2 = _ew(lambda a, b, c, g1, g2: (_ple(a, b, c), _rms(_ple(a, b, c), g1), _rms(_ple(a, b, c), g2)),
                     [x1, gt0, pp0], [W["kv_norm"], W["b_norm_pre"]], [F32, MXU_DTYPE, MXU_DTYPE], 0, name="ple0_mix")

    w_k, w_v = W["w_kv"][:, :sw], W["w_kv"][:, sw:]
    k = _mm(hk, w_k, out_dtype=MXU_DTYPE, name="kv_k")
    v = _mm(hk, w_v, out_dtype=MXU_DTYPE, name="kv_v")
    w_q, w_gb = W["b_w_in"][:, :sw], W["b_w_in"][:, sw:]
    q = _mm(h2, w_q, out_dtype=MXU_DTYPE, name="b_in_q")
    gate2 = _mm(h2, w_gb, out_dtype=MXU_DTYPE, name="b_in_gate")
    o, tot = _sba_fwd(q, k, v)
    y5in, = _ew(lambda a, b: (_ogate(a, b),), [o, gate2], [], [MXU_DTYPE], 0, name="b_gate")
    y5 = _mm(y5in, W["b_w_out"], out_dtype=MXU_DTYPE, name="b_out")
    x3, = _ew(lambda a, b, gn: (a + _rms(b, gn),), [x2, y5], [W["b_norm_post"]], [F32], 0, name="b_post_norm")
    gt1 = _mm(x3, W["ple_w_gate"][1], out_dtype=MXU_DTYPE, name="ple1_gate")
    pp1 = _mm(p1, W["ple_w_proj"][1], out_dtype=MXU_DTYPE, name="ple1_proj")

    def loss_fn(xt, gt, pp, tg):
        s = jax.nn.sigmoid(gt)
        d = (xt + s * pp - tg) * (1.0 / D_MODEL)
        lsum = jnp.sum(d * d, axis=(0, 1), keepdims=True) * (0.5 * D_MODEL)
        return d, d * pp * s * (1.0 - s), d * s, lsum

    dx3a, dgt1, dpp1, loss = _ew(loss_fn, [x3, gt1, pp1, tgt], [], [F32, MXU_DTYPE, MXU_DTYPE], 1, name="loss_head")
    g_gate1 = _mm(x3, dgt1, ta=True, name="ple1_gate_dw")
    g_proj1 = _mm(p1, dpp1, ta=True, name="ple1_proj_dw")
    dx3 = _mm(dgt1, W["ple_w_gate"][1], tb=True, add=dx3a, name="ple1_gate_dx")

    def post_norm_bwd(ct, yt, gn):
        _, vjp = jax.vjp(_rms, yt, gn)
        dy, dg = vjp(ct)
        return dy, dg

    dy5, g["b_norm_post"] = _ew(post_norm_bwd, [dx3, y5], [W["b_norm_post"]], [MXU_DTYPE], 1, name="b_post_norm_bwd")
    g["b_w_out"] = _mm(y5in, dy5, ta=True, name="b_out_dw")
    dy5in = _mm(dy5, W["b_w_out"], tb=True, out_dtype=MXU_DTYPE, name="b_out_dx")

    def ogate_bwd(ct, ot, gt):
        _, vjp = jax.vjp(_ogate, ot, gt)
        return vjp(ct)

    do, dgate2 = _ew(ogate_bwd, [dy5in, o, gate2], [], [MXU_DTYPE, MXU_DTYPE], 0, name="b_gate_bwd")
    dq, dk, dv = _sba_bwd(q, k, v, do, tot)
    g["b_w_in"] = jnp.concatenate([_mm(h2, dq, ta=True, name="b_in_q_dw"),
                                   _mm(h2, dgate2, ta=True, name="b_in_gate_dw")], axis=1)
    dh2 = _mm(dgate2, w_gb, tb=True, add=_mm(dq, w_q, tb=True, name="b_in_q_dx"), out_dtype=MXU_DTYPE,
              name="b_in_gate_dx")
    g["w_kv"] = jnp.concatenate([_mm(hk, dk, ta=True, name="kv_k_dw"), _mm(hk, dv, ta=True, name="kv_v_dw")], axis=1)
    dhk = _mm(dv, w_v, tb=True, add=_mm(dk, w_k, tb=True, name="kv_k_dx"), out_dtype=MXU_DTYPE, name="kv_v_dx")

    def mix_bwd(ct, c2, ck, xt, gt, pp, g1, g2):
        x2v, vjp_ple = jax.vjp(_ple, xt, gt, pp)
        _, vjp_k = jax.vjp(_rms, x2v, g1)
        _, vjp_b = jax.vjp(_rms, x2v, g2)
        dxk, dg1 = vjp_k(ck)
        dxb, dg2 = vjp_b(c2)
        dx2 = ct + dxk + dxb
        dx1, dgt, dpp = vjp_ple(dx2)
        return dx1, dgt, dpp, dg1, dg2

    dx1a, dgt0, dpp0, g["kv_norm"], g["b_norm_pre"] = _ew(
        mix_bwd, [dx3, dh2, dhk, x1, gt0, pp0], [W["kv_norm"], W["b_norm_pre"]], [F32, MXU_DTYPE, MXU_DTYPE], 2,
        name="ple0_mix_bwd")
    g_gate0 = _mm(x1, dgt0, ta=True, name="ple0_gate_dw")
    g_proj0 = _mm(p0, dpp0, ta=True, name="ple0_proj_dw")
    dx1 = _mm(dgt0, W["ple_w_gate"][0], tb=True, add=dx1a, name="ple0_gate_dx")
    g["ple_w_gate"] = jnp.stack([g_gate0, g_gate1])
    g["ple_w_proj"] = jnp.stack([g_proj0, g_proj1])

    dy4, g["a_norm_post"] = _ew(post_norm_bwd, [dx1, y4], [W["a_norm_post"]], [MXU_DTYPE], 1, name="a_post_norm_bwd")
    g["a_w_out"] = _mm(y3, dy4, ta=True, name="a_out_dw")
    dy3 = _mm(dy4, W["a_w_out"], tb=True, out_dtype=MXU_DTYPE, name="a_out_dx")

    def gate_bwd(ct, gt_, tt_, gat, bg):
        _, vjp = jax.vjp(_glu_gate, gt_, tt_, gat, bg)
        dg_, dt_, dgate_, dbg = vjp(ct)
        return dg_, dt_, dgate_, dbg

    dgl_a, dt, dgate, g["a_b_glu"] = _ew(gate_bwd, [dy3, gl, t, gate], [W["a_b_glu"]], [F32, MXU_DTYPE, MXU_DTYPE], 1,
                                         name="a_gate_bwd")
    g["a_w_glu"] = _mm(gl, dt, ta=True, name="a_glu_dw")
    dgl = _mm(dt, W["a_w_glu"], tb=True, add=dgl_a, name="a_glu_dx")

    def gelu_bwd(ct, yt, ut, ds):
        _, vjp = jax.vjp(_gelu_skip, yt, ut, ds)
        return vjp(ct)

    dys, du_a, g["a_d_skip"] = _ew(gelu_bwd, [dgl, ys, u], [W["a_d_skip"]], [MXU_DTYPE, F32], 1, name="a_gelu_bwd")
    du, d_bbd, d_cbd, d_ar8, d_ai8 = _s5_bwd(u, dys, du_a, hs, bbd, cbd_t, tab_f, tab_b)
    g["a_w_in"] = jnp.concatenate([_mm(h1, du, ta=True, name="a_in_u_dw"),
                                   _mm(h1, dgate, ta=True, name="a_in_gate_dw")], axis=1)
    dh1 = _mm(dgate, w_ga, tb=True, add=_mm(du, w_u, tb=True, name="a_in_u_dx"), out_dtype=MXU_DTYPE,
              name="a_in_gate_dx")

    def pre_norm_bwd(ct, ch, xt, gn):
        _, vjp = jax.vjp(_rms, xt, gn)
        dx, dg = vjp(ch)
        return ct + dx, dg

    dx0, g["a_norm_pre"] = _ew(pre_norm_bwd, [dx1, dh1, x0], [W["a_norm_pre"]], [F32], 1, name="a_pre_norm_bwd")

    w = SSM_LANES
    d_bbt_re = _block_diag_extract(d_bbd[:, :, :w], GROUP_SIZE, STATE).reshape(N_GROUPS, GROUP_SIZE, STATE)
    d_bbt_im = _block_diag_extract(d_bbd[:, :, w:], GROUP_SIZE, STATE).reshape(N_GROUPS, GROUP_SIZE, STATE)
    d_crt = _block_diag_extract(d_cbd[:, :w, :], STATE, GROUP_SIZE).reshape(N_GROUPS, STATE, GROUP_SIZE)
    d_cit = -_block_diag_extract(d_cbd[:, w:, :], STATE, GROUP_SIZE).reshape(N_GROUPS, STATE, GROUP_SIZE)
    g["a_c_re"] = d_crt.transpose(0, 2, 1)[None]
    g["a_c_im"] = d_cit.transpose(0, 2, 1)[None]
    d_lr, d_li, d_ldt, d_btr, d_bti = _s5_params_bwd(
        W["lam_re"], W["lam_im"], W["log_dt"], W["bt_re"], W["bt_im"], _per_sublane(d_ar8), _per_sublane(d_ai8),
        d_bbt_re, d_bbt_im)
    g["a_lam_re"], g["a_lam_im"] = d_lr[None], d_li[None]
    g["a_log_dt"] = d_ldt.reshape(1, N_GROUPS)
    g["a_b_re"] = d_btr.transpose(0, 2, 1)[None]
    g["a_b_im"] = d_bti.transpose(0, 2, 1)[None]
    return loss, dx0, g


def kernel(x, p, a_norm_pre, a_norm_post, a_w_in, a_lam_re, a_lam_im, a_log_dt, a_b_re, a_b_im, a_c_re, a_c_im, a_d_skip, a_w_glu, a_b_glu, a_w_out, kv_norm, w_kv, b_norm_pre, b_norm_post, b_w_in, b_w_out, ple_w_proj, ple_w_gate, loss_target, m_a_norm_pre, m_a_norm_post, m_a_w_in, m_a_lam_re, m_a_lam_im, m_a_log_dt, m_a_b_re, m_a_b_im, m_a_c_re, m_a_c_im, m_a_d_skip, m_a_w_glu, m_a_b_glu, m_a_w_out, m_kv_norm, m_w_kv, m_b_norm_pre, m_b_norm_post, m_b_w_in, m_b_w_out, m_ple_w_proj, m_ple_w_gate, v_a_norm_pre, v_a_norm_post, v_a_w_in, v_a_lam_re, v_a_lam_im, v_a_log_dt, v_a_b_re, v_a_b_im, v_a_c_re, v_a_c_im, v_a_d_skip, v_a_w_glu, v_a_b_glu, v_a_w_out, v_kv_norm, v_w_kv, v_b_norm_pre, v_b_norm_post, v_b_w_in, v_b_w_out, v_ple_w_proj, v_ple_w_gate):
    loc = dict(a_norm_pre=a_norm_pre, a_norm_post=a_norm_post, a_w_in=a_w_in, a_lam_re=a_lam_re, a_lam_im=a_lam_im,
               a_log_dt=a_log_dt, a_b_re=a_b_re, a_b_im=a_b_im, a_c_re=a_c_re, a_c_im=a_c_im, a_d_skip=a_d_skip,
               a_w_glu=a_w_glu, a_b_glu=a_b_glu, a_w_out=a_w_out, kv_norm=kv_norm, w_kv=w_kv, b_norm_pre=b_norm_pre,
               b_norm_post=b_norm_post, b_w_in=b_w_in, b_w_out=b_w_out, ple_w_proj=ple_w_proj, ple_w_gate=ple_w_gate)
    mom = dict(a_norm_pre=m_a_norm_pre, a_norm_post=m_a_norm_post, a_w_in=m_a_w_in, a_lam_re=m_a_lam_re,
               a_lam_im=m_a_lam_im, a_log_dt=m_a_log_dt, a_b_re=m_a_b_re, a_b_im=m_a_b_im, a_c_re=m_a_c_re,
               a_c_im=m_a_c_im, a_d_skip=m_a_d_skip, a_w_glu=m_a_w_glu, a_b_glu=m_a_b_glu, a_w_out=m_a_w_out,
               kv_norm=m_kv_norm, w_kv=m_w_kv, b_norm_pre=m_b_norm_pre, b_norm_post=m_b_norm_post, b_w_in=m_b_w_in,
               b_w_out=m_b_w_out, ple_w_proj=m_ple_w_proj, ple_w_gate=m_ple_w_gate)
    var = dict(a_norm_pre=v_a_norm_pre, a_norm_post=v_a_norm_post, a_w_in=v_a_w_in, a_lam_re=v_a_lam_re,
               a_lam_im=v_a_lam_im, a_log_dt=v_a_log_dt, a_b_re=v_a_b_re, a_b_im=v_a_b_im, a_c_re=v_a_c_re,
               a_c_im=v_a_c_im, a_d_skip=v_a_d_skip, a_w_glu=v_a_w_glu, a_b_glu=v_a_b_glu, a_w_out=v_a_w_out,
               kv_norm=v_kv_norm, w_kv=v_w_kv, b_norm_pre=v_b_norm_pre, b_norm_post=v_b_norm_post, b_w_in=v_b_w_in,
               b_w_out=v_b_w_out, ple_w_proj=v_ple_w_proj, ple_w_gate=v_ple_w_gate)
    vec_names = [n for n, _ in _SHARDED if loc[n].shape[-2] == 1 and loc[n].ndim == 2]
    mat_names = [n for n, _ in _SHARDED if n not in vec_names]
    axis_of = dict(_SHARDED)
    mat_shapes = [loc[n].shape for n in mat_names]
    vec_shapes = [loc[n].shape for n in vec_names]
    mat_rows = _size_rows(mat_shapes, 32)
    vec_rows = _size_rows(vec_shapes, 16)
    mat_slab = _pack([loc[n] for n in mat_names], mat_rows).astype(MXU_DTYPE)
    vec_slab = _pack([loc[n] for n in vec_names], vec_rows)
    mat_all, vec_all = _all_gather_chips_two_level([mat_slab, vec_slab], name="gather_weights")
    W = {}
    for names, slab_all, shapes in ((mat_names, mat_all, mat_shapes), (vec_names, vec_all, vec_shapes)):
        per_chip = [_unpack(slab_all[j], shapes) for j in range(4)]
        for i, n in enumerate(names):
            W[n] = jnp.concatenate([per_chip[j][i] for j in range(4)], axis=axis_of[n])
    for n in ("a_w_in", "a_w_glu", "a_w_out", "b_w_in", "b_w_out"):
        W[n] = W[n][0]
    W["kv_norm"] = kv_norm.reshape(1, D_MODEL)
    W["b_norm_pre"], W["b_norm_post"] = b_norm_pre, b_norm_post
    W["lam_re"], W["lam_im"] = a_lam_re[0], a_lam_im[0]
    W["log_dt"] = a_log_dt.reshape(N_GROUPS, 1)
    W["bt_re"], W["bt_im"] = a_b_re[0].transpose(0, 2, 1), a_b_im[0].transpose(0, 2, 1)
    W["c_re"], W["c_im"] = a_c_re[0], a_c_im[0]

    loss, dx0, g = _forward_backward(x[0], p[0, 0], p[1, 0], loss_target[0], W)
    for n in ("a_w_in", "a_w_glu", "a_w_out", "b_w_in", "b_w_out"):
        g[n] = g[n][None]
    g["kv_norm"] = g["kv_norm"].reshape(D_MODEL)
    loss = lax.psum(loss[0, 0], ("x", "y", "c"))

    rep_shapes = [loc[n].shape for n in _REPLICATED]
    rep_rows = _size_rows(rep_shapes, 32)
    rep_slab = _pack([g[n] for n in _REPLICATED], rep_rows).reshape(4, rep_rows // 4, SLAB_W)
    sh_names = [n for n, _ in _SHARDED]
    sh_shapes = [loc[n].shape for n in sh_names]
    sh_rows = _size_rows(sh_shapes, 8)
    rows = _round_up(sh_rows + rep_rows // 4, 1024)
    half = rows // 2
    slabs = []
    for j in range(4):
        pieces = [jnp.split(g[n], 4, axis=axis_of[n])[j] for n in sh_names]
        sh = _pack(pieces, sh_rows)
        slabs.append(jnp.concatenate([sh, rep_slab[j], jnp.zeros((rows - sh_rows - rep_rows // 4, SLAB_W), F32)]))
    gs = jnp.stack(slabs).reshape(4, 2, half, SLAB_W)
    ci = lax.axis_index("c")
    mine = lax.dynamic_index_in_dim(gs, ci, axis=1, keepdims=False).reshape(4 * half, SLAB_W)
    theirs = lax.dynamic_index_in_dim(gs, 1 - ci, axis=1, keepdims=False).reshape(4 * half, SLAB_W)
    got = _sibling_exchange(theirs, name="reduce_cores")
    part, = _ew(lambda a, b: (a + b,), [mine, got], [], [MXU_DTYPE], 0, name="reduce_cores_add", tr=512)
    recv = _all_to_all_chips(part.reshape(4, half, SLAB_W), name="reduce_chips")
    tot_half, = _ew(lambda a, b, c, d: (((a + b) + c) + d,), [recv[0], recv[1], recv[2], recv[3]], [], [F32], 0,
                    name="reduce_chips_add", tr=512)
    other_half = _sibling_exchange(tot_half, name="share_cores")
    lo_half = jnp.where(ci == 0, tot_half, other_half)
    hi_half = jnp.where(ci == 0, other_half, tot_half)
    gsum = jnp.concatenate([lo_half, hi_half])
    rep_q = gsum[sh_rows:sh_rows + rep_rows // 4]
    rep_all, = _all_gather_chips([rep_q], name="gather_replicated")
    gsum_full = jnp.concatenate([gsum[:sh_rows], rep_all.reshape(rep_rows, SLAB_W)])

    all_names = sh_names + _REPLICATED
    all_shapes = sh_shapes + rep_shapes

    def slab_of(d):
        return jnp.concatenate([_pack([d[n] for n in sh_names], sh_rows), _pack([d[n] for n in _REPLICATED], rep_rows)])

    n_rows = sh_rows + rep_rows
    outs = _ew(_adamw_tile, [slab_of(loc), gsum_full, slab_of(mom), slab_of(var)], [], [F32] * 3, 0, name="adamw",
               tr=_pick_tile(n_rows, 512))
    res = []
    for slab in [gsum_full] + list(outs):
        parts = _unpack(slab[:sh_rows], sh_shapes) + _unpack(slab[sh_rows:], rep_shapes)
        res.append(dict(zip(all_names, parts)))
    out = [loss, dx0[None]]
    for d in res:
        out += [d[n] for n in _WEIGHT_ORDER]
    return tuple(out)
```

```python
import functools
import math

import jax
import jax.numpy as jnp
from jax import lax
from jax.experimental import pallas as pl
from jax.experimental.pallas import tpu as pltpu

F32 = jnp.float32
MXU_DTYPE = jnp.bfloat16

D_MODEL = 1024
N_GROUPS = 64
GROUP_SIZE = 16
STATE = 64
GROUPS_PER_BLOCK = 8
SSM_LANES = GROUPS_PER_BLOCK * STATE
N_GROUP_BLOCKS = N_GROUPS // GROUPS_PER_BLOCK
HEAD_DIM = 64
HEADS_PER_BLOCK = 2
BWD_QUERY_BLOCK = 256
KEY_BLOCKS_PER_STEP = 4
FWD_QUERY_BLOCK = 512
FWD_KEY_BLOCKS_PER_STEP = 4
CUMSUM_CHUNK = 256
SCAN_TILES_PER_STEP = 2
EPS = 1e-6
SLAB_W = 512
SUBLANES = 8
VMEM_LIMIT = 56 * 1024 * 1024

ADAM_LR = 0.001
ADAM_B1 = 0.9
ADAM_B2 = 0.999
ADAM_EPS = 1e-08
ADAM_WD = 0.01
ADAM_STEP = 10

MESH = pl.DeviceIdType.MESH
ANY = pl.BlockSpec(memory_space=pl.ANY)


def _cparams(sem):
    return pltpu.CompilerParams(dimension_semantics=sem, vmem_limit_bytes=VMEM_LIMIT)


def _mm(a, b, *, ta=False, tb=False, add=None, out_dtype=F32, name, tm=1024, tn=1024, tk=1024):
    if ta:
        kdim, m = a.shape
    else:
        m, kdim = a.shape
    if tb:
        n, kb = b.shape
    else:
        kb, n = b.shape
    assert kdim == kb, (a.shape, b.shape, ta, tb)
    tm, tn, tk = min(tm, m), min(tn, n), min(tk, kdim)
    assert m % tm == 0 and n % tn == 0 and kdim % tk == 0
    nk = kdim // tk
    dims = (((0 if ta else 1,), (1 if tb else 0,)), ((), ()))
    has_add = add is not None

    def body(*refs):
        if has_add:
            a_ref, b_ref, add_ref, o_ref, acc_ref = refs
        else:
            a_ref, b_ref, o_ref, acc_ref = refs
        k = pl.program_id(2)

        @pl.when(k == 0)
        def _():
            acc_ref[...] = jnp.zeros_like(acc_ref)

        acc_ref[...] += lax.dot_general(a_ref[...].astype(MXU_DTYPE), b_ref[...].astype(MXU_DTYPE), dims,
                                        preferred_element_type=F32)

        @pl.when(k == nk - 1)
        def _():
            r = acc_ref[...]
            if has_add:
                r = r + add_ref[...].astype(F32)
            o_ref[...] = r.astype(o_ref.dtype)

    a_spec = pl.BlockSpec((tk, tm), lambda i, j, k: (k, i)) if ta else pl.BlockSpec((tm, tk), lambda i, j, k: (i, k))
    b_spec = pl.BlockSpec((tn, tk), lambda i, j, k: (j, k)) if tb else pl.BlockSpec((tk, tn), lambda i, j, k: (k, j))
    o_spec = pl.BlockSpec((tm, tn), lambda i, j, k: (i, j))
    in_specs = [a_spec, b_spec] + ([o_spec] if has_add else [])
    args = (a, b) + ((add,) if has_add else ())
    return pl.pallas_call(
        body, name=name, grid=(m // tm, n // tn, nk), in_specs=in_specs, out_specs=o_spec,
        out_shape=jax.ShapeDtypeStruct((m, n), out_dtype), scratch_shapes=[pltpu.VMEM((tm, tn), F32)],
        compiler_params=_cparams(("parallel", "parallel", "arbitrary")),
    )(*args)


def _ew(fn, rows, params, row_dtypes, n_acc, *, name, tr=256):
    nrows = rows[0].shape[0]
    tr = min(tr, nrows)
    assert nrows % tr == 0
    n_in, n_par, n_row = len(rows), len(params), len(row_dtypes)
    tile_avals = [jax.ShapeDtypeStruct((tr, r.shape[1]), F32) for r in rows]
    par_avals = [jax.ShapeDtypeStruct(q.shape, F32) for q in params]
    out_avals = jax.eval_shape(fn, *tile_avals, *par_avals)
    assert len(out_avals) == n_row + n_acc

    def body(*refs):
        ins = [r[...].astype(F32) for r in refs[:n_in + n_par]]
        outs = fn(*ins)
        row_refs = refs[n_in + n_par:n_in + n_par + n_row]
        acc_refs = refs[n_in + n_par + n_row:]
        for r, o in zip(row_refs, outs[:n_row]):
            r[...] = o.astype(r.dtype)
        if n_acc:
            @pl.when(pl.program_id(0) == 0)
            def _():
                for r in acc_refs:
                    r[...] = jnp.zeros_like(r)
            for r, o in zip(acc_refs, outs[n_row:]):
                r[...] += o

    in_specs = [pl.BlockSpec((tr, r.shape[1]), lambda i: (i, 0)) for r in rows]
    in_specs += [pl.BlockSpec(q.shape, lambda i: (0, 0)) for q in params]
    out_specs = [pl.BlockSpec((tr, o.shape[1]), lambda i: (i, 0)) for o in out_avals[:n_row]]
    out_specs += [pl.BlockSpec(o.shape, lambda i: (0, 0)) for o in out_avals[n_row:]]
    out_shape = [jax.ShapeDtypeStruct((nrows, o.shape[1]), dt) for o, dt in zip(out_avals[:n_row], row_dtypes)]
    out_shape += [jax.ShapeDtypeStruct(o.shape, F32) for o in out_avals[n_row:]]
    return pl.pallas_call(
        body, name=name, grid=(nrows // tr,), in_specs=in_specs, out_specs=out_specs, out_shape=out_shape,
        compiler_params=_cparams(("arbitrary",)),
    )(*rows, *params)


def _rms(xt, g):
    r = lax.rsqrt(jnp.mean(xt * xt, axis=-1, keepdims=True) + EPS)
    return xt * r * g


def _gelu_skip(ys, u, dskip):
    return jax.nn.gelu(ys + dskip * u)


def _glu_gate(g, t, gate, bglu):
    return g * jax.nn.sigmoid(t + bglu) * jax.nn.silu(gate)


def _ogate(o, gate):
    return o * jax.nn.silu(gate)


def _ple(xt, gt, pp):
    return xt + jax.nn.sigmoid(gt) * pp


def _s5_disc(lam_re, lam_im, log_dt, bt_re, bt_im):
    lr = jnp.minimum(lam_re, -1e-4)
    li = lam_im
    dt = jnp.exp(log_dt)
    mag = jnp.exp(lr * dt)
    a_re = mag * jnp.cos(li * dt)
    a_im = mag * jnp.sin(li * dt)
    den = lr * lr + li * li
    nr = a_re - 1.0
    f_re = (nr * lr + a_im * li) / den
    f_im = (a_im * lr - nr * li) / den
    bb_re = f_re[:, None, :] * bt_re - f_im[:, None, :] * bt_im
    bb_im = f_re[:, None, :] * bt_im + f_im[:, None, :] * bt_re
    return a_re, a_im, bb_re, bb_im


def _s5_params_fwd(lam_re, lam_im, log_dt, bt_re, bt_im):
    def body(lr_ref, li_ref, dt_ref, br_ref, bi_ref, ar_ref, ai_ref, bbr_ref, bbi_ref, pr_ref, pi_ref):
        a_re, a_im, bb_re, bb_im = _s5_disc(lr_ref[...], li_ref[...], dt_ref[...], br_ref[...], bi_ref[...])
        ar_ref[...] = a_re
        ai_ref[...] = a_im
        bbr_ref[...] = bb_re
        bbi_ref[...] = bb_im
        pr, pi = a_re, a_im
        for t in range(SUBLANES):
            pr_ref[t] = pr
            pi_ref[t] = pi
            pr, pi = pr * a_re - pi * a_im, pr * a_im + pi * a_re

    g, s = lam_re.shape
    sd = jax.ShapeDtypeStruct
    return pl.pallas_call(
        body, name="s5_params_fwd",
        out_shape=[sd((g, s), F32), sd((g, s), F32), sd(bt_re.shape, F32), sd(bt_re.shape, F32),
                   sd((SUBLANES, g, s), F32), sd((SUBLANES, g, s), F32)],
    )(lam_re, lam_im, log_dt, bt_re, bt_im)


def _s5_params_bwd(lam_re, lam_im, log_dt, bt_re, bt_im, d_ar, d_ai, d_bbr, d_bbi):
    def body(lr_ref, li_ref, dt_ref, br_ref, bi_ref, c0, c1, c2, c3, o0, o1, o2, o3, o4):
        _, vjp = jax.vjp(_s5_disc, lr_ref[...], li_ref[...], dt_ref[...], br_ref[...], bi_ref[...])
        grads = vjp((jnp.sum(c0[...], axis=0), jnp.sum(c1[...], axis=0), c2[...], c3[...]))
        for o, gval in zip((o0, o1, o2, o3, o4), grads):
            o[...] = gval

    sd = jax.ShapeDtypeStruct
    return pl.pallas_call(
        body, name="s5_params_bwd",
        out_shape=[sd(lam_re.shape, F32), sd(lam_im.shape, F32), sd(log_dt.shape, F32), sd(bt_re.shape, F32),
                   sd(bt_im.shape, F32)],
    )(lam_re, lam_im, log_dt, bt_re, bt_im, d_ar, d_ai, d_bbr, d_bbi)


def _block_diag(t):
    nb, g, r, c = t.shape
    eye = jnp.eye(g, dtype=t.dtype)
    return jnp.einsum("ngrc,gh->ngrhc", t, eye).reshape(nb, g * r, g * c)


def _block_diag_extract(m, r, c):
    nb = m.shape[0]
    g = GROUPS_PER_BLOCK
    eye = jnp.eye(g, dtype=m.dtype)[None, :, None, :, None]
    return jnp.sum(m.reshape(nb, g, r, g, c) * eye, axis=3)


def _per_sublane(acc):
    return acc.reshape(N_GROUP_BLOCKS, SUBLANES, GROUPS_PER_BLOCK, STATE).transpose(1, 0, 2, 3).reshape(
        SUBLANES, N_GROUPS, STATE)


def _scan_tables(pr, pi):
    row = jnp.arange(SUBLANES)[:, None, None]

    def tables(im_sign, fwd):
        kinds = []
        for k in (1, 2, 4):
            mask = (row >= k) if fwd else (row < SUBLANES - k)
            kinds.append(jnp.where(mask, pr[k - 1][None], 0.0))
            kinds.append(jnp.where(mask, im_sign * pi[k - 1][None], 0.0))
        if fwd:
            kinds += [pr, im_sign * pi]
        else:
            kinds += [pr[::-1], im_sign * pi[::-1]]
        return jnp.stack(kinds, axis=0).transpose(2, 0, 1, 3)

    return tables(1.0, True), tables(-1.0, False)


def _scan_rows(buf_ref, tab_ref, carry_r, carry_i, n_tiles, reverse):
    w = SSM_LANES
    assert n_tiles % SCAN_TILES_PER_STEP == 0

    def local_scan(jj):
        rows = pl.ds(pl.multiple_of(jj * SUBLANES, SUBLANES), SUBLANES)
        zr = buf_ref[rows, 0:w]
        zi = buf_ref[rows, w:2 * w]
        for n, k in enumerate((1, 2, 4)):
            ar = tab_ref[2 * n]
            ai = tab_ref[2 * n + 1]
            sh = (SUBLANES - k) if reverse else k
            rr = pltpu.roll(zr, sh, 0)
            ri = pltpu.roll(zi, sh, 0)
            zr, zi = zr + ar * rr - ai * ri, zi + ar * ri + ai * rr
        return rows, zr, zi

    def step(j, carry):
        first = j * SCAN_TILES_PER_STEP
        tiles = [local_scan((n_tiles - 1 - first - d) if reverse else (first + d)) for d in range(SCAN_TILES_PER_STEP)]
        cr, ci = carry
        pr = tab_ref[6]
        pi = tab_ref[7]
        for rows, zr, zi in tiles:
            zr, zi = zr + pr * cr - pi * ci, zi + pr * ci + pi * cr
            buf_ref[rows, 0:w] = zr
            buf_ref[rows, w:2 * w] = zi
            cr, ci = (zr[0:1], zi[0:1]) if reverse else (zr[SUBLANES - 1:SUBLANES], zi[SUBLANES - 1:SUBLANES])
        return cr, ci

    return lax.fori_loop(0, n_tiles // SCAN_TILES_PER_STEP, step, (carry_r, carry_i))


def _s5_fwd(u, bbd, cbd, tab_f, *, tt=512):
    seq = u.shape[0]
    tt = min(tt, seq)
    nt = seq // tt
    w = SSM_LANES

    def body(u_ref, b_ref, c_ref, tab_ref, y_ref, hs_ref, buf_ref, h_ref):
        it = pl.program_id(1)

        @pl.when(it == 0)
        def _():
            h_ref[...] = jnp.zeros_like(h_ref)

        hs_ref[...] = h_ref[...]
        buf_ref[...] = jnp.dot(u_ref[...].astype(MXU_DTYPE), b_ref[...], preferred_element_type=F32)
        hr, hi = _scan_rows(buf_ref, tab_ref, h_ref[:, 0:w], h_ref[:, w:2 * w], tt // SUBLANES, False)
        h_ref[:, 0:w] = hr
        h_ref[:, w:2 * w] = hi
        y_ref[...] = jnp.dot(buf_ref[...].astype(MXU_DTYPE), c_ref[...], preferred_element_type=F32)

    return pl.pallas_call(
        body, name="s5_fwd", grid=(N_GROUP_BLOCKS, nt),
        in_specs=[pl.BlockSpec((tt, 128), lambda g, t: (t, g)),
                  pl.BlockSpec((None, 128, 2 * w), lambda g, t: (g, 0, 0)),
                  pl.BlockSpec((None, 2 * w, 128), lambda g, t: (g, 0, 0)),
                  pl.BlockSpec((None, 8, SUBLANES, w), lambda g, t: (g, 0, 0, 0))],
        out_specs=[pl.BlockSpec((tt, 128), lambda g, t: (t, g)),
                   pl.BlockSpec((None, None, 1, 2 * w), lambda g, t: (g, t, 0, 0))],
        out_shape=[jax.ShapeDtypeStruct((seq, D_MODEL), F32),
                   jax.ShapeDtypeStruct((N_GROUP_BLOCKS, nt, 1, 2 * w), F32)],
        scratch_shapes=[pltpu.VMEM((tt, 2 * w), F32), pltpu.VMEM((1, 2 * w), F32)],
        compiler_params=_cparams(("parallel", "arbitrary")),
    )(u, bbd, cbd, tab_f)


def _s5_bwd(u, dy, du_add, hs, bbd, cbd_t, tab_f, tab_b, *, tt=512):
    seq = u.shape[0]
    tt = min(tt, seq)
    nt = seq // tt
    w = SSM_LANES
    n_tiles = tt // SUBLANES

    def body(u_ref, dy_ref, dua_ref, hs_ref, b_ref, ct_ref, tabf_ref, tabb_ref,
             du_ref, db_ref, dc_ref, dar_ref, dai_ref, s_ref, l_ref, lam_ref):
        it = pl.program_id(1)

        @pl.when(it == 0)
        def _():
            lam_ref[...] = jnp.zeros_like(lam_ref)
            db_ref[...] = jnp.zeros_like(db_ref)
            dc_ref[...] = jnp.zeros_like(dc_ref)
            dar_ref[...] = jnp.zeros_like(dar_ref)
            dai_ref[...] = jnp.zeros_like(dai_ref)

        ub = u_ref[...].astype(MXU_DTYPE)
        dyb = dy_ref[...].astype(MXU_DTYPE)
        s_ref[...] = jnp.dot(ub, b_ref[...], preferred_element_type=F32)
        h_in_r = hs_ref[:, 0:w]
        h_in_i = hs_ref[:, w:2 * w]
        _scan_rows(s_ref, tabf_ref, h_in_r, h_in_i, n_tiles, False)
        dc_ref[...] += lax.dot_general(s_ref[...].astype(MXU_DTYPE), dyb, (((0,), (0,)), ((), ())),
                                       preferred_element_type=F32)
        l_ref[...] = jnp.dot(dyb, ct_ref[...], preferred_element_type=F32)
        lr, li = _scan_rows(l_ref, tabb_ref, lam_ref[:, 0:w], lam_ref[:, w:2 * w], n_tiles, True)
        lam_ref[:, 0:w] = lr
        lam_ref[:, w:2 * w] = li
        lb = l_ref[...].astype(MXU_DTYPE)
        db_ref[...] += lax.dot_general(ub, lb, (((0,), (0,)), ((), ())), preferred_element_type=F32)
        du = lax.dot_general(lb, b_ref[...], (((1,), (1,)), ((), ())), preferred_element_type=F32)
        du_ref[...] = (du + dua_ref[...].astype(F32)).astype(du_ref.dtype)

        def tile(j, carry):
            pr, pi, accr, acci = carry
            rows = pl.ds(pl.multiple_of(j * SUBLANES, SUBLANES), SUBLANES)
            sr = s_ref[rows, 0:w]
            si = s_ref[rows, w:2 * w]
            first = lax.broadcasted_iota(jnp.int32, (SUBLANES, w), 0) == 0
            sr_prev = jnp.where(first, pr, pltpu.roll(sr, 1, 0))
            si_prev = jnp.where(first, pi, pltpu.roll(si, 1, 0))
            gr = l_ref[rows, 0:w]
            gi = l_ref[rows, w:2 * w]
            accr = accr + gr * sr_prev + gi * si_prev
            acci = acci + gi * sr_prev - gr * si_prev
            return sr[SUBLANES - 1:SUBLANES], si[SUBLANES - 1:SUBLANES], accr, acci

        zero = jnp.zeros((SUBLANES, w), F32)
        _, _, accr, acci = lax.fori_loop(0, n_tiles, tile, (h_in_r, h_in_i, zero, zero))
        dar_ref[...] += accr
        dai_ref[...] += acci

    rev = lambda g, t: (nt - 1 - t, g)
    return pl.pallas_call(
        body, name="s5_bwd", grid=(N_GROUP_BLOCKS, nt),
        in_specs=[pl.BlockSpec((tt, 128), rev), pl.BlockSpec((tt, 128), rev), pl.BlockSpec((tt, 128), rev),
                  pl.BlockSpec((None, None, 1, 2 * w), lambda g, t: (g, nt - 1 - t, 0, 0)),
                  pl.BlockSpec((None, 128, 2 * w), lambda g, t: (g, 0, 0)),
                  pl.BlockSpec((None, 128, 2 * w), lambda g, t: (g, 0, 0)),
                  pl.BlockSpec((None, 8, SUBLANES, w), lambda g, t: (g, 0, 0, 0)),
                  pl.BlockSpec((None, 8, SUBLANES, w), lambda g, t: (g, 0, 0, 0))],
        out_specs=[pl.BlockSpec((tt, 128), rev),
                   pl.BlockSpec((None, 128, 2 * w), lambda g, t: (g, 0, 0)),
                   pl.BlockSpec((None, 2 * w, 128), lambda g, t: (g, 0, 0)),
                   pl.BlockSpec((None, SUBLANES, w), lambda g, t: (g, 0, 0)),
                   pl.BlockSpec((None, SUBLANES, w), lambda g, t: (g, 0, 0))],
        out_shape=[jax.ShapeDtypeStruct((seq, D_MODEL), MXU_DTYPE),
                   jax.ShapeDtypeStruct((N_GROUP_BLOCKS, 128, 2 * w), F32),
                   jax.ShapeDtypeStruct((N_GROUP_BLOCKS, 2 * w, 128), F32),
                   jax.ShapeDtypeStruct((N_GROUP_BLOCKS, SUBLANES, w), F32),
                   jax.ShapeDtypeStruct((N_GROUP_BLOCKS, SUBLANES, w), F32)],
        scratch_shapes=[pltpu.VMEM((tt, 2 * w), F32), pltpu.VMEM((tt, 2 * w), F32), pltpu.VMEM((1, 2 * w), F32)],
        compiler_params=_cparams(("parallel", "arbitrary")),
    )(u, dy, du_add, hs, bbd, cbd_t, tab_f, tab_b)


LOG2E = 1.4426950408889634
LN2 = 0.6931471805599453
SOFTPLUS2_LINEAR = 28.0


def _softplus2(z):
    u = 1.0 + jnp.exp2(z)
    return jnp.where(z > SOFTPLUS2_LINEAR, z, jnp.log2(u)), u


def _tri(n, keep, value=1.0):
    row = lax.broadcasted_iota(jnp.int32, (n, n), 0)
    col = lax.broadcasted_iota(jnp.int32, (n, n), 1)
    return jnp.where(keep(row, col), value, 0.0).astype(MXU_DTYPE)


def _suffix_sums(x, tri):
    c = tri.shape[0]
    outs, carry = [], None
    for i in reversed(range(x.shape[1] // c)):
        part = jnp.dot(x[:, i * c:(i + 1) * c].astype(MXU_DTYPE), tri, preferred_element_type=F32)
        if carry is not None:
            part = part + carry
        carry = part[:, 0:1]
        outs.append(part)
    return jnp.concatenate(outs[::-1], axis=1) if len(outs) > 1 else outs[0]


def _prefix_sums(x, tri, carry):
    c = tri.shape[0]
    outs = []
    for i in range(x.shape[1] // c):
        part = jnp.dot(x[:, i * c:(i + 1) * c].astype(MXU_DTYPE), tri, preferred_element_type=F32) + carry
        carry = part[:, c - 1:c]
        outs.append(part)
    return jnp.concatenate(outs, axis=1) if len(outs) > 1 else outs[0]


def _head_masks():
    lane = lax.broadcasted_iota(jnp.int32, (1, HEADS_PER_BLOCK * HEAD_DIM), 1)
    return [(lane >= h * HEAD_DIM) & (lane < (h + 1) * HEAD_DIM) for h in range(HEADS_PER_BLOCK)]


def _sba_fwd(q, k, v, *, tb=FWD_QUERY_BLOCK, key_blocks=FWD_KEY_BLOCKS_PER_STEP):
    seq = q.shape[0]
    tb = min(tb, seq)
    nq = seq // tb
    cw = HEADS_PER_BLOCK * HEAD_DIM
    scale = HEAD_DIM ** -0.5

    def body(q_ref, k_ref, v_ref, o_ref, t_ref, acc_ref, r_ref):
        qi = pl.program_id(1)
        masks = _head_masks()
        qf = q_ref[...].astype(F32) * (scale * LOG2E)
        q2 = jnp.concatenate([jnp.where(m, qf, 0.0) for m in masks], axis=0).astype(MXU_DTYPE)
        causal = lax.broadcasted_iota(jnp.int32, (tb, tb), 1) < lax.broadcasted_iota(jnp.int32, (tb, tb), 0)
        causal2 = jnp.concatenate([causal] * HEADS_PER_BLOCK, axis=0)
        neg_upper = _tri(min(tb, CUMSUM_CHUNK), lambda j, s: j >= s, -1.0)
        acc_ref[...] = jnp.zeros_like(acc_ref)
        r_ref[...] = jnp.zeros_like(r_ref)

        def block(start, width, diag):
            ks = pl.ds(pl.multiple_of(start, tb), width)
            kb = k_ref[ks, :].astype(MXU_DTYPE)
            vb = v_ref[ks, :]
            z = lax.dot_general(q2, kb, (((1,), (1,)), ((), ())), preferred_element_type=F32)
            sp, _ = _softplus2(z)
            if diag:
                sp = jnp.where(causal2, sp, 0.0)
            incl = _suffix_sums(sp, neg_upper)
            a = z + incl + r_ref[...]
            if diag:
                a = jnp.where(causal2, a, -1e30)
            wgt = jnp.exp2(a).astype(MXU_DTYPE)
            wcat = jnp.concatenate([wgt[h * tb:(h + 1) * tb] for h in range(HEADS_PER_BLOCK)], axis=1)
            vcat = jnp.concatenate([jnp.where(m, vb, jnp.zeros_like(vb)) for m in masks], axis=0).astype(MXU_DTYPE)
            acc_ref[...] += jnp.dot(wcat, vcat, preferred_element_type=F32)
            r_ref[...] += incl[:, 0:1]

        block(qi * tb, tb, True)
        n_wide = qi // key_blocks

        def single(i, c):
            block((qi - 1 - i) * tb, tb, False)
            return c

        lax.fori_loop(0, qi - n_wide * key_blocks, single, 0)

        def step(i, c):
            block((n_wide - 1 - i) * (key_blocks * tb), key_blocks * tb, False)
            return c

        lax.fori_loop(0, n_wide, step, 0)
        o_ref[...] = acc_ref[...].astype(o_ref.dtype)
        for h in range(HEADS_PER_BLOCK):
            t_ref[:, h:h + 1] = r_ref[h * tb:(h + 1) * tb]

    n_hp = D_MODEL // cw
    return pl.pallas_call(
        body, name="sba_fwd", grid=(n_hp, nq),
        in_specs=[pl.BlockSpec((tb, cw), lambda h, i: (i, h)),
                  pl.BlockSpec((seq, cw), lambda h, i: (0, h)),
                  pl.BlockSpec((seq, cw), lambda h, i: (0, h))],
        out_specs=[pl.BlockSpec((tb, cw), lambda h, i: (i, h)),
                   pl.BlockSpec((None, tb, HEADS_PER_BLOCK), lambda h, i: (h, i, 0))],
        out_shape=[jax.ShapeDtypeStruct((seq, D_MODEL), MXU_DTYPE),
                   jax.ShapeDtypeStruct((n_hp, seq, HEADS_PER_BLOCK), F32)],
        scratch_shapes=[pltpu.VMEM((tb, cw), F32), pltpu.VMEM((HEADS_PER_BLOCK * tb, 1), F32)],
        compiler_params=_cparams(("parallel", "arbitrary")),
    )(q, k, v)


def _sba_bwd(q, k, v, do, tot, *, tb=BWD_QUERY_BLOCK):
    seq = q.shape[0]
    tb = min(tb, seq)
    nq = seq // tb
    cw = HEADS_PER_BLOCK * HEAD_DIM
    scale = HEAD_DIM ** -0.5

    def body(q_ref, k_ref, v_ref, do_ref, t_ref, dq_ref, dk_ref, dv_ref, acc_ref, lp_ref, dp_ref):
        qi = pl.program_id(1)

        @pl.when(qi == 0)
        def _():
            dk_ref[...] = jnp.zeros_like(dk_ref)
            dv_ref[...] = jnp.zeros_like(dv_ref)

        masks = _head_masks()
        qf = q_ref[...].astype(F32) * (scale * LOG2E)
        q2 = jnp.concatenate([jnp.where(m, qf, 0.0) for m in masks], axis=0).astype(MXU_DTYPE)
        dof = do_ref[...].astype(F32)
        do2 = jnp.concatenate([jnp.where(m, dof, 0.0) for m in masks], axis=0).astype(MXU_DTYPE)
        tot2 = jnp.concatenate([t_ref[:, h:h + 1] for h in range(HEADS_PER_BLOCK)], axis=0)
        causal = lax.broadcasted_iota(jnp.int32, (tb, tb), 1) < lax.broadcasted_iota(jnp.int32, (tb, tb), 0)
        causal2 = jnp.concatenate([causal] * HEADS_PER_BLOCK, axis=0)
        neg_upper = _tri(min(tb, CUMSUM_CHUNK), lambda j, s: j >= s, -1.0)
        lower = _tri(min(tb, CUMSUM_CHUNK), lambda j, s: j <= s)
        acc_ref[...] = jnp.zeros_like(acc_ref)
        lp_ref[...] = jnp.zeros_like(lp_ref)
        dp_ref[...] = jnp.zeros_like(dp_ref)

        def block(start, width, diag):
            ks = pl.ds(pl.multiple_of(start, tb), width)
            kb = k_ref[ks, :]
            vb = v_ref[ks, :].astype(MXU_DTYPE)
            z = lax.dot_general(q2, kb.astype(MXU_DTYPE), (((1,), (1,)), ((), ())), preferred_element_type=F32)
            sp, u = _softplus2(z)
            if diag:
                sp = jnp.where(causal2, sp, 0.0)
            incl = _suffix_sums(sp, neg_upper)
            lp_next = lp_ref[...] + incl[:, 0:1]
            lp_ref[...] = lp_next
            a = z + incl + (tot2 - lp_next)
            if diag:
                a = jnp.where(causal2, a, -1e30)
            wgt = jnp.exp2(a)
            dw = lax.dot_general(do2, vb, (((1,), (1,)), ((), ())), preferred_element_type=F32)
            da = dw * wgt
            pre = _prefix_sums(da, lower, dp_ref[...])
            dp_ref[...] = pre[:, width - 1:width]
            sig = 1.0 - pl.reciprocal(u, approx=True)
            dz = da - pre * sig
            if diag:
                dz = jnp.where(causal2, dz, 0.0)
            dzb = dz.astype(MXU_DTYPE)
            dzcat = jnp.concatenate([dzb[h * tb:(h + 1) * tb] for h in range(HEADS_PER_BLOCK)], axis=1)
            kcat = jnp.concatenate([jnp.where(m, kb, jnp.zeros_like(kb)) for m in masks], axis=0).astype(MXU_DTYPE)
            acc_ref[...] += jnp.dot(dzcat, kcat, preferred_element_type=F32)
            dk_ref[ks, :] += lax.dot_general(dzb, q2, (((0,), (0,)), ((), ())), preferred_element_type=F32) * LN2
            dv_ref[ks, :] += lax.dot_general(wgt.astype(MXU_DTYPE), do2, (((0,), (0,)), ((), ())),
                                             preferred_element_type=F32)

        n_wide = qi // KEY_BLOCKS_PER_STEP

        def step(i, c):
            block(i * (KEY_BLOCKS_PER_STEP * tb), KEY_BLOCKS_PER_STEP * tb, False)
            return c

        lax.fori_loop(0, n_wide, step, 0)

        def single(i, c):
            block((n_wide * KEY_BLOCKS_PER_STEP + i) * tb, tb, False)
            return c

        lax.fori_loop(0, qi - n_wide * KEY_BLOCKS_PER_STEP, single, 0)
        block(qi * tb, tb, True)
        dq_ref[...] = acc_ref[...] * scale

    n_hp = D_MODEL // cw
    qspec = pl.BlockSpec((tb, cw), lambda h, i: (i, h))
    full = pl.BlockSpec((seq, cw), lambda h, i: (0, h))
    return pl.pallas_call(
        body, name="sba_bwd", grid=(n_hp, nq),
        in_specs=[qspec, full, full, qspec, pl.BlockSpec((None, tb, HEADS_PER_BLOCK), lambda h, i: (h, i, 0))],
        out_specs=[qspec, full, full],
        out_shape=[jax.ShapeDtypeStruct((seq, D_MODEL), F32)] * 3,
        scratch_shapes=[pltpu.VMEM((tb, cw), F32), pltpu.VMEM((HEADS_PER_BLOCK * tb, 1), F32),
                        pltpu.VMEM((HEADS_PER_BLOCK * tb, 1), F32)],
        compiler_params=_cparams(("parallel", "arbitrary")),
    )(q, k, v, do, tot)


def _my_place():
    return lax.axis_index("x"), lax.axis_index("y"), lax.axis_index("c")


def _other_chips(x, y):
    return [(1 - x, y), (x, 1 - y), (1 - x, 1 - y)]


def _all_gather_chips(slabs, *, name):
    n = len(slabs)

    def body(*refs):
        ins, outs = refs[:n], refs[n:2 * n]
        send_sems, recv_sems, loc_sems = refs[2 * n:]
        x, y, c = _my_place()
        me = 2 * x + y
        local = [pltpu.make_async_copy(ins[i], outs[i].at[me], loc_sems.at[i]) for i in range(n)]
        for cp in local:
            cp.start()
        remote = []
        for j, (px, py) in enumerate(_other_chips(x, y)):
            for i in range(n):
                remote.append(pltpu.make_async_remote_copy(
                    src_ref=ins[i], dst_ref=outs[i].at[me], send_sem=send_sems.at[j * n + i],
                    recv_sem=recv_sems.at[j * n + i], device_id=(px, py, c), device_id_type=MESH))
        for cp in remote:
            cp.start()
        for cp in remote:
            cp.wait_recv()
        for cp in remote:
            cp.wait_send()
        for cp in local:
            cp.wait()

    return pl.pallas_call(
        body, name=name, in_specs=[ANY] * n, out_specs=[ANY] * n,
        out_shape=[jax.ShapeDtypeStruct((4,) + s.shape, s.dtype) for s in slabs],
        scratch_shapes=[pltpu.SemaphoreType.DMA((3 * n,)), pltpu.SemaphoreType.DMA((3 * n,)),
                        pltpu.SemaphoreType.DMA((n,))],
    )(*slabs)


def _all_gather_chips_two_level(slabs, *, name):
    n = len(slabs)
    halves = [s.shape[0] // 2 for s in slabs]

    def body(*refs):
        ins, outs = refs[:n], refs[n:2 * n]
        ici_send, ici_recv, d2d_send, d2d_recv = refs[2 * n:]
        x, y, c = _my_place()
        chips = _other_chips(x, y)
        mine = [pl.ds(c * h, h) for h in halves]
        theirs = [pl.ds((1 - c) * h, h) for h in halves]

        def over_ici(j, i, chip):
            px, py = chips[j]
            slot = 2 * chip[0] + chip[1]
            return pltpu.make_async_remote_copy(
                src_ref=ins[i].at[mine[i]], dst_ref=outs[i].at[slot, mine[i]], send_sem=ici_send.at[j * n + i],
                recv_sem=ici_recv.at[j * n + i], device_id=(px, py, c), device_id_type=MESH)

        def over_d2d(j, i, rows):
            px, py = chips[j]
            block = outs[i].at[2 * px + py, rows]
            return pltpu.make_async_remote_copy(
                src_ref=block, dst_ref=block, send_sem=d2d_send.at[j * n + i], recv_sem=d2d_recv.at[j * n + i],
                device_id=(x, y, 1 - c), device_id_type=MESH)

        sent = [over_ici(j, i, (x, y)) for j in range(3) for i in range(n)]
        for cp in sent:
            cp.start()
        passed = []
        for j in range(3):
            for i in range(n):
                over_ici(j, i, chips[j]).wait_recv()
                cp = over_d2d(j, i, mine[i])
                cp.start()
                passed.append(cp)
        for j in range(3):
            for i in range(n):
                over_d2d(j, i, theirs[i]).wait_recv()
        for cp in sent + passed:
            cp.wait_send()

    outs = pl.pallas_call(
        body, name=name, in_specs=[ANY] * n, out_specs=[ANY] * n,
        out_shape=[jax.ShapeDtypeStruct((4,) + s.shape, s.dtype) for s in slabs],
        scratch_shapes=[pltpu.SemaphoreType.DMA((3 * n,)), pltpu.SemaphoreType.DMA((3 * n,)),
                        pltpu.SemaphoreType.DMA((3 * n,)), pltpu.SemaphoreType.DMA((3 * n,))],
    )(*slabs)
    me = 2 * lax.axis_index("x") + lax.axis_index("y")
    return [[jnp.where(me == j, s, o[j]) for j in range(4)] for o, s in zip(outs, slabs)]


def _sibling_exchange(send, *, name):
    def body(s_ref, o_ref, send_sem, recv_sem):
        x, y, c = _my_place()
        cp = pltpu.make_async_remote_copy(src_ref=s_ref, dst_ref=o_ref, send_sem=send_sem, recv_sem=recv_sem,
                                          device_id=(x, y, 1 - c), device_id_type=MESH)
        cp.start()
        cp.wait_recv()
        cp.wait_send()

    return pl.pallas_call(
        body, name=name, in_specs=[ANY], out_specs=ANY,
        out_shape=jax.ShapeDtypeStruct(send.shape, send.dtype),
        scratch_shapes=[pltpu.SemaphoreType.DMA, pltpu.SemaphoreType.DMA],
    )(send)


def _all_to_all_chips(parts, *, name):
    def body(p_ref, o_ref, send_sems, recv_sems):
        x, y, c = _my_place()
        me = 2 * x + y
        remote = []
        for j, (px, py) in enumerate(_other_chips(x, y)):
            remote.append(pltpu.make_async_remote_copy(
                src_ref=p_ref.at[2 * px + py], dst_ref=o_ref.at[me], send_sem=send_sems.at[j],
                recv_sem=recv_sems.at[j], device_id=(px, py, c), device_id_type=MESH))
        for cp in remote:
            cp.start()
        for cp in remote:
            cp.wait_recv()
        for cp in remote:
            cp.wait_send()

    out = pl.pallas_call(
        body, name=name, in_specs=[ANY], out_specs=ANY,
        out_shape=jax.ShapeDtypeStruct(parts.shape, parts.dtype),
        scratch_shapes=[pltpu.SemaphoreType.DMA((3,)), pltpu.SemaphoreType.DMA((3,))],
    )(parts)
    me = 2 * lax.axis_index("x") + lax.axis_index("y")
    return [jnp.where(me == j, parts[j], out[j]) for j in range(4)]


_SHARDED = [("a_norm_pre", 1), ("a_norm_post", 1), ("a_d_skip", 1), ("a_b_glu", 1), ("a_w_in", 2), ("a_w_glu", 1),
            ("a_w_out", 1), ("w_kv", 1), ("b_w_in", 2), ("b_w_out", 1), ("ple_w_proj", 2), ("ple_w_gate", 1)]
_REPLICATED = ["a_lam_re", "a_lam_im", "a_log_dt", "a_b_re", "a_b_im", "a_c_re", "a_c_im", "kv_norm", "b_norm_pre",
               "b_norm_post"]
_WEIGHT_ORDER = ["a_norm_pre", "a_norm_post", "a_w_in", "a_lam_re", "a_lam_im", "a_log_dt", "a_b_re", "a_b_im",
                 "a_c_re", "a_c_im", "a_d_skip", "a_w_glu", "a_b_glu", "a_w_out", "kv_norm", "w_kv", "b_norm_pre",
                 "b_norm_post", "b_w_in", "b_w_out", "ple_w_proj", "ple_w_gate"]


def _round_up(n, m):
    return (n + m - 1) // m * m


def _pack(pieces, rows):
    flat = jnp.concatenate([p.reshape(-1).astype(F32) for p in pieces])
    flat = jnp.pad(flat, (0, rows * SLAB_W - flat.shape[0]))
    return flat.reshape(rows, SLAB_W)


def _unpack(slab, shapes):
    flat = slab.reshape(-1)
    out, off = [], 0
    for s in shapes:
        n = math.prod(s)
        out.append(flat[off:off + n].reshape(s))
        off += n
    return out


def _pick_tile(n, cap):
    return max(t for t in range(SUBLANES, cap + 1, SUBLANES) if n % t == 0)


def _size_rows(shapes, mult):
    return _round_up(_round_up(sum(math.prod(s) for s in shapes), SLAB_W) // SLAB_W, mult)


def _adamw_tile(w, g, m, v):
    m2 = ADAM_B1 * m + (1.0 - ADAM_B1) * g
    v2 = ADAM_B2 * v + (1.0 - ADAM_B2) * (g * g)
    m_hat = m2 / (1.0 - ADAM_B1 ** ADAM_STEP)
    v_hat = v2 / (1.0 - ADAM_B2 ** ADAM_STEP)
    delta = -ADAM_LR * (m_hat / (jnp.sqrt(v_hat) + ADAM_EPS) + ADAM_WD * w)
    return delta, m2, v2


def _forward_backward(x0, p0, p1, tgt, W):
    seq = x0.shape[0]
    sw = D_MODEL
    g = {}

    h1, = _ew(lambda xt, gn: (_rms(xt, gn),), [x0], [W["a_norm_pre"]], [MXU_DTYPE], 0, name="a_pre_norm")
    w_u, w_ga = W["a_w_in"][:, :sw], W["a_w_in"][:, sw:]
    u = _mm(h1, w_u, name="a_in_u")
    gate = _mm(h1, w_ga, out_dtype=MXU_DTYPE, name="a_in_gate")
    a_re, a_im, bbt_re, bbt_im, pw_re, pw_im = _s5_params_fwd(W["lam_re"], W["lam_im"], W["log_dt"], W["bt_re"],
                                                               W["bt_im"])
    nb, gpb = N_GROUP_BLOCKS, GROUPS_PER_BLOCK
    bbd = jnp.concatenate([_block_diag(bbt_re.reshape(nb, gpb, GROUP_SIZE, STATE)),
                           _block_diag(bbt_im.reshape(nb, gpb, GROUP_SIZE, STATE))], axis=2).astype(MXU_DTYPE)
    crt = W["c_re"].transpose(0, 2, 1).reshape(nb, gpb, STATE, GROUP_SIZE)
    cit = W["c_im"].transpose(0, 2, 1).reshape(nb, gpb, STATE, GROUP_SIZE)
    cbd = jnp.concatenate([_block_diag(crt), -_block_diag(cit)], axis=1).astype(MXU_DTYPE)
    cbd_t = cbd.transpose(0, 2, 1)
    tab_f, tab_b = _scan_tables(pw_re.reshape(SUBLANES, nb, SSM_LANES), pw_im.reshape(SUBLANES, nb, SSM_LANES))
    ys, hs = _s5_fwd(u, bbd, cbd, tab_f)
    gl, = _ew(lambda a, b, d: (_gelu_skip(a, b, d),), [ys, u], [W["a_d_skip"]], [MXU_DTYPE], 0, name="a_gelu")
    t = _mm(gl, W["a_w_glu"], out_dtype=MXU_DTYPE, name="a_glu")
    y3, = _ew(lambda a, b, c, d: (_glu_gate(a, b, c, d),), [gl, t, gate], [W["a_b_glu"]], [MXU_DTYPE], 0, name="a_gate")
    y4 = _mm(y3, W["a_w_out"], out_dtype=MXU_DTYPE, name="a_out")
    x1, = _ew(lambda a, b, gn: (a + _rms(b, gn),), [x0, y4], [W["a_norm_post"]], [F32], 0, name="a_post_norm")
    gt0 = _mm(x1, W["ple_w_gate"][0], out_dtype=MXU_DTYPE, name="ple0_gate")
    pp0 = _mm(p0, W["ple_w_proj"][0], out_dtype=MXU_DTYPE, name="ple0_proj")
    x2, hk, h2 = _ew(lambda a, b, c, g1, g2: (_ple(a, b, c), _rms(_ple(a, b, c), g1), _rms(_ple(a, b, c), g2)),
                     [x1, gt0, pp0], [W["kv_norm"], W["b_norm_pre"]], [F32, MXU_DTYPE, MXU_DTYPE], 0, name="ple0_mix")

    w_k, w_v = W["w_kv"][:, :sw], W["w_kv"][:, sw:]
    k = _mm(hk, w_k, out_dtype=MXU_DTYPE, name="kv_k")
    v = _mm(hk, w_v, out_dtype=MXU_DTYPE, name="kv_v")
    w_q, w_gb = W["b_w_in"][:, :sw], W["b_w_in"][:, sw:]
    q = _mm(h2, w_q, out_dtype=MXU_DTYPE, name="b_in_q")
    gate2 = _mm(h2, w_gb, out_dtype=MXU_DTYPE, name="b_in_gate")
    o, tot = _sba_fwd(q, k, v)
    y5in, = _ew(lambda a, b: (_ogate(a, b),), [o, gate2], [], [MXU_DTYPE], 0, name="b_gate")
    y5 = _mm(y5in, W["b_w_out"], out_dtype=MXU_DTYPE, name="b_out")
    x3, = _ew(lambda a, b, gn: (a + _rms(b, gn),), [x2, y5], [W["b_norm_post"]], [F32], 0, name="b_post_norm")
    gt1 = _mm(x3, W["ple_w_gate"][1], out_dtype=MXU_DTYPE, name="ple1_gate")
    pp1 = _mm(p1, W["ple_w_proj"][1], out_dtype=MXU_DTYPE, name="ple1_proj")

    def loss_fn(xt, gt, pp, tg):
        s = jax.nn.sigmoid(gt)
        d = (xt + s * pp - tg) * (1.0 / D_MODEL)
        lsum = jnp.sum(d * d, axis=(0, 1), keepdims=True) * (0.5 * D_MODEL)
        return d, d * pp * s * (1.0 - s), d * s, lsum

    dx3a, dgt1, dpp1, loss = _ew(loss_fn, [x3, gt1, pp1, tgt], [], [F32, MXU_DTYPE, MXU_DTYPE], 1, name="loss_head")
    g_gate1 = _mm(x3, dgt1, ta=True, name="ple1_gate_dw")
    g_proj1 = _mm(p1, dpp1, ta=True, name="ple1_proj_dw")
    dx3 = _mm(dgt1, W["ple_w_gate"][1], tb=True, add=dx3a, name="ple1_gate_dx")

    def post_norm_bwd(ct, yt, gn):
        _, vjp = jax.vjp(_rms, yt, gn)
        dy, dg = vjp(ct)
        return dy, dg

    dy5, g["b_norm_post"] = _ew(post_norm_bwd, [dx3, y5], [W["b_norm_post"]], [MXU_DTYPE], 1, name="b_post_norm_bwd")
    g["b_w_out"] = _mm(y5in, dy5, ta=True, name="b_out_dw")
    dy5in = _mm(dy5, W["b_w_out"], tb=True, out_dtype=MXU_DTYPE, name="b_out_dx")

    def ogate_bwd(ct, ot, gt):
        _, vjp = jax.vjp(_ogate, ot, gt)
        return vjp(ct)

    do, dgate2 = _ew(ogate_bwd, [dy5in, o, gate2], [], [MXU_DTYPE, MXU_DTYPE], 0, name="b_gate_bwd")
    dq, dk, dv = _sba_bwd(q, k, v, do, tot)
    g["b_w_in"] = jnp.concatenate([_mm(h2, dq, ta=True, name="b_in_q_dw"),
                                   _mm(h2, dgate2, ta=True, name="b_in_gate_dw")], axis=1)
    dh2 = _mm(dgate2, w_gb, tb=True, add=_mm(dq, w_q, tb=True, name="b_in_q_dx"), out_dtype=MXU_DTYPE,
              name="b_in_gate_dx")
    g["w_kv"] = jnp.concatenate([_mm(hk, dk, ta=True, name="kv_k_dw"), _mm(hk, dv, ta=True, name="kv_v_dw")], axis=1)
    dhk = _mm(dv, w_v, tb=True, add=_mm(dk, w_k, tb=True, name="kv_k_dx"), out_dtype=MXU_DTYPE, name="kv_v_dx")

    def mix_bwd(ct, c2, ck, xt, gt, pp, g1, g2):
        x2v, vjp_ple = jax.vjp(_ple, xt, gt, pp)
        _, vjp_k = jax.vjp(_rms, x2v, g1)
        _, vjp_b = jax.vjp(_rms, x2v, g2)
        dxk, dg1 = vjp_k(ck)
        dxb, dg2 = vjp_b(c2)
        dx2 = ct + dxk + dxb
        dx1, dgt, dpp = vjp_ple(dx2)
        return dx1, dgt, dpp, dg1, dg2

    dx1a, dgt0, dpp0, g["kv_norm"], g["b_norm_pre"] = _ew(
        mix_bwd, [dx3, dh2, dhk, x1, gt0, pp0], [W["kv_norm"], W["b_norm_pre"]], [F32, MXU_DTYPE, MXU_DTYPE], 2,
        name="ple0_mix_bwd")
    g_gate0 = _mm(x1, dgt0, ta=True, name="ple0_gate_dw")
    g_proj0 = _mm(p0, dpp0, ta=True, name="ple0_proj_dw")
    dx1 = _mm(dgt0, W["ple_w_gate"][0], tb=True, add=dx1a, name="ple0_gate_dx")
    g["ple_w_gate"] = jnp.stack([g_gate0, g_gate1])
    g["ple_w_proj"] = jnp.stack([g_proj0, g_proj1])

    dy4, g["a_norm_post"] = _ew(post_norm_bwd, [dx1, y4], [W["a_norm_post"]], [MXU_DTYPE], 1, name="a_post_norm_bwd")
    g["a_w_out"] = _mm(y3, dy4, ta=True, name="a_out_dw")
    dy3 = _mm(dy4, W["a_w_out"], tb=True, out_dtype=MXU_DTYPE, name="a_out_dx")

    def gate_bwd(ct, gt_, tt_, gat, bg):
        _, vjp = jax.vjp(_glu_gate, gt_, tt_, gat, bg)
        dg_, dt_, dgate_, dbg = vjp(ct)
        return dg_, dt_, dgate_, dbg

    dgl_a, dt, dgate, g["a_b_glu"] = _ew(gate_bwd, [dy3, gl, t, gate], [W["a_b_glu"]], [F32, MXU_DTYPE, MXU_DTYPE], 1,
                                         name="a_gate_bwd")
    g["a_w_glu"] = _mm(gl, dt, ta=True, name="a_glu_dw")
    dgl = _mm(dt, W["a_w_glu"], tb=True, add=dgl_a, name="a_glu_dx")

    def gelu_bwd(ct, yt, ut, ds):
        _, vjp = jax.vjp(_gelu_skip, yt, ut, ds)
        return vjp(ct)

    dys, du_a, g["a_d_skip"] = _ew(gelu_bwd, [dgl, ys, u], [W["a_d_skip"]], [MXU_DTYPE, F32], 1, name="a_gelu_bwd")
    du, d_bbd, d_cbd, d_ar8, d_ai8 = _s5_bwd(u, dys, du_a, hs, bbd, cbd_t, tab_f, tab_b)
    g["a_w_in"] = jnp.concatenate([_mm(h1, du, ta=True, name="a_in_u_dw"),
                                   _mm(h1, dgate, ta=True, name="a_in_gate_dw")], axis=1)
    dh1 = _mm(dgate, w_ga, tb=True, add=_mm(du, w_u, tb=True, name="a_in_u_dx"), out_dtype=MXU_DTYPE,
              name="a_in_gate_dx")

    def pre_norm_bwd(ct, ch, xt, gn):
        _, vjp = jax.vjp(_rms, xt, gn)
        dx, dg = vjp(ch)
        return ct + dx, dg

    dx0, g["a_norm_pre"] = _ew(pre_norm_bwd, [dx1, dh1, x0], [W["a_norm_pre"]], [F32], 1, name="a_pre_norm_bwd")

    w = SSM_LANES
    d_bbt_re = _block_diag_extract(d_bbd[:, :, :w], GROUP_SIZE, STATE).reshape(N_GROUPS, GROUP_SIZE, STATE)
    d_bbt_im = _block_diag_extract(d_bbd[:, :, w:], GROUP_SIZE, STATE).reshape(N_GROUPS, GROUP_SIZE, STATE)
    d_crt = _block_diag_extract(d_cbd[:, :w, :], STATE, GROUP_SIZE).reshape(N_GROUPS, STATE, GROUP_SIZE)
    d_cit = -_block_diag_extract(d_cbd[:, w:, :], STATE, GROUP_SIZE).reshape(N_GROUPS, STATE, GROUP_SIZE)
    g["a_c_re"] = d_crt.transpose(0, 2, 1)[None]
    g["a_c_im"] = d_cit.transpose(0, 2, 1)[None]
    d_lr, d_li, d_ldt, d_btr, d_bti = _s5_params_bwd(
        W["lam_re"], W["lam_im"], W["log_dt"], W["bt_re"], W["bt_im"], _per_sublane(d_ar8), _per_sublane(d_ai8),
        d_bbt_re, d_bbt_im)
    g["a_lam_re"], g["a_lam_im"] = d_lr[None], d_li[None]
    g["a_log_dt"] = d_ldt.reshape(1, N_GROUPS)
    g["a_b_re"] = d_btr.transpose(0, 2, 1)[None]
    g["a_b_im"] = d_bti.transpose(0, 2, 1)[None]
    return loss, dx0, g


def kernel(x, p, a_norm_pre, a_norm_post, a_w_in, a_lam_re, a_lam_im, a_log_dt, a_b_re, a_b_im, a_c_re, a_c_im, a_d_skip, a_w_glu, a_b_glu, a_w_out, kv_norm, w_kv, b_norm_pre, b_norm_post, b_w_in, b_w_out, ple_w_proj, ple_w_gate, loss_target, m_a_norm_pre, m_a_norm_post, m_a_w_in, m_a_lam_re, m_a_lam_im, m_a_log_dt, m_a_b_re, m_a_b_im, m_a_c_re, m_a_c_im, m_a_d_skip, m_a_w_glu, m_a_b_glu, m_a_w_out, m_kv_norm, m_w_kv, m_b_norm_pre, m_b_norm_post, m_b_w_in, m_b_w_out, m_ple_w_proj, m_ple_w_gate, v_a_norm_pre, v_a_norm_post, v_a_w_in, v_a_lam_re, v_a_lam_im, v_a_log_dt, v_a_b_re, v_a_b_im, v_a_c_re, v_a_c_im, v_a_d_skip, v_a_w_glu, v_a_b_glu, v_a_w_out, v_kv_norm, v_w_kv, v_b_norm_pre, v_b_norm_post, v_b_w_in, v_b_w_out, v_ple_w_proj, v_ple_w_gate):
    loc = dict(a_norm_pre=a_norm_pre, a_norm_post=a_norm_post, a_w_in=a_w_in, a_lam_re=a_lam_re, a_lam_im=a_lam_im,
               a_log_dt=a_log_dt, a_b_re=a_b_re, a_b_im=a_b_im, a_c_re=a_c_re, a_c_im=a_c_im, a_d_skip=a_d_skip,
               a_w_glu=a_w_glu, a_b_glu=a_b_glu, a_w_out=a_w_out, kv_norm=kv_norm, w_kv=w_kv, b_norm_pre=b_norm_pre,
               b_norm_post=b_norm_post, b_w_in=b_w_in, b_w_out=b_w_out, ple_w_proj=ple_w_proj, ple_w_gate=ple_w_gate)
    mom = dict(a_norm_pre=m_a_norm_pre, a_norm_post=m_a_norm_post, a_w_in=m_a_w_in, a_lam_re=m_a_lam_re,
               a_lam_im=m_a_lam_im, a_log_dt=m_a_log_dt, a_b_re=m_a_b_re, a_b_im=m_a_b_im, a_c_re=m_a_c_re,
               a_c_im=m_a_c_im, a_d_skip=m_a_d_skip, a_w_glu=m_a_w_glu, a_b_glu=m_a_b_glu, a_w_out=m_a_w_out,
               kv_norm=m_kv_norm, w_kv=m_w_kv, b_norm_pre=m_b_norm_pre, b_norm_post=m_b_norm_post, b_w_in=m_b_w_in,
               b_w_out=m_b_w_out, ple_w_proj=m_ple_w_proj, ple_w_gate=m_ple_w_gate)
    var = dict(a_norm_pre=v_a_norm_pre, a_norm_post=v_a_norm_post, a_w_in=v_a_w_in, a_lam_re=v_a_lam_re,
               a_lam_im=v_a_lam_im, a_log_dt=v_a_log_dt, a_b_re=v_a_b_re, a_b_im=v_a_b_im, a_c_re=v_a_c_re,
               a_c_im=v_a_c_im, a_d_skip=v_a_d_skip, a_w_glu=v_a_w_glu, a_b_glu=v_a_b_glu, a_w_out=v_a_w_out,
               kv_norm=v_kv_norm, w_kv=v_w_kv, b_norm_pre=v_b_norm_pre, b_norm_post=v_b_norm_post, b_w_in=v_b_w_in,
               b_w_out=v_b_w_out, ple_w_proj=v_ple_w_proj, ple_w_gate=v_ple_w_gate)
    vec_names = [n for n, _ in _SHARDED if loc[n].shape[-2] == 1 and loc[n].ndim == 2]
    mat_names = [n for n, _ in _SHARDED if n not in vec_names]
    axis_of = dict(_SHARDED)
    mat_shapes = [loc[n].shape for n in mat_names]
    vec_shapes = [loc[n].shape for n in vec_names]
    mat_rows = _size_rows(mat_shapes, 32)
    vec_rows = _size_rows(vec_shapes, 16)
    mat_slab = _pack([loc[n] for n in mat_names], mat_rows).astype(MXU_DTYPE)
    vec_slab = _pack([loc[n] for n in vec_names], vec_rows)
    mat_all, vec_all = _all_gather_chips_two_level([mat_slab, vec_slab], name="gather_weights")
    W = {}
    for names, slab_all, shapes in ((mat_names, mat_all, mat_shapes), (vec_names, vec_all, vec_shapes)):
        per_chip = [_unpack(slab_all[j], shapes) for j in range(4)]
        for i, n in enumerate(names):
            W[n] = jnp.concatenate([per_chip[j][i] for j in range(4)], axis=axis_of[n])
    for n in ("a_w_in", "a_w_glu", "a_w_out", "b_w_in", "b_w_out"):
        W[n] = W[n][0]
    W["kv_norm"] = kv_norm.reshape(1, D_MODEL)
    W["b_norm_pre"], W["b_norm_post"] = b_norm_pre, b_norm_post
    W["lam_re"], W["lam_im"] = a_lam_re[0], a_lam_im[0]
    W["log_dt"] = a_log_dt.reshape(N_GROUPS, 1)
    W["bt_re"], W["bt_im"] = a_b_re[0].transpose(0, 2, 1), a_b_im[0].transpose(0, 2, 1)
    W["c_re"], W["c_im"] = a_c_re[0], a_c_im[0]

    loss, dx0, g = _forward_backward(x[0], p[0, 0], p[1, 0], loss_target[0], W)
    for n in ("a_w_in", "a_w_glu", "a_w_out", "b_w_in", "b_w_out"):
        g[n] = g[n][None]
    g["kv_norm"] = g["kv_norm"].reshape(D_MODEL)
    loss = lax.psum(loss[0, 0], ("x", "y", "c"))

    rep_shapes = [loc[n].shape for n in _REPLICATED]
    rep_rows = _size_rows(rep_shapes, 32)
    rep_slab = _pack([g[n] for n in _REPLICATED], rep_rows).reshape(4, rep_rows // 4, SLAB_W)
    sh_names = [n for n, _ in _SHARDED]
    sh_shapes = [loc[n].shape for n in sh_names]
    sh_rows = _size_rows(sh_shapes, 8)
    rows = _round_up(sh_rows + rep_rows // 4, 1024)
    half = rows // 2
    slabs = []
    for j in range(4):
        pieces = [jnp.split(g[n], 4, axis=axis_of[n])[j] for n in sh_names]
        sh = _pack(pieces, sh_rows)
        slabs.append(jnp.concatenate([sh, rep_slab[j], jnp.zeros((rows - sh_rows - rep_rows // 4, SLAB_W), F32)]))
    gs = jnp.stack(slabs).reshape(4, 2, half, SLAB_W)
    ci = lax.axis_index("c")
    mine = lax.dynamic_index_in_dim(gs, ci, axis=1, keepdims=False).reshape(4 * half, SLAB_W)
    theirs = lax.dynamic_index_in_dim(gs, 1 - ci, axis=1, keepdims=False).reshape(4 * half, SLAB_W)
    got = _sibling_exchange(theirs, name="reduce_cores")
    part, = _ew(lambda a, b: (a + b,), [mine, got], [], [MXU_DTYPE], 0, name="reduce_cores_add", tr=512)
    recv = _all_to_all_chips(part.reshape(4, half, SLAB_W), name="reduce_chips")
    tot_half, = _ew(lambda a, b, c, d: (((a + b) + c) + d,), [recv[0], recv[1], recv[2], recv[3]], [], [F32], 0,
                    name="reduce_chips_add", tr=512)
    other_half = _sibling_exchange(tot_half, name="share_cores")
    lo_half = jnp.where(ci == 0, tot_half, other_half)
    hi_half = jnp.where(ci == 0, other_half, tot_half)
    gsum = jnp.concatenate([lo_half, hi_half])
    rep_q = gsum[sh_rows:sh_rows + rep_rows // 4]
    rep_all, = _all_gather_chips([rep_q], name="gather_replicated")
    gsum_full = jnp.concatenate([gsum[:sh_rows], rep_all.reshape(rep_rows, SLAB_W)])

    all_names = sh_names + _REPLICATED
    all_shapes = sh_shapes + rep_shapes

    def slab_of(d):
        return jnp.concatenate([_pack([d[n] for n in sh_names], sh_rows), _pack([d[n] for n in _REPLICATED], rep_rows)])

    n_rows = sh_rows + rep_rows
    outs = _ew(_adamw_tile, [slab_of(loc), gsum_full, slab_of(mom), slab_of(var)], [], [F32] * 3, 0, name="adamw",
               tr=_pick_tile(n_rows, 512))
    res = []
    for slab in [gsum_full] + list(outs):
        parts = _unpack(slab[:sh_rows], sh_shapes) + _unpack(slab[sh_rows:], rep_shapes)
        res.append(dict(zip(all_names, parts)))
    out = [loss, dx0[None]]
    for d in res:
        out += [d[n] for n in _WEIGHT_ORDER]
    return tuple(out)
```

```python
import functools
import math

import jax
import jax.numpy as jnp
from jax import lax
from jax.experimental import pallas as pl
from jax.experimental.pallas import tpu as pltpu

F32 = jnp.float32
MXU_DTYPE = jnp.bfloat16

D_MODEL = 1024
N_GROUPS = 64
GROUP_SIZE = 16
STATE = 64
GROUPS_PER_BLOCK = 8
SSM_LANES = GROUPS_PER_BLOCK * STATE
N_GROUP_BLOCKS = N_GROUPS // GROUPS_PER_BLOCK
HEAD_DIM = 64
HEADS_PER_BLOCK = 2
BWD_QUERY_BLOCK = 256
KEY_BLOCKS_PER_STEP = 4
FWD_QUERY_BLOCK = 512
FWD_KEY_BLOCKS_PER_STEP = 4
CUMSUM_CHUNK = 256
SCAN_TILES_PER_STEP = 2
EPS = 1e-6
SLAB_W = 512
SUBLANES = 8
VMEM_LIMIT = 56 * 1024 * 1024

ADAM_LR = 0.001
ADAM_B1 = 0.9
ADAM_B2 = 0.999
ADAM_EPS = 1e-08
ADAM_WD = 0.01
ADAM_STEP = 10

MESH = pl.DeviceIdType.MESH
ANY = pl.BlockSpec(memory_space=pl.ANY)


def _cparams(sem):
    return pltpu.CompilerParams(dimension_semantics=sem, vmem_limit_bytes=VMEM_LIMIT)


def _mm(a, b, *, ta=False, tb=False, add=None, out_dtype=F32, name, tm=1024, tn=1024, tk=1024):
    if ta:
        kdim, m = a.shape
    else:
        m, kdim = a.shape
    if tb:
        n, kb = b.shape
    else:
        kb, n = b.shape
    assert kdim == kb, (a.shape, b.shape, ta, tb)
    tm, tn, tk = min(tm, m), min(tn, n), min(tk, kdim)
    assert m % tm == 0 and n % tn == 0 and kdim % tk == 0
    nk = kdim // tk
    dims = (((0 if ta else 1,), (1 if tb else 0,)), ((), ()))
    has_add = add is not None

    def body(*refs):
        if has_add:
            a_ref, b_ref, add_ref, o_ref, acc_ref = refs
        else:
            a_ref, b_ref, o_ref, acc_ref = refs
        k = pl.program_id(2)

        @pl.when(k == 0)
        def _():
            acc_ref[...] = jnp.zeros_like(acc_ref)

        acc_ref[...] += lax.dot_general(a_ref[...].astype(MXU_DTYPE), b_ref[...].astype(MXU_DTYPE), dims,
                                        preferred_element_type=F32)

        @pl.when(k == nk - 1)
        def _():
            r = acc_ref[...]
            if has_add:
                r = r + add_ref[...].astype(F32)
            o_ref[...] = r.astype(o_ref.dtype)

    a_spec = pl.BlockSpec((tk, tm), lambda i, j, k: (k, i)) if ta else pl.BlockSpec((tm, tk), lambda i, j, k: (i, k))
    b_spec = pl.BlockSpec((tn, tk), lambda i, j, k: (j, k)) if tb else pl.BlockSpec((tk, tn), lambda i, j, k: (k, j))
    o_spec = pl.BlockSpec((tm, tn), lambda i, j, k: (i, j))
    in_specs = [a_spec, b_spec] + ([o_spec] if has_add else [])
    args = (a, b) + ((add,) if has_add else ())
    return pl.pallas_call(
        body, name=name, grid=(m // tm, n // tn, nk), in_specs=in_specs, out_specs=o_spec,
        out_shape=jax.ShapeDtypeStruct((m, n), out_dtype), scratch_shapes=[pltpu.VMEM((tm, tn), F32)],
        compiler_params=_cparams(("parallel", "parallel", "arbitrary")),
    )(*args)


def _ew(fn, rows, params, row_dtypes, n_acc, *, name, tr=256):
    nrows = rows[0].shape[0]
    tr = min(tr, nrows)
    assert nrows % tr == 0
    n_in, n_par, n_row = len(rows), len(params), len(row_dtypes)
    tile_avals = [jax.ShapeDtypeStruct((tr, r.shape[1]), F32) for r in rows]
    par_avals = [jax.ShapeDtypeStruct(q.shape, F32) for q in params]
    out_avals = jax.eval_shape(fn, *tile_avals, *par_avals)
    assert len(out_avals) == n_row + n_acc

    def body(*refs):
        ins = [r[...].astype(F32) for r in refs[:n_in + n_par]]
        outs = fn(*ins)
        row_refs = refs[n_in + n_par:n_in + n_par + n_row]
        acc_refs = refs[n_in + n_par + n_row:]
        for r, o in zip(row_refs, outs[:n_row]):
            r[...] = o.astype(r.dtype)
        if n_acc:
            @pl.when(pl.program_id(0) == 0)
            def _():
                for r in acc_refs:
                    r[...] = jnp.zeros_like(r)
            for r, o in zip(acc_refs, outs[n_row:]):
                r[...] += o

    in_specs = [pl.BlockSpec((tr, r.shape[1]), lambda i: (i, 0)) for r in rows]
    in_specs += [pl.BlockSpec(q.shape, lambda i: (0, 0)) for q in params]
    out_specs = [pl.BlockSpec((tr, o.shape[1]), lambda i: (i, 0)) for o in out_avals[:n_row]]
    out_specs += [pl.BlockSpec(o.shape, lambda i: (0, 0)) for o in out_avals[n_row:]]
    out_shape = [jax.ShapeDtypeStruct((nrows, o.shape[1]), dt) for o, dt in zip(out_avals[:n_row], row_dtypes)]
    out_shape += [jax.ShapeDtypeStruct(o.shape, F32) for o in out_avals[n_row:]]
    return pl.pallas_call(
        body, name=name, grid=(nrows // tr,), in_specs=in_specs, out_specs=out_specs, out_shape=out_shape,
        compiler_params=_cparams(("arbitrary",)),
    )(*rows, *params)


def _rms(xt, g):
    r = lax.rsqrt(jnp.mean(xt * xt, axis=-1, keepdims=True) + EPS)
    return xt * r * g


def _gelu_skip(ys, u, dskip):
    return jax.nn.gelu(ys + dskip * u)


def _glu_gate(g, t, gate, bglu):
    return g * jax.nn.sigmoid(t + bglu) * jax.nn.silu(gate)


def _ogate(o, gate):
    return o * jax.nn.silu(gate)


def _ple(xt, gt, pp):
    return xt + jax.nn.sigmoid(gt) * pp


def _s5_disc(lam_re, lam_im, log_dt, bt_re, bt_im):
    lr = jnp.minimum(lam_re, -1e-4)
    li = lam_im
    dt = jnp.exp(log_dt)
    mag = jnp.exp(lr * dt)
    a_re = mag * jnp.cos(li * dt)
    a_im = mag * jnp.sin(li * dt)
    den = lr * lr + li * li
    nr = a_re - 1.0
    f_re = (nr * lr + a_im * li) / den
    f_im = (a_im * lr - nr * li) / den
    bb_re = f_re[:, None, :] * bt_re - f_im[:, None, :] * bt_im
    bb_im = f_re[:, None, :] * bt_im + f_im[:, None, :] * bt_re
    return a_re, a_im, bb_re, bb_im


def _s5_params_fwd(lam_re, lam_im, log_dt, bt_re, bt_im):
    def body(lr_ref, li_ref, dt_ref, br_ref, bi_ref, ar_ref, ai_ref, bbr_ref, bbi_ref, pr_ref, pi_ref):
        a_re, a_im, bb_re, bb_im = _s5_disc(lr_ref[...], li_ref[...], dt_ref[...], br_ref[...], bi_ref[...])
        ar_ref[...] = a_re
        ai_ref[...] = a_im
        bbr_ref[...] = bb_re
        bbi_ref[...] = bb_im
        pr, pi = a_re, a_im
        for t in range(SUBLANES):
            pr_ref[t] = pr
            pi_ref[t] = pi
            pr, pi = pr * a_re - pi * a_im, pr * a_im + pi * a_re

    g, s = lam_re.shape
    sd = jax.ShapeDtypeStruct
    return pl.pallas_call(
        body, name="s5_params_fwd",
        out_shape=[sd((g, s), F32), sd((g, s), F32), sd(bt_re.shape, F32), sd(bt_re.shape, F32),
                   sd((SUBLANES, g, s), F32), sd((SUBLANES, g, s), F32)],
    )(lam_re, lam_im, log_dt, bt_re, bt_im)


def _s5_params_bwd(lam_re, lam_im, log_dt, bt_re, bt_im, d_ar, d_ai, d_bbr, d_bbi):
    def body(lr_ref, li_ref, dt_ref, br_ref, bi_ref, c0, c1, c2, c3, o0, o1, o2, o3, o4):
        _, vjp = jax.vjp(_s5_disc, lr_ref[...], li_ref[...], dt_ref[...], br_ref[...], bi_ref[...])
        grads = vjp((jnp.sum(c0[...], axis=0), jnp.sum(c1[...], axis=0), c2[...], c3[...]))
        for o, gval in zip((o0, o1, o2, o3, o4), grads):
            o[...] = gval

    sd = jax.ShapeDtypeStruct
    return pl.pallas_call(
        body, name="s5_params_bwd",
        out_shape=[sd(lam_re.shape, F32), sd(lam_im.shape, F32), sd(log_dt.shape, F32), sd(bt_re.shape, F32),
                   sd(bt_im.shape, F32)],
    )(lam_re, lam_im, log_dt, bt_re, bt_im, d_ar, d_ai, d_bbr, d_bbi)


def _block_diag(t):
    nb, g, r, c = t.shape
    eye = jnp.eye(g, dtype=t.dtype)
    return jnp.einsum("ngrc,gh->ngrhc", t, eye).reshape(nb, g * r, g * c)


def _block_diag_extract(m, r, c):
    nb = m.shape[0]
    g = GROUPS_PER_BLOCK
    eye = jnp.eye(g, dtype=m.dtype)[None, :, None, :, None]
    return jnp.sum(m.reshape(nb, g, r, g, c) * eye, axis=3)


def _per_sublane(acc):
    return acc.reshape(N_GROUP_BLOCKS, SUBLANES, GROUPS_PER_BLOCK, STATE).transpose(1, 0, 2, 3).reshape(
        SUBLANES, N_GROUPS, STATE)


def _scan_tables(pr, pi):
    row = jnp.arange(SUBLANES)[:, None, None]

    def tables(im_sign, fwd):
        kinds = []
        for k in (1, 2, 4):
            mask = (row >= k) if fwd else (row < SUBLANES - k)
            kinds.append(jnp.where(mask, pr[k - 1][None], 0.0))
            kinds.append(jnp.where(mask, im_sign * pi[k - 1][None], 0.0))
        if fwd:
            kinds += [pr, im_sign * pi]
        else:
            kinds += [pr[::-1], im_sign * pi[::-1]]
        return jnp.stack(kinds, axis=0).transpose(2, 0, 1, 3)

    return tables(1.0, True), tables(-1.0, False)


def _scan_rows(buf_ref, tab_ref, carry_r, carry_i, n_tiles, reverse):
    w = SSM_LANES
    assert n_tiles % SCAN_TILES_PER_STEP == 0

    def local_scan(jj):
        rows = pl.ds(pl.multiple_of(jj * SUBLANES, SUBLANES), SUBLANES)
        zr = buf_ref[rows, 0:w]
        zi = buf_ref[rows, w:2 * w]
        for n, k in enumerate((1, 2, 4)):
            ar = tab_ref[2 * n]
            ai = tab_ref[2 * n + 1]
            sh = (SUBLANES - k) if reverse else k
            rr = pltpu.roll(zr, sh, 0)
            ri = pltpu.roll(zi, sh, 0)
            zr, zi = zr + ar * rr - ai * ri, zi + ar * ri + ai * rr
        return rows, zr, zi

    def step(j, carry):
        first = j * SCAN_TILES_PER_STEP
        tiles = [local_scan((n_tiles - 1 - first - d) if reverse else (first + d)) for d in range(SCAN_TILES_PER_STEP)]
        cr, ci = carry
        pr = tab_ref[6]
        pi = tab_ref[7]
        for rows, zr, zi in tiles:
            zr, zi = zr + pr * cr - pi * ci, zi + pr * ci + pi * cr
            buf_ref[rows, 0:w] = zr
            buf_ref[rows, w:2 * w] = zi
            cr, ci = (zr[0:1], zi[0:1]) if reverse else (zr[SUBLANES - 1:SUBLANES], zi[SUBLANES - 1:SUBLANES])
        return cr, ci

    return lax.fori_loop(0, n_tiles // SCAN_TILES_PER_STEP, step, (carry_r, carry_i))


def _s5_fwd(u, bbd, cbd, tab_f, *, tt=512):
    seq = u.shape[0]
    tt = min(tt, seq)
    nt = seq // tt
    w = SSM_LANES

    def body(u_ref, b_ref, c_ref, tab_ref, y_ref, hs_ref, buf_ref, h_ref):
        it = pl.program_id(1)

        @pl.when(it == 0)
        def _():
            h_ref[...] = jnp.zeros_like(h_ref)

        hs_ref[...] = h_ref[...]
        buf_ref[...] = jnp.dot(u_ref[...].astype(MXU_DTYPE), b_ref[...], preferred_element_type=F32)
        hr, hi = _scan_rows(buf_ref, tab_ref, h_ref[:, 0:w], h_ref[:, w:2 * w], tt // SUBLANES, False)
        h_ref[:, 0:w] = hr
        h_ref[:, w:2 * w] = hi
        y_ref[...] = jnp.dot(buf_ref[...].astype(MXU_DTYPE), c_ref[...],
                             preferred_element_type=F32).astype(y_ref.dtype)

    return pl.pallas_call(
        body, name="s5_fwd", grid=(N_GROUP_BLOCKS, nt),
        in_specs=[pl.BlockSpec((tt, 128), lambda g, t: (t, g)),
                  pl.BlockSpec((None, 128, 2 * w), lambda g, t: (g, 0, 0)),
                  pl.BlockSpec((None, 2 * w, 128), lambda g, t: (g, 0, 0)),
                  pl.BlockSpec((None, 8, SUBLANES, w), lambda g, t: (g, 0, 0, 0))],
        out_specs=[pl.BlockSpec((tt, 128), lambda g, t: (t, g)),
                   pl.BlockSpec((None, None, 1, 2 * w), lambda g, t: (g, t, 0, 0))],
        out_shape=[jax.ShapeDtypeStruct((seq, D_MODEL), MXU_DTYPE),
                   jax.ShapeDtypeStruct((N_GROUP_BLOCKS, nt, 1, 2 * w), F32)],
        scratch_shapes=[pltpu.VMEM((tt, 2 * w), F32), pltpu.VMEM((1, 2 * w), F32)],
        compiler_params=_cparams(("parallel", "arbitrary")),
    )(u, bbd, cbd, tab_f)


def _s5_bwd(u, dy, du_add, hs, bbd, cbd_t, tab_f, tab_b, *, tt=512):
    seq = u.shape[0]
    tt = min(tt, seq)
    nt = seq // tt
    w = SSM_LANES
    n_tiles = tt // SUBLANES

    def body(u_ref, dy_ref, dua_ref, hs_ref, b_ref, ct_ref, tabf_ref, tabb_ref,
             du_ref, db_ref, dc_ref, dar_ref, dai_ref, s_ref, l_ref, lam_ref):
        it = pl.program_id(1)

        @pl.when(it == 0)
        def _():
            lam_ref[...] = jnp.zeros_like(lam_ref)
            db_ref[...] = jnp.zeros_like(db_ref)
            dc_ref[...] = jnp.zeros_like(dc_ref)
            dar_ref[...] = jnp.zeros_like(dar_ref)
            dai_ref[...] = jnp.zeros_like(dai_ref)

        ub = u_ref[...].astype(MXU_DTYPE)
        dyb = dy_ref[...].astype(MXU_DTYPE)
        s_ref[...] = jnp.dot(ub, b_ref[...], preferred_element_type=F32)
        h_in_r = hs_ref[:, 0:w]
        h_in_i = hs_ref[:, w:2 * w]
        _scan_rows(s_ref, tabf_ref, h_in_r, h_in_i, n_tiles, False)
        dc_ref[...] += lax.dot_general(s_ref[...].astype(MXU_DTYPE), dyb, (((0,), (0,)), ((), ())),
                                       preferred_element_type=F32)
        l_ref[...] = jnp.dot(dyb, ct_ref[...], preferred_element_type=F32)
        lr, li = _scan_rows(l_ref, tabb_ref, lam_ref[:, 0:w], lam_ref[:, w:2 * w], n_tiles, True)
        lam_ref[:, 0:w] = lr
        lam_ref[:, w:2 * w] = li
        lb = l_ref[...].astype(MXU_DTYPE)
        db_ref[...] += lax.dot_general(ub, lb, (((0,), (0,)), ((), ())), preferred_element_type=F32)
        du = lax.dot_general(lb, b_ref[...], (((1,), (1,)), ((), ())), preferred_element_type=F32)
        du_ref[...] = (du + dua_ref[...].astype(F32)).astype(du_ref.dtype)

        def tile(j, carry):
            pr, pi, accr, acci = carry
            rows = pl.ds(pl.multiple_of(j * SUBLANES, SUBLANES), SUBLANES)
            sr = s_ref[rows, 0:w]
            si = s_ref[rows, w:2 * w]
            first = lax.broadcasted_iota(jnp.int32, (SUBLANES, w), 0) == 0
            sr_prev = jnp.where(first, pr, pltpu.roll(sr, 1, 0))
            si_prev = jnp.where(first, pi, pltpu.roll(si, 1, 0))
            gr = l_ref[rows, 0:w]
            gi = l_ref[rows, w:2 * w]
            accr = accr + gr * sr_prev + gi * si_prev
            acci = acci + gi * sr_prev - gr * si_prev
            return sr[SUBLANES - 1:SUBLANES], si[SUBLANES - 1:SUBLANES], accr, acci

        zero = jnp.zeros((SUBLANES, w), F32)
        _, _, accr, acci = lax.fori_loop(0, n_tiles, tile, (h_in_r, h_in_i, zero, zero))
        dar_ref[...] += accr
        dai_ref[...] += acci

    rev = lambda g, t: (nt - 1 - t, g)
    return pl.pallas_call(
        body, name="s5_bwd", grid=(N_GROUP_BLOCKS, nt),
        in_specs=[pl.BlockSpec((tt, 128), rev), pl.BlockSpec((tt, 128), rev), pl.BlockSpec((tt, 128), rev),
                  pl.BlockSpec((None, None, 1, 2 * w), lambda g, t: (g, nt - 1 - t, 0, 0)),
                  pl.BlockSpec((None, 128, 2 * w), lambda g, t: (g, 0, 0)),
                  pl.BlockSpec((None, 128, 2 * w), lambda g, t: (g, 0, 0)),
                  pl.BlockSpec((None, 8, SUBLANES, w), lambda g, t: (g, 0, 0, 0)),
                  pl.BlockSpec((None, 8, SUBLANES, w), lambda g, t: (g, 0, 0, 0))],
        out_specs=[pl.BlockSpec((tt, 128), rev),
                   pl.BlockSpec((None, 128, 2 * w), lambda g, t: (g, 0, 0)),
                   pl.BlockSpec((None, 2 * w, 128), lambda g, t: (g, 0, 0)),
                   pl.BlockSpec((None, SUBLANES, w), lambda g, t: (g, 0, 0)),
                   pl.BlockSpec((None, SUBLANES, w), lambda g, t: (g, 0, 0))],
        out_shape=[jax.ShapeDtypeStruct((seq, D_MODEL), MXU_DTYPE),
                   jax.ShapeDtypeStruct((N_GROUP_BLOCKS, 128, 2 * w), F32),
                   jax.ShapeDtypeStruct((N_GROUP_BLOCKS, 2 * w, 128), F32),
                   jax.ShapeDtypeStruct((N_GROUP_BLOCKS, SUBLANES, w), F32),
                   jax.ShapeDtypeStruct((N_GROUP_BLOCKS, SUBLANES, w), F32)],
        scratch_shapes=[pltpu.VMEM((tt, 2 * w), F32), pltpu.VMEM((tt, 2 * w), F32), pltpu.VMEM((1, 2 * w), F32)],
        compiler_params=_cparams(("parallel", "arbitrary")),
    )(u, dy, du_add, hs, bbd, cbd_t, tab_f, tab_b)


LOG2E = 1.4426950408889634
LN2 = 0.6931471805599453
SOFTPLUS2_LINEAR = 28.0


def _softplus2(z):
    u = 1.0 + jnp.exp2(z)
    return jnp.where(z > SOFTPLUS2_LINEAR, z, jnp.log2(u)), u


def _tri(n, keep, value=1.0):
    row = lax.broadcasted_iota(jnp.int32, (n, n), 0)
    col = lax.broadcasted_iota(jnp.int32, (n, n), 1)
    return jnp.where(keep(row, col), value, 0.0).astype(MXU_DTYPE)


def _suffix_sums(x, tri):
    c = tri.shape[0]
    outs, carry = [], None
    for i in reversed(range(x.shape[1] // c)):
        part = jnp.dot(x[:, i * c:(i + 1) * c].astype(MXU_DTYPE), tri, preferred_element_type=F32)
        if carry is not None:
            part = part + carry
        carry = part[:, 0:1]
        outs.append(part)
    return jnp.concatenate(outs[::-1], axis=1) if len(outs) > 1 else outs[0]


def _prefix_sums(x, tri, carry):
    c = tri.shape[0]
    outs = []
    for i in range(x.shape[1] // c):
        part = jnp.dot(x[:, i * c:(i + 1) * c].astype(MXU_DTYPE), tri, preferred_element_type=F32) + carry
        carry = part[:, c - 1:c]
        outs.append(part)
    return jnp.concatenate(outs, axis=1) if len(outs) > 1 else outs[0]


def _head_masks():
    lane = lax.broadcasted_iota(jnp.int32, (1, HEADS_PER_BLOCK * HEAD_DIM), 1)
    return [(lane >= h * HEAD_DIM) & (lane < (h + 1) * HEAD_DIM) for h in range(HEADS_PER_BLOCK)]


def _sba_fwd(q, k, v, *, tb=FWD_QUERY_BLOCK, key_blocks=FWD_KEY_BLOCKS_PER_STEP):
    seq = q.shape[0]
    tb = min(tb, seq)
    nq = seq // tb
    cw = HEADS_PER_BLOCK * HEAD_DIM
    scale = HEAD_DIM ** -0.5

    def body(q_ref, k_ref, v_ref, o_ref, t_ref, acc_ref, r_ref):
        qi = pl.program_id(1)
        masks = _head_masks()
        qf = q_ref[...].astype(F32) * (scale * LOG2E)
        q2 = jnp.concatenate([jnp.where(m, qf, 0.0) for m in masks], axis=0).astype(MXU_DTYPE)
        causal = lax.broadcasted_iota(jnp.int32, (tb, tb), 1) < lax.broadcasted_iota(jnp.int32, (tb, tb), 0)
        causal2 = jnp.concatenate([causal] * HEADS_PER_BLOCK, axis=0)
        neg_upper = _tri(min(tb, CUMSUM_CHUNK), lambda j, s: j >= s, -1.0)
        acc_ref[...] = jnp.zeros_like(acc_ref)
        r_ref[...] = jnp.zeros_like(r_ref)

        def block(start, width, diag):
            ks = pl.ds(pl.multiple_of(start, tb), width)
            kb = k_ref[ks, :].astype(MXU_DTYPE)
            vb = v_ref[ks, :]
            z = lax.dot_general(q2, kb, (((1,), (1,)), ((), ())), preferred_element_type=F32)
            sp, _ = _softplus2(z)
            if diag:
                sp = jnp.where(causal2, sp, 0.0)
            incl = _suffix_sums(sp, neg_upper)
            a = z + incl + r_ref[...]
            if diag:
                a = jnp.where(causal2, a, -1e30)
            wgt = jnp.exp2(a).astype(MXU_DTYPE)
            wcat = jnp.concatenate([wgt[h * tb:(h + 1) * tb] for h in range(HEADS_PER_BLOCK)], axis=1)
            vcat = jnp.concatenate([jnp.where(m, vb, jnp.zeros_like(vb)) for m in masks], axis=0).astype(MXU_DTYPE)
            acc_ref[...] += jnp.dot(wcat, vcat, preferred_element_type=F32)
            r_ref[...] += incl[:, 0:1]

        block(qi * tb, tb, True)
        n_wide = qi // key_blocks

        def single(i, c):
            block((qi - 1 - i) * tb, tb, False)
            return c

        lax.fori_loop(0, qi - n_wide * key_blocks, single, 0)

        def step(i, c):
            block((n_wide - 1 - i) * (key_blocks * tb), key_blocks * tb, False)
            return c

        lax.fori_loop(0, n_wide, step, 0)
        o_ref[...] = acc_ref[...].astype(o_ref.dtype)
        for h in range(HEADS_PER_BLOCK):
            t_ref[:, h:h + 1] = r_ref[h * tb:(h + 1) * tb]

    n_hp = D_MODEL // cw
    return pl.pallas_call(
        body, name="sba_fwd", grid=(n_hp, nq),
        in_specs=[pl.BlockSpec((tb, cw), lambda h, i: (i, h)),
                  pl.BlockSpec((seq, cw), lambda h, i: (0, h)),
                  pl.BlockSpec((seq, cw), lambda h, i: (0, h))],
        out_specs=[pl.BlockSpec((tb, cw), lambda h, i: (i, h)),
                   pl.BlockSpec((None, tb, HEADS_PER_BLOCK), lambda h, i: (h, i, 0))],
        out_shape=[jax.ShapeDtypeStruct((seq, D_MODEL), MXU_DTYPE),
                   jax.ShapeDtypeStruct((n_hp, seq, HEADS_PER_BLOCK), F32)],
        scratch_shapes=[pltpu.VMEM((tb, cw), F32), pltpu.VMEM((HEADS_PER_BLOCK * tb, 1), F32)],
        compiler_params=_cparams(("parallel", "arbitrary")),
    )(q, k, v)


def _sba_bwd(q, k, v, do, tot, *, tb=BWD_QUERY_BLOCK):
    seq = q.shape[0]
    tb = min(tb, seq)
    nq = seq // tb
    cw = HEADS_PER_BLOCK * HEAD_DIM
    scale = HEAD_DIM ** -0.5

    def body(q_ref, k_ref, v_ref, do_ref, t_ref, dq_ref, dk_ref, dv_ref, acc_ref, lp_ref, dp_ref):
        qi = pl.program_id(1)

        @pl.when(qi == 0)
        def _():
            dk_ref[...] = jnp.zeros_like(dk_ref)
            dv_ref[...] = jnp.zeros_like(dv_ref)

        masks = _head_masks()
        qf = q_ref[...].astype(F32) * (scale * LOG2E)
        q2 = jnp.concatenate([jnp.where(m, qf, 0.0) for m in masks], axis=0).astype(MXU_DTYPE)
        dof = do_ref[...].astype(F32)
        do2 = jnp.concatenate([jnp.where(m, dof, 0.0) for m in masks], axis=0).astype(MXU_DTYPE)
        tot2 = jnp.concatenate([t_ref[:, h:h + 1] for h in range(HEADS_PER_BLOCK)], axis=0)
        causal = lax.broadcasted_iota(jnp.int32, (tb, tb), 1) < lax.broadcasted_iota(jnp.int32, (tb, tb), 0)
        causal2 = jnp.concatenate([causal] * HEADS_PER_BLOCK, axis=0)
        neg_upper = _tri(min(tb, CUMSUM_CHUNK), lambda j, s: j >= s, -1.0)
        lower = _tri(min(tb, CUMSUM_CHUNK), lambda j, s: j <= s)
        acc_ref[...] = jnp.zeros_like(acc_ref)
        lp_ref[...] = jnp.zeros_like(lp_ref)
        dp_ref[...] = jnp.zeros_like(dp_ref)

        def block(start, width, diag):
            ks = pl.ds(pl.multiple_of(start, tb), width)
            kb = k_ref[ks, :]
            vb = v_ref[ks, :].astype(MXU_DTYPE)
            z = lax.dot_general(q2, kb.astype(MXU_DTYPE), (((1,), (1,)), ((), ())), preferred_element_type=F32)
            sp, u = _softplus2(z)
            if diag:
                sp = jnp.where(causal2, sp, 0.0)
            incl = _suffix_sums(sp, neg_upper)
            lp_next = lp_ref[...] + incl[:, 0:1]
            lp_ref[...] = lp_next
            a = z + incl + (tot2 - lp_next)
            if diag:
                a = jnp.where(causal2, a, -1e30)
            wgt = jnp.exp2(a)
            dw = lax.dot_general(do2, vb, (((1,), (1,)), ((), ())), preferred_element_type=F32)
            da = dw * wgt
            pre = _prefix_sums(da, lower, dp_ref[...])
            dp_ref[...] = pre[:, width - 1:width]
            sig = 1.0 - pl.reciprocal(u, approx=True)
            dz = da - pre * sig
            if diag:
                dz = jnp.where(causal2, dz, 0.0)
            dzb = dz.astype(MXU_DTYPE)
            dzcat = jnp.concatenate([dzb[h * tb:(h + 1) * tb] for h in range(HEADS_PER_BLOCK)], axis=1)
            kcat = jnp.concatenate([jnp.where(m, kb, jnp.zeros_like(kb)) for m in masks], axis=0).astype(MXU_DTYPE)
            acc_ref[...] += jnp.dot(dzcat, kcat, preferred_element_type=F32)
            dk_ref[ks, :] += lax.dot_general(dzb, q2, (((0,), (0,)), ((), ())), preferred_element_type=F32) * LN2
            dv_ref[ks, :] += lax.dot_general(wgt.astype(MXU_DTYPE), do2, (((0,), (0,)), ((), ())),
                                             preferred_element_type=F32)

        n_wide = qi // KEY_BLOCKS_PER_STEP

        def step(i, c):
            block(i * (KEY_BLOCKS_PER_STEP * tb), KEY_BLOCKS_PER_STEP * tb, False)
            return c

        lax.fori_loop(0, n_wide, step, 0)

        def single(i, c):
            block((n_wide * KEY_BLOCKS_PER_STEP + i) * tb, tb, False)
            return c

        lax.fori_loop(0, qi - n_wide * KEY_BLOCKS_PER_STEP, single, 0)
        block(qi * tb, tb, True)
        dq_ref[...] = acc_ref[...] * scale

    n_hp = D_MODEL // cw
    qspec = pl.BlockSpec((tb, cw), lambda h, i: (i, h))
    full = pl.BlockSpec((seq, cw), lambda h, i: (0, h))
    return pl.pallas_call(
        body, name="sba_bwd", grid=(n_hp, nq),
        in_specs=[qspec, full, full, qspec, pl.BlockSpec((None, tb, HEADS_PER_BLOCK), lambda h, i: (h, i, 0))],
        out_specs=[qspec, full, full],
        out_shape=[jax.ShapeDtypeStruct((seq, D_MODEL), F32)] * 3,
        scratch_shapes=[pltpu.VMEM((tb, cw), F32), pltpu.VMEM((HEADS_PER_BLOCK * tb, 1), F32),
                        pltpu.VMEM((HEADS_PER_BLOCK * tb, 1), F32)],
        compiler_params=_cparams(("parallel", "arbitrary")),
    )(q, k, v, do, tot)


def _my_place():
    return lax.axis_index("x"), lax.axis_index("y"), lax.axis_index("c")


def _other_chips(x, y):
    return [(1 - x, y), (x, 1 - y), (1 - x, 1 - y)]


def _all_gather_chips(slabs, *, name):
    n = len(slabs)

    def body(*refs):
        ins, outs = refs[:n], refs[n:2 * n]
        send_sems, recv_sems, loc_sems = refs[2 * n:]
        x, y, c = _my_place()
        me = 2 * x + y
        local = [pltpu.make_async_copy(ins[i], outs[i].at[me], loc_sems.at[i]) for i in range(n)]
        for cp in local:
            cp.start()
        remote = []
        for j, (px, py) in enumerate(_other_chips(x, y)):
            for i in range(n):
                remote.append(pltpu.make_async_remote_copy(
                    src_ref=ins[i], dst_ref=outs[i].at[me], send_sem=send_sems.at[j * n + i],
                    recv_sem=recv_sems.at[j * n + i], device_id=(px, py, c), device_id_type=MESH))
        for cp in remote:
            cp.start()
        for cp in remote:
            cp.wait_recv()
        for cp in remote:
            cp.wait_send()
        for cp in local:
            cp.wait()

    return pl.pallas_call(
        body, name=name, in_specs=[ANY] * n, out_specs=[ANY] * n,
        out_shape=[jax.ShapeDtypeStruct((4,) + s.shape, s.dtype) for s in slabs],
        scratch_shapes=[pltpu.SemaphoreType.DMA((3 * n,)), pltpu.SemaphoreType.DMA((3 * n,)),
                        pltpu.SemaphoreType.DMA((n,))],
    )(*slabs)


def _all_gather_chips_two_level(slabs, *, name):
    n = len(slabs)
    halves = [s.shape[0] // 2 for s in slabs]

    def body(*refs):
        ins, outs = refs[:n], refs[n:2 * n]
        ici_send, ici_recv, d2d_send, d2d_recv = refs[2 * n:]
        x, y, c = _my_place()
        chips = _other_chips(x, y)
        mine = [pl.ds(c * h, h) for h in halves]
        theirs = [pl.ds((1 - c) * h, h) for h in halves]

        def over_ici(j, i, chip):
            px, py = chips[j]
            slot = 2 * chip[0] + chip[1]
            return pltpu.make_async_remote_copy(
                src_ref=ins[i].at[mine[i]], dst_ref=outs[i].at[slot, mine[i]], send_sem=ici_send.at[j * n + i],
                recv_sem=ici_recv.at[j * n + i], device_id=(px, py, c), device_id_type=MESH)

        def over_d2d(j, i, rows):
            px, py = chips[j]
            block = outs[i].at[2 * px + py, rows]
            return pltpu.make_async_remote_copy(
                src_ref=block, dst_ref=block, send_sem=d2d_send.at[j * n + i], recv_sem=d2d_recv.at[j * n + i],
                device_id=(x, y, 1 - c), device_id_type=MESH)

        sent = [over_ici(j, i, (x, y)) for j in range(3) for i in range(n)]
        for cp in sent:
            cp.start()
        passed = []
        for j in range(3):
            for i in range(n):
                over_ici(j, i, chips[j]).wait_recv()
                cp = over_d2d(j, i, mine[i])
                cp.start()
                passed.append(cp)
        for j in range(3):
            for i in range(n):
                over_d2d(j, i, theirs[i]).wait_recv()
        for cp in sent + passed:
            cp.wait_send()

    outs = pl.pallas_call(
        body, name=name, in_specs=[ANY] * n, out_specs=[ANY] * n,
        out_shape=[jax.ShapeDtypeStruct((4,) + s.shape, s.dtype) for s in slabs],
        scratch_shapes=[pltpu.SemaphoreType.DMA((3 * n,)), pltpu.SemaphoreType.DMA((3 * n,)),
                        pltpu.SemaphoreType.DMA((3 * n,)), pltpu.SemaphoreType.DMA((3 * n,))],
    )(*slabs)
    me = 2 * lax.axis_index("x") + lax.axis_index("y")
    return [[jnp.where(me == j, s, o[j]) for j in range(4)] for o, s in zip(outs, slabs)]


def _sibling_exchange(send, *, name):
    def body(s_ref, o_ref, send_sem, recv_sem):
        x, y, c = _my_place()
        cp = pltpu.make_async_remote_copy(src_ref=s_ref, dst_ref=o_ref, send_sem=send_sem, recv_sem=recv_sem,
                                          device_id=(x, y, 1 - c), device_id_type=MESH)
        cp.start()
        cp.wait_recv()
        cp.wait_send()

    return pl.pallas_call(
        body, name=name, in_specs=[ANY], out_specs=ANY,
        out_shape=jax.ShapeDtypeStruct(send.shape, send.dtype),
        scratch_shapes=[pltpu.SemaphoreType.DMA, pltpu.SemaphoreType.DMA],
    )(send)


def _all_to_all_chips(parts, *, name):
    def body(p_ref, o_ref, send_sems, recv_sems):
        x, y, c = _my_place()
        me = 2 * x + y
        remote = []
        for j, (px, py) in enumerate(_other_chips(x, y)):
            remote.append(pltpu.make_async_remote_copy(
                src_ref=p_ref.at[2 * px + py], dst_ref=o_ref.at[me], send_sem=send_sems.at[j],
                recv_sem=recv_sems.at[j], device_id=(px, py, c), device_id_type=MESH))
        for cp in remote:
            cp.start()
        for cp in remote:
            cp.wait_recv()
        for cp in remote:
            cp.wait_send()

    out = pl.pallas_call(
        body, name=name, in_specs=[ANY], out_specs=ANY,
        out_shape=jax.ShapeDtypeStruct(parts.shape, parts.dtype),
        scratch_shapes=[pltpu.SemaphoreType.DMA((3,)), pltpu.SemaphoreType.DMA((3,))],
    )(parts)
    me = 2 * lax.axis_index("x") + lax.axis_index("y")
    return [jnp.where(me == j, parts[j], out[j]) for j in range(4)]


_SHARDED = [("a_norm_pre", 1), ("a_norm_post", 1), ("a_d_skip", 1), ("a_b_glu", 1), ("a_w_in", 2), ("a_w_glu", 1),
            ("a_w_out", 1), ("w_kv", 1), ("b_w_in", 2), ("b_w_out", 1), ("ple_w_proj", 2), ("ple_w_gate", 1)]
_REPLICATED = ["a_lam_re", "a_lam_im", "a_log_dt", "a_b_re", "a_b_im", "a_c_re", "a_c_im", "kv_norm", "b_norm_pre",
               "b_norm_post"]
_WEIGHT_ORDER = ["a_norm_pre", "a_norm_post", "a_w_in", "a_lam_re", "a_lam_im", "a_log_dt", "a_b_re", "a_b_im",
                 "a_c_re", "a_c_im", "a_d_skip", "a_w_glu", "a_b_glu", "a_w_out", "kv_norm", "w_kv", "b_norm_pre",
                 "b_norm_post", "b_w_in", "b_w_out", "ple_w_proj", "ple_w_gate"]


def _round_up(n, m):
    return (n + m - 1) // m * m


def _pack(pieces, rows):
    flat = jnp.concatenate([p.reshape(-1).astype(F32) for p in pieces])
    flat = jnp.pad(flat, (0, rows * SLAB_W - flat.shape[0]))
    return flat.reshape(rows, SLAB_W)


def _unpack(slab, shapes):
    flat = slab.reshape(-1)
    out, off = [], 0
    for s in shapes:
        n = math.prod(s)
        out.append(flat[off:off + n].reshape(s))
        off += n
    return out


def _pick_tile(n, cap):
    return max(t for t in range(SUBLANES, cap + 1, SUBLANES) if n % t == 0)


def _size_rows(shapes, mult):
    return _round_up(_round_up(sum(math.prod(s) for s in shapes), SLAB_W) // SLAB_W, mult)


def _adamw_tile(w, g, m, v):
    m2 = ADAM_B1 * m + (1.0 - ADAM_B1) * g
    v2 = ADAM_B2 * v + (1.0 - ADAM_B2) * (g * g)
    m_hat = m2 / (1.0 - ADAM_B1 ** ADAM_STEP)
    v_hat = v2 / (1.0 - ADAM_B2 ** ADAM_STEP)
    delta = -ADAM_LR * (m_hat / (jnp.sqrt(v_hat) + ADAM_EPS) + ADAM_WD * w)
    return delta, m2, v2


def _forward_backward(x0, p0, p1, tgt, W):
    seq = x0.shape[0]
    sw = D_MODEL
    g = {}

    h1, = _ew(lambda xt, gn: (_rms(xt, gn),), [x0], [W["a_norm_pre"]], [MXU_DTYPE], 0, name="a_pre_norm")
    w_u, w_ga = W["a_w_in"][:, :sw], W["a_w_in"][:, sw:]
    u = _mm(h1, w_u, out_dtype=MXU_DTYPE, name="a_in_u")
    gate = _mm(h1, w_ga, out_dtype=MXU_DTYPE, name="a_in_gate")
    a_re, a_im, bbt_re, bbt_im, pw_re, pw_im = _s5_params_fwd(W["lam_re"], W["lam_im"], W["log_dt"], W["bt_re"],
                                                               W["bt_im"])
    nb, gpb = N_GROUP_BLOCKS, GROUPS_PER_BLOCK
    bbd = jnp.concatenate([_block_diag(bbt_re.reshape(nb, gpb, GROUP_SIZE, STATE)),
                           _block_diag(bbt_im.reshape(nb, gpb, GROUP_SIZE, STATE))], axis=2).astype(MXU_DTYPE)
    crt = W["c_re"].transpose(0, 2, 1).reshape(nb, gpb, STATE, GROUP_SIZE)
    cit = W["c_im"].transpose(0, 2, 1).reshape(nb, gpb, STATE, GROUP_SIZE)
    cbd = jnp.concatenate([_block_diag(crt), -_block_diag(cit)], axis=1).astype(MXU_DTYPE)
    cbd_t = cbd.transpose(0, 2, 1)
    tab_f, tab_b = _scan_tables(pw_re.reshape(SUBLANES, nb, SSM_LANES), pw_im.reshape(SUBLANES, nb, SSM_LANES))
    ys, hs = _s5_fwd(u, bbd, cbd, tab_f)
    gl, = _ew(lambda a, b, d: (_gelu_skip(a, b, d),), [ys, u], [W["a_d_skip"]], [MXU_DTYPE], 0, name="a_gelu")
    t = _mm(gl, W["a_w_glu"], out_dtype=MXU_DTYPE, name="a_glu")
    y3, = _ew(lambda a, b, c, d: (_glu_gate(a, b, c, d),), [gl, t, gate], [W["a_b_glu"]], [MXU_DTYPE], 0, name="a_gate")
    y4 = _mm(y3, W["a_w_out"], out_dtype=MXU_DTYPE, name="a_out")
    x1, = _ew(lambda a, b, gn: (a + _rms(b, gn),), [x0, y4], [W["a_norm_post"]], [F32], 0, name="a_post_norm")
    gt0 = _mm(x1, W["ple_w_gate"][0], out_dtype=MXU_DTYPE, name="ple0_gate")
    pp0 = _mm(p0, W["ple_w_proj"][0], out_dtype=MXU_DTYPE, name="ple0_proj")
    x2, hk, h2 = _ew(lambda a, b, c, g1, g2: (_ple(a, b, c), _rms(_ple(a, b, c), g1), _rms(_ple(a, b, c), g2)),
                     [x1, gt0, pp0], [W["kv_norm"], W["b_norm_pre"]], [F32, MXU_DTYPE, MXU_DTYPE], 0, name="ple0_mix")

    w_k, w_v = W["w_kv"][:, :sw], W["w_kv"][:, sw:]
    k = _mm(hk, w_k, out_dtype=MXU_DTYPE, name="kv_k")
    v = _mm(hk, w_v, out_dtype=MXU_DTYPE, name="kv_v")
    w_q, w_gb = W["b_w_in"][:, :sw], W["b_w_in"][:, sw:]
    q = _mm(h2, w_q, out_dtype=MXU_DTYPE, name="b_in_q")
    gate2 = _mm(h2, w_gb, out_dtype=MXU_DTYPE, name="b_in_gate")
    o, tot = _sba_fwd(q, k, v)
    y5in, = _ew(lambda a, b: (_ogate(a, b),), [o, gate2], [], [MXU_DTYPE], 0, name="b_gate")
    y5 = _mm(y5in, W["b_w_out"], out_dtype=MXU_DTYPE, name="b_out")
    x3, = _ew(lambda a, b, gn: (a + _rms(b, gn),), [x2, y5], [W["b_norm_post"]], [F32], 0, name="b_post_norm")
    gt1 = _mm(x3, W["ple_w_gate"][1], out_dtype=MXU_DTYPE, name="ple1_gate")
    pp1 = _mm(p1, W["ple_w_proj"][1], out_dtype=MXU_DTYPE, name="ple1_proj")

    def loss_fn(xt, gt, pp, tg):
        s = jax.nn.sigmoid(gt)
        d = (xt + s * pp - tg) * (1.0 / D_MODEL)
        lsum = jnp.sum(d * d, axis=(0, 1), keepdims=True) * (0.5 * D_MODEL)
        return d, d * pp * s * (1.0 - s), d * s, lsum

    dx3a, dgt1, dpp1, loss = _ew(loss_fn, [x3, gt1, pp1, tgt], [], [F32, MXU_DTYPE, MXU_DTYPE], 1, name="loss_head")
    g_gate1 = _mm(x3, dgt1, ta=True, name="ple1_gate_dw")
    g_proj1 = _mm(p1, dpp1, ta=True, name="ple1_proj_dw")
    dx3 = _mm(dgt1, W["ple_w_gate"][1], tb=True, add=dx3a, name="ple1_gate_dx")

    def post_norm_bwd(ct, yt, gn):
        _, vjp = jax.vjp(_rms, yt, gn)
        dy, dg = vjp(ct)
        return dy, dg

    dy5, g["b_norm_post"] = _ew(post_norm_bwd, [dx3, y5], [W["b_norm_post"]], [MXU_DTYPE], 1, name="b_post_norm_bwd")
    g["b_w_out"] = _mm(y5in, dy5, ta=True, name="b_out_dw")
    dy5in = _mm(dy5, W["b_w_out"], tb=True, out_dtype=MXU_DTYPE, name="b_out_dx")

    def ogate_bwd(ct, ot, gt):
        _, vjp = jax.vjp(_ogate, ot, gt)
        return vjp(ct)

    do, dgate2 = _ew(ogate_bwd, [dy5in, o, gate2], [], [MXU_DTYPE, MXU_DTYPE], 0, name="b_gate_bwd")
    dq, dk, dv = _sba_bwd(q, k, v, do, tot)
    g["b_w_in"] = jnp.concatenate([_mm(h2, dq, ta=True, name="b_in_q_dw"),
                                   _mm(h2, dgate2, ta=True, name="b_in_gate_dw")], axis=1)
    dh2 = _mm(dgate2, w_gb, tb=True, add=_mm(dq, w_q, tb=True, name="b_in_q_dx"), out_dtype=MXU_DTYPE,
              name="b_in_gate_dx")
    g["w_kv"] = jnp.concatenate([_mm(hk, dk, ta=True, name="kv_k_dw"), _mm(hk, dv, ta=True, name="kv_v_dw")], axis=1)
    dhk = _mm(dv, w_v, tb=True, add=_mm(dk, w_k, tb=True, name="kv_k_dx"), out_dtype=MXU_DTYPE, name="kv_v_dx")

    def mix_bwd(ct, c2, ck, xt, gt, pp, g1, g2):
        x2v, vjp_ple = jax.vjp(_ple, xt, gt, pp)
        _, vjp_k = jax.vjp(_rms, x2v, g1)
        _, vjp_b = jax.vjp(_rms, x2v, g2)
        dxk, dg1 = vjp_k(ck)
        dxb, dg2 = vjp_b(c2)
        dx2 = ct + dxk + dxb
        dx1, dgt, dpp = vjp_ple(dx2)
        return dx1, dgt, dpp, dg1, dg2

    dx1a, dgt0, dpp0, g["kv_norm"], g["b_norm_pre"] = _ew(
        mix_bwd, [dx3, dh2, dhk, x1, gt0, pp0], [W["kv_norm"], W["b_norm_pre"]], [F32, MXU_DTYPE, MXU_DTYPE], 2,
        name="ple0_mix_bwd")
    g_gate0 = _mm(x1, dgt0, ta=True, name="ple0_gate_dw")
    g_proj0 = _mm(p0, dpp0, ta=True, name="ple0_proj_dw")
    dx1 = _mm(dgt0, W["ple_w_gate"][0], tb=True, add=dx1a, name="ple0_gate_dx")
    g["ple_w_gate"] = jnp.stack([g_gate0, g_gate1])
    g["ple_w_proj"] = jnp.stack([g_proj0, g_proj1])

    dy4, g["a_norm_post"] = _ew(post_norm_bwd, [dx1, y4], [W["a_norm_post"]], [MXU_DTYPE], 1, name="a_post_norm_bwd")
    g["a_w_out"] = _mm(y3, dy4, ta=True, name="a_out_dw")
    dy3 = _mm(dy4, W["a_w_out"], tb=True, out_dtype=MXU_DTYPE, name="a_out_dx")

    def gate_bwd(ct, gt_, tt_, gat, bg):
        _, vjp = jax.vjp(_glu_gate, gt_, tt_, gat, bg)
        dg_, dt_, dgate_, dbg = vjp(ct)
        return dg_, dt_, dgate_, dbg

    dgl_a, dt, dgate, g["a_b_glu"] = _ew(gate_bwd, [dy3, gl, t, gate], [W["a_b_glu"]], [F32, MXU_DTYPE, MXU_DTYPE], 1,
                                         name="a_gate_bwd")
    g["a_w_glu"] = _mm(gl, dt, ta=True, name="a_glu_dw")
    dgl = _mm(dt, W["a_w_glu"], tb=True, add=dgl_a, name="a_glu_dx")

    def gelu_bwd(ct, yt, ut, ds):
        _, vjp = jax.vjp(_gelu_skip, yt, ut, ds)
        return vjp(ct)

    dys, du_a, g["a_d_skip"] = _ew(gelu_bwd, [dgl, ys, u], [W["a_d_skip"]], [MXU_DTYPE, F32], 1, name="a_gelu_bwd")
    du, d_bbd, d_cbd, d_ar8, d_ai8 = _s5_bwd(u, dys, du_a, hs, bbd, cbd_t, tab_f, tab_b)
    g["a_w_in"] = jnp.concatenate([_mm(h1, du, ta=True, name="a_in_u_dw"),
                                   _mm(h1, dgate, ta=True, name="a_in_gate_dw")], axis=1)
    dh1 = _mm(dgate, w_ga, tb=True, add=_mm(du, w_u, tb=True, name="a_in_u_dx"), out_dtype=MXU_DTYPE,
              name="a_in_gate_dx")

    def pre_norm_bwd(ct, ch, xt, gn):
        _, vjp = jax.vjp(_rms, xt, gn)
        dx, dg = vjp(ch)
        return ct + dx, dg

    dx0, g["a_norm_pre"] = _ew(pre_norm_bwd, [dx1, dh1, x0], [W["a_norm_pre"]], [F32], 1, name="a_pre_norm_bwd")

    w = SSM_LANES
    d_bbt_re = _block_diag_extract(d_bbd[:, :, :w], GROUP_SIZE, STATE).reshape(N_GROUPS, GROUP_SIZE, STATE)
    d_bbt_im = _block_diag_extract(d_bbd[:, :, w:], GROUP_SIZE, STATE).reshape(N_GROUPS, GROUP_SIZE, STATE)
    d_crt = _block_diag_extract(d_cbd[:, :w, :], STATE, GROUP_SIZE).reshape(N_GROUPS, STATE, GROUP_SIZE)
    d_cit = -_block_diag_extract(d_cbd[:, w:, :], STATE, GROUP_SIZE).reshape(N_GROUPS, STATE, GROUP_SIZE)
    g["a_c_re"] = d_crt.transpose(0, 2, 1)[None]
    g["a_c_im"] = d_cit.transpose(0, 2, 1)[None]
    d_lr, d_li, d_ldt, d_btr, d_bti = _s5_params_bwd(
        W["lam_re"], W["lam_im"], W["log_dt"], W["bt_re"], W["bt_im"], _per_sublane(d_ar8), _per_sublane(d_ai8),
        d_bbt_re, d_bbt_im)
    g["a_lam_re"], g["a_lam_im"] = d_lr[None], d_li[None]
    g["a_log_dt"] = d_ldt.reshape(1, N_GROUPS)
    g["a_b_re"] = d_btr.transpose(0, 2, 1)[None]
    g["a_b_im"] = d_bti.transpose(0, 2, 1)[None]
    return loss, dx0, g


def kernel(x, p, a_norm_pre, a_norm_post, a_w_in, a_lam_re, a_lam_im, a_log_dt, a_b_re, a_b_im, a_c_re, a_c_im, a_d_skip, a_w_glu, a_b_glu, a_w_out, kv_norm, w_kv, b_norm_pre, b_norm_post, b_w_in, b_w_out, ple_w_proj, ple_w_gate, loss_target, m_a_norm_pre, m_a_norm_post, m_a_w_in, m_a_lam_re, m_a_lam_im, m_a_log_dt, m_a_b_re, m_a_b_im, m_a_c_re, m_a_c_im, m_a_d_skip, m_a_w_glu, m_a_b_glu, m_a_w_out, m_kv_norm, m_w_kv, m_b_norm_pre, m_b_norm_post, m_b_w_in, m_b_w_out, m_ple_w_proj, m_ple_w_gate, v_a_norm_pre, v_a_norm_post, v_a_w_in, v_a_lam_re, v_a_lam_im, v_a_log_dt, v_a_b_re, v_a_b_im, v_a_c_re, v_a_c_im, v_a_d_skip, v_a_w_glu, v_a_b_glu, v_a_w_out, v_kv_norm, v_w_kv, v_b_norm_pre, v_b_norm_post, v_b_w_in, v_b_w_out, v_ple_w_proj, v_ple_w_gate):
    loc = dict(a_norm_pre=a_norm_pre, a_norm_post=a_norm_post, a_w_in=a_w_in, a_lam_re=a_lam_re, a_lam_im=a_lam_im,
               a_log_dt=a_log_dt, a_b_re=a_b_re, a_b_im=a_b_im, a_c_re=a_c_re, a_c_im=a_c_im, a_d_skip=a_d_skip,
               a_w_glu=a_w_glu, a_b_glu=a_b_glu, a_w_out=a_w_out, kv_norm=kv_norm, w_kv=w_kv, b_norm_pre=b_norm_pre,
               b_norm_post=b_norm_post, b_w_in=b_w_in, b_w_out=b_w_out, ple_w_proj=ple_w_proj, ple_w_gate=ple_w_gate)
    mom = dict(a_norm_pre=m_a_norm_pre, a_norm_post=m_a_norm_post, a_w_in=m_a_w_in, a_lam_re=m_a_lam_re,
               a_lam_im=m_a_lam_im, a_log_dt=m_a_log_dt, a_b_re=m_a_b_re, a_b_im=m_a_b_im, a_c_re=m_a_c_re,
               a_c_im=m_a_c_im, a_d_skip=m_a_d_skip, a_w_glu=m_a_w_glu, a_b_glu=m_a_b_glu, a_w_out=m_a_w_out,
               kv_norm=m_kv_norm, w_kv=m_w_kv, b_norm_pre=m_b_norm_pre, b_norm_post=m_b_norm_post, b_w_in=m_b_w_in,
               b_w_out=m_b_w_out, ple_w_proj=m_ple_w_proj, ple_w_gate=m_ple_w_gate)
    var = dict(a_norm_pre=v_a_norm_pre, a_norm_post=v_a_norm_post, a_w_in=v_a_w_in, a_lam_re=v_a_lam_re,
               a_lam_im=v_a_lam_im, a_log_dt=v_a_log_dt, a_b_re=v_a_b_re, a_b_im=v_a_b_im, a_c_re=v_a_c_re,
               a_c_im=v_a_c_im, a_d_skip=v_a_d_skip, a_w_glu=v_a_w_glu, a_b_glu=v_a_b_glu, a_w_out=v_a_w_out,
               kv_norm=v_kv_norm, w_kv=v_w_kv, b_norm_pre=v_b_norm_pre, b_norm_post=v_b_norm_post, b_w_in=v_b_w_in,
               b_w_out=v_b_w_out, ple_w_proj=v_ple_w_proj, ple_w_gate=v_ple_w_gate)
    vec_names = [n for n, _ in _SHARDED if loc[n].shape[-2] == 1 and loc[n].ndim == 2]
    mat_names = [n for n, _ in _SHARDED if n not in vec_names]
    axis_of = dict(_SHARDED)
    mat_shapes = [loc[n].shape for n in mat_names]
    vec_shapes = [loc[n].shape for n in vec_names]
    mat_rows = _size_rows(mat_shapes, 32)
    vec_rows = _size_rows(vec_shapes, 16)
    mat_slab = _pack([loc[n] for n in mat_names], mat_rows).astype(MXU_DTYPE)
    vec_slab = _pack([loc[n] for n in vec_names], vec_rows)
    mat_all, vec_all = _all_gather_chips_two_level([mat_slab, vec_slab], name="gather_weights")
    W = {}
    for names, slab_all, shapes in ((mat_names, mat_all, mat_shapes), (vec_names, vec_all, vec_shapes)):
        per_chip = [_unpack(slab_all[j], shapes) for j in range(4)]
        for i, n in enumerate(names):
            W[n] = jnp.concatenate([per_chip[j][i] for j in range(4)], axis=axis_of[n])
    for n in ("a_w_in", "a_w_glu", "a_w_out", "b_w_in", "b_w_out"):
        W[n] = W[n][0]
    W["kv_norm"] = kv_norm.reshape(1, D_MODEL)
    W["b_norm_pre"], W["b_norm_post"] = b_norm_pre, b_norm_post
    W["lam_re"], W["lam_im"] = a_lam_re[0], a_lam_im[0]
    W["log_dt"] = a_log_dt.reshape(N_GROUPS, 1)
    W["bt_re"], W["bt_im"] = a_b_re[0].transpose(0, 2, 1), a_b_im[0].transpose(0, 2, 1)
    W["c_re"], W["c_im"] = a_c_re[0], a_c_im[0]

    loss, dx0, g = _forward_backward(x[0], p[0, 0], p[1, 0], loss_target[0], W)
    for n in ("a_w_in", "a_w_glu", "a_w_out", "b_w_in", "b_w_out"):
        g[n] = g[n][None]
    g["kv_norm"] = g["kv_norm"].reshape(D_MODEL)
    loss = lax.psum(loss[0, 0], ("x", "y", "c"))

    rep_shapes = [loc[n].shape for n in _REPLICATED]
    rep_rows = _size_rows(rep_shapes, 32)
    rep_slab = _pack([g[n] for n in _REPLICATED], rep_rows).reshape(4, rep_rows // 4, SLAB_W)
    sh_names = [n for n, _ in _SHARDED]
    sh_shapes = [loc[n].shape for n in sh_names]
    sh_rows = _size_rows(sh_shapes, 8)
    rows = _round_up(sh_rows + rep_rows // 4, 1024)
    half = rows // 2
    slabs = []
    for j in range(4):
        pieces = [jnp.split(g[n], 4, axis=axis_of[n])[j] for n in sh_names]
        sh = _pack(pieces, sh_rows)
        slabs.append(jnp.concatenate([sh, rep_slab[j], jnp.zeros((rows - sh_rows - rep_rows // 4, SLAB_W), F32)]))
    ci = lax.axis_index("c")
    mine = jnp.concatenate([lax.dynamic_slice_in_dim(s, ci * half, half, 0) for s in slabs])
    theirs = jnp.concatenate([lax.dynamic_slice_in_dim(s, (1 - ci) * half, half, 0) for s in slabs])
    got = _sibling_exchange(theirs, name="reduce_cores")
    part, = _ew(lambda a, b: (a + b,), [mine, got], [], [MXU_DTYPE], 0, name="reduce_cores_add", tr=512)
    recv = _all_to_all_chips(part.reshape(4, half, SLAB_W), name="reduce_chips")
    tot_half, = _ew(lambda a, b, c, d: (((a + b) + c) + d,), [recv[0], recv[1], recv[2], recv[3]], [], [F32], 0,
                    name="reduce_chips_add", tr=512)
    other_half = _sibling_exchange(tot_half, name="share_cores")
    lo_half = jnp.where(ci == 0, tot_half, other_half)
    hi_half = jnp.where(ci == 0, other_half, tot_half)
    gsum = jnp.concatenate([lo_half, hi_half])
    rep_q = gsum[sh_rows:sh_rows + rep_rows // 4]
    rep_all, = _all_gather_chips([rep_q], name="gather_replicated")
    gsum_full = jnp.concatenate([gsum[:sh_rows], rep_all.reshape(rep_rows, SLAB_W)])

    all_names = sh_names + _REPLICATED
    all_shapes = sh_shapes + rep_shapes

    def slab_of(d):
        return jnp.concatenate([_pack([d[n] for n in sh_names], sh_rows), _pack([d[n] for n in _REPLICATED], rep_rows)])

    n_rows = sh_rows + rep_rows
    outs = _ew(_adamw_tile, [slab_of(loc), gsum_full, slab_of(mom), slab_of(var)], [], [F32] * 3, 0, name="adamw",
               tr=_pick_tile(n_rows, 512))
    res = []
    for slab in [gsum_full] + list(outs):
        parts = _unpack(slab[:sh_rows], sh_shapes) + _unpack(slab[sh_rows:], rep_shapes)
        res.append(dict(zip(all_names, parts)))
    out = [loss, dx0[None]]
    for d in res:
        out += [d[n] for n in _WEIGHT_ORDER]
    return tuple(out)
```

```python
import functools
import math

import jax
import jax.numpy as jnp
from jax import lax
from jax.experimental import pallas as pl
from jax.experimental.pallas import tpu as pltpu

F32 = jnp.float32
MXU_DTYPE = jnp.bfloat16

D_MODEL = 1024
N_GROUPS = 64
GROUP_SIZE = 16
STATE = 64
GROUPS_PER_BLOCK = 8
SSM_LANES = GROUPS_PER_BLOCK * STATE
N_GROUP_BLOCKS = N_GROUPS // GROUPS_PER_BLOCK
HEAD_DIM = 64
HEADS_PER_BLOCK = 2
BWD_QUERY_BLOCK = 256
KEY_BLOCKS_PER_STEP = 4
FWD_QUERY_BLOCK = 512
FWD_KEY_BLOCKS_PER_STEP = 4
CUMSUM_CHUNK = 256
SCAN_TILES_PER_STEP = 2
EPS = 1e-6
SLAB_W = 512
SUBLANES = 8
VMEM_LIMIT = 56 * 1024 * 1024

ADAM_LR = 0.001
ADAM_B1 = 0.9
ADAM_B2 = 0.999
ADAM_EPS = 1e-08
ADAM_WD = 0.01
ADAM_STEP = 10

MESH = pl.DeviceIdType.MESH
ANY = pl.BlockSpec(memory_space=pl.ANY)


def _cparams(sem):
    return pltpu.CompilerParams(dimension_semantics=sem, vmem_limit_bytes=VMEM_LIMIT)


def _mm(a, b, *, ta=False, tb=False, add=None, out_dtype=F32, name, tm=1024, tn=1024, tk=1024):
    if ta:
        kdim, m = a.shape
    else:
        m, kdim = a.shape
    if tb:
        n, kb = b.shape
    else:
        kb, n = b.shape
    assert kdim == kb, (a.shape, b.shape, ta, tb)
    tm, tn, tk = min(tm, m), min(tn, n), min(tk, kdim)
    assert m % tm == 0 and n % tn == 0 and kdim % tk == 0
    nk = kdim // tk
    dims = (((0 if ta else 1,), (1 if tb else 0,)), ((), ()))
    has_add = add is not None

    def body(*refs):
        if has_add:
            a_ref, b_ref, add_ref, o_ref, acc_ref = refs
        else:
            a_ref, b_ref, o_ref, acc_ref = refs
        k = pl.program_id(2)

        @pl.when(k == 0)
        def _():
            acc_ref[...] = jnp.zeros_like(acc_ref)

        acc_ref[...] += lax.dot_general(a_ref[...].astype(MXU_DTYPE), b_ref[...].astype(MXU_DTYPE), dims,
                                        preferred_element_type=F32)

        @pl.when(k == nk - 1)
        def _():
            r = acc_ref[...]
            if has_add:
                r = r + add_ref[...].astype(F32)
            o_ref[...] = r.astype(o_ref.dtype)

    a_spec = pl.BlockSpec((tk, tm), lambda i, j, k: (k, i)) if ta else pl.BlockSpec((tm, tk), lambda i, j, k: (i, k))
    b_spec = pl.BlockSpec((tn, tk), lambda i, j, k: (j, k)) if tb else pl.BlockSpec((tk, tn), lambda i, j, k: (k, j))
    o_spec = pl.BlockSpec((tm, tn), lambda i, j, k: (i, j))
    in_specs = [a_spec, b_spec] + ([o_spec] if has_add else [])
    args = (a, b) + ((add,) if has_add else ())
    return pl.pallas_call(
        body, name=name, grid=(m // tm, n // tn, nk), in_specs=in_specs, out_specs=o_spec,
        out_shape=jax.ShapeDtypeStruct((m, n), out_dtype), scratch_shapes=[pltpu.VMEM((tm, tn), F32)],
        compiler_params=_cparams(("parallel", "parallel", "arbitrary")),
    )(*args)


def _ew(fn, rows, params, row_dtypes, n_acc, *, name, tr=256):
    nrows = rows[0].shape[0]
    tr = min(tr, nrows)
    assert nrows % tr == 0
    n_in, n_par, n_row = len(rows), len(params), len(row_dtypes)
    tile_avals = [jax.ShapeDtypeStruct((tr, r.shape[1]), F32) for r in rows]
    par_avals = [jax.ShapeDtypeStruct(q.shape, F32) for q in params]
    out_avals = jax.eval_shape(fn, *tile_avals, *par_avals)
    assert len(out_avals) == n_row + n_acc

    def body(*refs):
        ins = [r[...].astype(F32) for r in refs[:n_in + n_par]]
        outs = fn(*ins)
        row_refs = refs[n_in + n_par:n_in + n_par + n_row]
        acc_refs = refs[n_in + n_par + n_row:]
        for r, o in zip(row_refs, outs[:n_row]):
            r[...] = o.astype(r.dtype)
        if n_acc:
            @pl.when(pl.program_id(0) == 0)
            def _():
                for r in acc_refs:
                    r[...] = jnp.zeros_like(r)
            for r, o in zip(acc_refs, outs[n_row:]):
                r[...] += o

    in_specs = [pl.BlockSpec((tr, r.shape[1]), lambda i: (i, 0)) for r in rows]
    in_specs += [pl.BlockSpec(q.shape, lambda i: (0, 0)) for q in params]
    out_specs = [pl.BlockSpec((tr, o.shape[1]), lambda i: (i, 0)) for o in out_avals[:n_row]]
    out_specs += [pl.BlockSpec(o.shape, lambda i: (0, 0)) for o in out_avals[n_row:]]
    out_shape = [jax.ShapeDtypeStruct((nrows, o.shape[1]), dt) for o, dt in zip(out_avals[:n_row], row_dtypes)]
    out_shape += [jax.ShapeDtypeStruct(o.shape, F32) for o in out_avals[n_row:]]
    return pl.pallas_call(
        body, name=name, grid=(nrows // tr,), in_specs=in_specs, out_specs=out_specs, out_shape=out_shape,
        compiler_params=_cparams(("arbitrary",)),
    )(*rows, *params)


def _rms(xt, g):
    r = lax.rsqrt(jnp.mean(xt * xt, axis=-1, keepdims=True) + EPS)
    return xt * r * g


def _gelu_skip(ys, u, dskip):
    return jax.nn.gelu(ys + dskip * u)


def _glu_gate(g, t, gate, bglu):
    return g * jax.nn.sigmoid(t + bglu) * jax.nn.silu(gate)


def _ogate(o, gate):
    return o * jax.nn.silu(gate)


def _ple(xt, gt, pp):
    return xt + jax.nn.sigmoid(gt) * pp


def _s5_disc(lam_re, lam_im, log_dt, bt_re, bt_im):
    lr = jnp.minimum(lam_re, -1e-4)
    li = lam_im
    dt = jnp.exp(log_dt)
    mag = jnp.exp(lr * dt)
    a_re = mag * jnp.cos(li * dt)
    a_im = mag * jnp.sin(li * dt)
    den = lr * lr + li * li
    nr = a_re - 1.0
    f_re = (nr * lr + a_im * li) / den
    f_im = (a_im * lr - nr * li) / den
    bb_re = f_re[:, None, :] * bt_re - f_im[:, None, :] * bt_im
    bb_im = f_re[:, None, :] * bt_im + f_im[:, None, :] * bt_re
    return a_re, a_im, bb_re, bb_im


def _s5_params_fwd(lam_re, lam_im, log_dt, bt_re, bt_im):
    def body(lr_ref, li_ref, dt_ref, br_ref, bi_ref, ar_ref, ai_ref, bbr_ref, bbi_ref, pr_ref, pi_ref):
        a_re, a_im, bb_re, bb_im = _s5_disc(lr_ref[...], li_ref[...], dt_ref[...], br_ref[...], bi_ref[...])
        ar_ref[...] = a_re
        ai_ref[...] = a_im
        bbr_ref[...] = bb_re
        bbi_ref[...] = bb_im
        pr, pi = a_re, a_im
        for t in range(SUBLANES):
            pr_ref[t] = pr
            pi_ref[t] = pi
            pr, pi = pr * a_re - pi * a_im, pr * a_im + pi * a_re

    g, s = lam_re.shape
    sd = jax.ShapeDtypeStruct
    return pl.pallas_call(
        body, name="s5_params_fwd",
        out_shape=[sd((g, s), F32), sd((g, s), F32), sd(bt_re.shape, F32), sd(bt_re.shape, F32),
                   sd((SUBLANES, g, s), F32), sd((SUBLANES, g, s), F32)],
    )(lam_re, lam_im, log_dt, bt_re, bt_im)


def _s5_params_bwd(lam_re, lam_im, log_dt, bt_re, bt_im, d_ar, d_ai, d_bbr, d_bbi):
    def body(lr_ref, li_ref, dt_ref, br_ref, bi_ref, c0, c1, c2, c3, o0, o1, o2, o3, o4):
        _, vjp = jax.vjp(_s5_disc, lr_ref[...], li_ref[...], dt_ref[...], br_ref[...], bi_ref[...])
        grads = vjp((jnp.sum(c0[...], axis=0), jnp.sum(c1[...], axis=0), c2[...], c3[...]))
        for o, gval in zip((o0, o1, o2, o3, o4), grads):
            o[...] = gval

    sd = jax.ShapeDtypeStruct
    return pl.pallas_call(
        body, name="s5_params_bwd",
        out_shape=[sd(lam_re.shape, F32), sd(lam_im.shape, F32), sd(log_dt.shape, F32), sd(bt_re.shape, F32),
                   sd(bt_im.shape, F32)],
    )(lam_re, lam_im, log_dt, bt_re, bt_im, d_ar, d_ai, d_bbr, d_bbi)


def _block_diag(t):
    nb, g, r, c = t.shape
    eye = jnp.eye(g, dtype=t.dtype)
    return jnp.einsum("ngrc,gh->ngrhc", t, eye).reshape(nb, g * r, g * c)


def _block_diag_extract(m, r, c):
    nb = m.shape[0]
    g = GROUPS_PER_BLOCK
    eye = jnp.eye(g, dtype=m.dtype)[None, :, None, :, None]
    return jnp.sum(m.reshape(nb, g, r, g, c) * eye, axis=3)


def _per_sublane(acc):
    return acc.reshape(N_GROUP_BLOCKS, SUBLANES, GROUPS_PER_BLOCK, STATE).transpose(1, 0, 2, 3).reshape(
        SUBLANES, N_GROUPS, STATE)


def _scan_tables(pr, pi):
    row = jnp.arange(SUBLANES)[:, None, None]

    def tables(im_sign, fwd):
        kinds = []
        for k in (1, 2, 4):
            mask = (row >= k) if fwd else (row < SUBLANES - k)
            kinds.append(jnp.where(mask, pr[k - 1][None], 0.0))
            kinds.append(jnp.where(mask, im_sign * pi[k - 1][None], 0.0))
        if fwd:
            kinds += [pr, im_sign * pi]
        else:
            kinds += [pr[::-1], im_sign * pi[::-1]]
        return jnp.stack(kinds, axis=0).transpose(2, 0, 1, 3)

    return tables(1.0, True), tables(-1.0, False)


def _scan_rows(buf_ref, tab_ref, carry_r, carry_i, n_tiles, reverse):
    w = SSM_LANES
    assert n_tiles % SCAN_TILES_PER_STEP == 0

    def local_scan(jj):
        rows = pl.ds(pl.multiple_of(jj * SUBLANES, SUBLANES), SUBLANES)
        zr = buf_ref[rows, 0:w]
        zi = buf_ref[rows, w:2 * w]
        for n, k in enumerate((1, 2, 4)):
            ar = tab_ref[2 * n]
            ai = tab_ref[2 * n + 1]
            sh = (SUBLANES - k) if reverse else k
            rr = pltpu.roll(zr, sh, 0)
            ri = pltpu.roll(zi, sh, 0)
            zr, zi = zr + ar * rr - ai * ri, zi + ar * ri + ai * rr
        return rows, zr, zi

    def step(j, carry):
        first = j * SCAN_TILES_PER_STEP
        tiles = [local_scan((n_tiles - 1 - first - d) if reverse else (first + d)) for d in range(SCAN_TILES_PER_STEP)]
        cr, ci = carry
        pr = tab_ref[6]
        pi = tab_ref[7]
        for rows, zr, zi in tiles:
            zr, zi = zr + pr * cr - pi * ci, zi + pr * ci + pi * cr
            buf_ref[rows, 0:w] = zr
            buf_ref[rows, w:2 * w] = zi
            cr, ci = (zr[0:1], zi[0:1]) if reverse else (zr[SUBLANES - 1:SUBLANES], zi[SUBLANES - 1:SUBLANES])
        return cr, ci

    return lax.fori_loop(0, n_tiles // SCAN_TILES_PER_STEP, step, (carry_r, carry_i))


def _s5_fwd(u, bbd, cbd, tab_f, *, tt=512):
    seq = u.shape[0]
    tt = min(tt, seq)
    nt = seq // tt
    w = SSM_LANES

    def body(u_ref, b_ref, c_ref, tab_ref, y_ref, hs_ref, buf_ref, h_ref):
        it = pl.program_id(1)

        @pl.when(it == 0)
        def _():
            h_ref[...] = jnp.zeros_like(h_ref)

        hs_ref[...] = h_ref[...]
        buf_ref[...] = jnp.dot(u_ref[...].astype(MXU_DTYPE), b_ref[...], preferred_element_type=F32)
        hr, hi = _scan_rows(buf_ref, tab_ref, h_ref[:, 0:w], h_ref[:, w:2 * w], tt // SUBLANES, False)
        h_ref[:, 0:w] = hr
        h_ref[:, w:2 * w] = hi
        y_ref[...] = jnp.dot(buf_ref[...].astype(MXU_DTYPE), c_ref[...], preferred_element_type=F32)

    return pl.pallas_call(
        body, name="s5_fwd", grid=(N_GROUP_BLOCKS, nt),
        in_specs=[pl.BlockSpec((tt, 128), lambda g, t: (t, g)),
                  pl.BlockSpec((None, 128, 2 * w), lambda g, t: (g, 0, 0)),
                  pl.BlockSpec((None, 2 * w, 128), lambda g, t: (g, 0, 0)),
                  pl.BlockSpec((None, 8, SUBLANES, w), lambda g, t: (g, 0, 0, 0))],
        out_specs=[pl.BlockSpec((tt, 128), lambda g, t: (t, g)),
                   pl.BlockSpec((None, None, 1, 2 * w), lambda g, t: (g, t, 0, 0))],
        out_shape=[jax.ShapeDtypeStruct((seq, D_MODEL), F32),
                   jax.ShapeDtypeStruct((N_GROUP_BLOCKS, nt, 1, 2 * w), F32)],
        scratch_shapes=[pltpu.VMEM((tt, 2 * w), F32), pltpu.VMEM((1, 2 * w), F32)],
        compiler_params=_cparams(("parallel", "arbitrary")),
    )(u, bbd, cbd, tab_f)


def _s5_bwd(u, dy, du_add, hs, bbd, cbd_t, tab_f, tab_b, *, tt=512):
    seq = u.shape[0]
    tt = min(tt, seq)
    nt = seq // tt
    w = SSM_LANES
    n_tiles = tt // SUBLANES

    def body(u_ref, dy_ref, dua_ref, hs_ref, b_ref, ct_ref, tabf_ref, tabb_ref,
             du_ref, db_ref, dc_ref, dar_ref, dai_ref, s_ref, l_ref, lam_ref):
        it = pl.program_id(1)

        @pl.when(it == 0)
        def _():
            lam_ref[...] = jnp.zeros_like(lam_ref)
            db_ref[...] = jnp.zeros_like(db_ref)
            dc_ref[...] = jnp.zeros_like(dc_ref)
            dar_ref[...] = jnp.zeros_like(dar_ref)
            dai_ref[...] = jnp.zeros_like(dai_ref)

        ub = u_ref[...].astype(MXU_DTYPE)
        dyb = dy_ref[...].astype(MXU_DTYPE)
        s_ref[...] = jnp.dot(ub, b_ref[...], preferred_element_type=F32)
        h_in_r = hs_ref[:, 0:w]
        h_in_i = hs_ref[:, w:2 * w]
        _scan_rows(s_ref, tabf_ref, h_in_r, h_in_i, n_tiles, False)
        dc_ref[...] += lax.dot_general(s_ref[...].astype(MXU_DTYPE), dyb, (((0,), (0,)), ((), ())),
                                       preferred_element_type=F32)
        l_ref[...] = jnp.dot(dyb, ct_ref[...], preferred_element_type=F32)
        lr, li = _scan_rows(l_ref, tabb_ref, lam_ref[:, 0:w], lam_ref[:, w:2 * w], n_tiles, True)
        lam_ref[:, 0:w] = lr
        lam_ref[:, w:2 * w] = li
        lb = l_ref[...].astype(MXU_DTYPE)
        db_ref[...] += lax.dot_general(ub, lb, (((0,), (0,)), ((), ())), preferred_element_type=F32)
        du = lax.dot_general(lb, b_ref[...], (((1,), (1,)), ((), ())), preferred_element_type=F32)
        du_ref[...] = (du + dua_ref[...].astype(F32)).astype(du_ref.dtype)

        def tile(j, carry):
            pr, pi, accr, acci = carry
            rows = pl.ds(pl.multiple_of(j * SUBLANES, SUBLANES), SUBLANES)
            sr = s_ref[rows, 0:w]
            si = s_ref[rows, w:2 * w]
            first = lax.broadcasted_iota(jnp.int32, (SUBLANES, w), 0) == 0
            sr_prev = jnp.where(first, pr, pltpu.roll(sr, 1, 0))
            si_prev = jnp.where(first, pi, pltpu.roll(si, 1, 0))
            gr = l_ref[rows, 0:w]
            gi = l_ref[rows, w:2 * w]
            accr = accr + gr * sr_prev + gi * si_prev
            acci = acci + gi * sr_prev - gr * si_prev
            return sr[SUBLANES - 1:SUBLANES], si[SUBLANES - 1:SUBLANES], accr, acci

        zero = jnp.zeros((SUBLANES, w), F32)
        _, _, accr, acci = lax.fori_loop(0, n_tiles, tile, (h_in_r, h_in_i, zero, zero))
        dar_ref[...] += accr
        dai_ref[...] += acci

    rev = lambda g, t: (nt - 1 - t, g)
    return pl.pallas_call(
        body, name="s5_bwd", grid=(N_GROUP_BLOCKS, nt),
        in_specs=[pl.BlockSpec((tt, 128), rev), pl.BlockSpec((tt, 128), rev), pl.BlockSpec((tt, 128), rev),
                  pl.BlockSpec((None, None, 1, 2 * w), lambda g, t: (g, nt - 1 - t, 0, 0)),
                  pl.BlockSpec((None, 128, 2 * w), lambda g, t: (g, 0, 0)),
                  pl.BlockSpec((None, 128, 2 * w), lambda g, t: (g, 0, 0)),
                  pl.BlockSpec((None, 8, SUBLANES, w), lambda g, t: (g, 0, 0, 0)),
                  pl.BlockSpec((None, 8, SUBLANES, w), lambda g, t: (g, 0, 0, 0))],
        out_specs=[pl.BlockSpec((tt, 128), rev),
                   pl.BlockSpec((None, 128, 2 * w), lambda g, t: (g, 0, 0)),
                   pl.BlockSpec((None, 2 * w, 128), lambda g, t: (g, 0, 0)),
                   pl.BlockSpec((None, SUBLANES, w), lambda g, t: (g, 0, 0)),
                   pl.BlockSpec((None, SUBLANES, w), lambda g, t: (g, 0, 0))],
        out_shape=[jax.ShapeDtypeStruct((seq, D_MODEL), MXU_DTYPE),
                   jax.ShapeDtypeStruct((N_GROUP_BLOCKS, 128, 2 * w), F32),
                   jax.ShapeDtypeStruct((N_GROUP_BLOCKS, 2 * w, 128), F32),
                   jax.ShapeDtypeStruct((N_GROUP_BLOCKS, SUBLANES, w), F32),
                   jax.ShapeDtypeStruct((N_GROUP_BLOCKS, SUBLANES, w), F32)],
        scratch_shapes=[pltpu.VMEM((tt, 2 * w), F32), pltpu.VMEM((tt, 2 * w), F32), pltpu.VMEM((1, 2 * w), F32)],
        compiler_params=_cparams(("parallel", "arbitrary")),
    )(u, dy, du_add, hs, bbd, cbd_t, tab_f, tab_b)


LOG2E = 1.4426950408889634
LN2 = 0.6931471805599453
SOFTPLUS2_LINEAR = 28.0


def _softplus2(z):
    u = 1.0 + jnp.exp2(z)
    return jnp.where(z > SOFTPLUS2_LINEAR, z, jnp.log2(u)), u


def _tri(n, keep, value=1.0):
    row = lax.broadcasted_iota(jnp.int32, (n, n), 0)
    col = lax.broadcasted_iota(jnp.int32, (n, n), 1)
    return jnp.where(keep(row, col), value, 0.0).astype(MXU_DTYPE)


def _suffix_sums(x, tri):
    c = tri.shape[0]
    outs, carry = [], None
    for i in reversed(range(x.shape[1] // c)):
        part = jnp.dot(x[:, i * c:(i + 1) * c].astype(MXU_DTYPE), tri, preferred_element_type=F32)
        if carry is not None:
            part = part + carry
        carry = part[:, 0:1]
        outs.append(part)
    return jnp.concatenate(outs[::-1], axis=1) if len(outs) > 1 else outs[0]


def _prefix_sums(x, tri, carry):
    c = tri.shape[0]
    outs = []
    for i in range(x.shape[1] // c):
        part = jnp.dot(x[:, i * c:(i + 1) * c].astype(MXU_DTYPE), tri, preferred_element_type=F32) + carry
        carry = part[:, c - 1:c]
        outs.append(part)
    return jnp.concatenate(outs, axis=1) if len(outs) > 1 else outs[0]


def _head_masks():
    lane = lax.broadcasted_iota(jnp.int32, (1, HEADS_PER_BLOCK * HEAD_DIM), 1)
    return [(lane >= h * HEAD_DIM) & (lane < (h + 1) * HEAD_DIM) for h in range(HEADS_PER_BLOCK)]


def _sba_fwd(q, k, v, *, tb=FWD_QUERY_BLOCK, key_blocks=FWD_KEY_BLOCKS_PER_STEP):
    seq = q.shape[0]
    tb = min(tb, seq)
    nq = seq // tb
    cw = HEADS_PER_BLOCK * HEAD_DIM
    scale = HEAD_DIM ** -0.5

    def body(q_ref, k_ref, v_ref, o_ref, t_ref, acc_ref, r_ref):
        qi = pl.program_id(1)
        masks = _head_masks()
        qf = q_ref[...].astype(F32) * (scale * LOG2E)
        q2 = jnp.concatenate([jnp.where(m, qf, 0.0) for m in masks], axis=0).astype(MXU_DTYPE)
        causal = lax.broadcasted_iota(jnp.int32, (tb, tb), 1) < lax.broadcasted_iota(jnp.int32, (tb, tb), 0)
        causal2 = jnp.concatenate([causal] * HEADS_PER_BLOCK, axis=0)
        neg_upper = _tri(min(tb, CUMSUM_CHUNK), lambda j, s: j >= s, -1.0)
        acc_ref[...] = jnp.zeros_like(acc_ref)
        r_ref[...] = jnp.zeros_like(r_ref)

        def block(start, width, diag):
            ks = pl.ds(pl.multiple_of(start, tb), width)
            kb = k_ref[ks, :].astype(MXU_DTYPE)
            vb = v_ref[ks, :]
            z = lax.dot_general(q2, kb, (((1,), (1,)), ((), ())), preferred_element_type=F32)
            sp, _ = _softplus2(z)
            if diag:
                sp = jnp.where(causal2, sp, 0.0)
            incl = _suffix_sums(sp, neg_upper)
            a = z + incl + r_ref[...]
            if diag:
                a = jnp.where(causal2, a, -1e30)
            wgt = jnp.exp2(a.astype(MXU_DTYPE))
            wcat =jnp.concatenate([wgt[h * tb:(h + 1) * tb] for h in range(HEADS_PER_BLOCK)], axis=1)
            vcat = jnp.concatenate([jnp.where(m, vb, jnp.zeros_like(vb)) for m in masks], axis=0).astype(MXU_DTYPE)
            acc_ref[...] += jnp.dot(wcat, vcat, preferred_element_type=F32)
            r_ref[...] += incl[:, 0:1]

        block(qi * tb, tb, True)
        n_wide = qi // key_blocks

        def single(i, c):
            block((qi - 1 - i) * tb, tb, False)
            return c

        lax.fori_loop(0, qi - n_wide * key_blocks, single, 0)

        def step(i, c):
            block((n_wide - 1 - i) * (key_blocks * tb), key_blocks * tb, False)
            return c

        lax.fori_loop(0, n_wide, step, 0)
        o_ref[...] = acc_ref[...].astype(o_ref.dtype)
        for h in range(HEADS_PER_BLOCK):
            t_ref[:, h:h + 1] = r_ref[h * tb:(h + 1) * tb]

    n_hp = D_MODEL // cw
    return pl.pallas_call(
        body, name="sba_fwd", grid=(n_hp, nq),
        in_specs=[pl.BlockSpec((tb, cw), lambda h, i: (i, h)),
                  pl.BlockSpec((seq, cw), lambda h, i: (0, h)),
                  pl.BlockSpec((seq, cw), lambda h, i: (0, h))],
        out_specs=[pl.BlockSpec((tb, cw), lambda h, i: (i, h)),
                   pl.BlockSpec((None, tb, HEADS_PER_BLOCK), lambda h, i: (h, i, 0))],
        out_shape=[jax.ShapeDtypeStruct((seq, D_MODEL), MXU_DTYPE),
                   jax.ShapeDtypeStruct((n_hp, seq, HEADS_PER_BLOCK), F32)],
        scratch_shapes=[pltpu.VMEM((tb, cw), F32), pltpu.VMEM((HEADS_PER_BLOCK * tb, 1), F32)],
        compiler_params=_cparams(("parallel", "arbitrary")),
    )(q, k, v)


def _sba_bwd(q, k, v, do, tot, *, tb=BWD_QUERY_BLOCK):
    seq = q.shape[0]
    tb = min(tb, seq)
    nq = seq // tb
    cw = HEADS_PER_BLOCK * HEAD_DIM
    scale = HEAD_DIM ** -0.5

    def body(q_ref, k_ref, v_ref, do_ref, t_ref, dq_ref, dk_ref, dv_ref, acc_ref, lp_ref, dp_ref):
        qi = pl.program_id(1)

        @pl.when(qi == 0)
        def _():
            dk_ref[...] = jnp.zeros_like(dk_ref)
            dv_ref[...] = jnp.zeros_like(dv_ref)

        masks = _head_masks()
        qf = q_ref[...].astype(F32) * (scale * LOG2E)
        q2 = jnp.concatenate([jnp.where(m, qf, 0.0) for m in masks], axis=0).astype(MXU_DTYPE)
        dof = do_ref[...].astype(F32)
        do2 = jnp.concatenate([jnp.where(m, dof, 0.0) for m in masks], axis=0).astype(MXU_DTYPE)
        tot2 = jnp.concatenate([t_ref[:, h:h + 1] for h in range(HEADS_PER_BLOCK)], axis=0)
        causal = lax.broadcasted_iota(jnp.int32, (tb, tb), 1) < lax.broadcasted_iota(jnp.int32, (tb, tb), 0)
        causal2 = jnp.concatenate([causal] * HEADS_PER_BLOCK, axis=0)
        neg_upper = _tri(min(tb, CUMSUM_CHUNK), lambda j, s: j >= s, -1.0)
        lower = _tri(min(tb, CUMSUM_CHUNK), lambda j, s: j <= s)
        acc_ref[...] = jnp.zeros_like(acc_ref)
        lp_ref[...] = jnp.zeros_like(lp_ref)
        dp_ref[...] = jnp.zeros_like(dp_ref)

        def block(start, width, diag):
            ks = pl.ds(pl.multiple_of(start, tb), width)
            kb = k_ref[ks, :]
            vb = v_ref[ks, :].astype(MXU_DTYPE)
            z = lax.dot_general(q2, kb.astype(MXU_DTYPE), (((1,), (1,)), ((), ())), preferred_element_type=F32)
            sp, u = _softplus2(z)
            if diag:
                sp = jnp.where(causal2, sp, 0.0)
            incl = _suffix_sums(sp, neg_upper)
            lp_next = lp_ref[...] + incl[:, 0:1]
            lp_ref[...] = lp_next
            a = z + incl + (tot2 - lp_next)
            if diag:
                a = jnp.where(causal2, a, -1e30)
            wgt = jnp.exp2(a)
            dw = lax.dot_general(do2, vb, (((1,), (1,)), ((), ())), preferred_element_type=F32)
            da = dw * wgt
            pre = _prefix_sums(da, lower, dp_ref[...])
            dp_ref[...] = pre[:, width - 1:width]
            sig = 1.0 - pl.reciprocal(u, approx=True)
            dz = da - pre * sig
            if diag:
                dz = jnp.where(causal2, dz, 0.0)
            dzb = dz.astype(MXU_DTYPE)
            dzcat = jnp.concatenate([dzb[h * tb:(h + 1) * tb] for h in range(HEADS_PER_BLOCK)], axis=1)
            kcat = jnp.concatenate([jnp.where(m, kb, jnp.zeros_like(kb)) for m in masks], axis=0).astype(MXU_DTYPE)
            acc_ref[...] += jnp.dot(dzcat, kcat, preferred_element_type=F32)
            dk_ref[ks, :] += lax.dot_general(dzb, q2, (((0,), (0,)), ((), ())), preferred_element_type=F32) * LN2
            dv_ref[ks, :] += lax.dot_general(wgt.astype(MXU_DTYPE), do2, (((0,), (0,)), ((), ())),
                                             preferred_element_type=F32)

        n_wide = qi // KEY_BLOCKS_PER_STEP

        def step(i, c):
            block(i * (KEY_BLOCKS_PER_STEP * tb), KEY_BLOCKS_PER_STEP * tb, False)
            return c

        lax.fori_loop(0, n_wide, step, 0)

        def single(i, c):
            block((n_wide * KEY_BLOCKS_PER_STEP + i) * tb, tb, False)
            return c

        lax.fori_loop(0, qi - n_wide * KEY_BLOCKS_PER_STEP, single, 0)
        block(qi * tb, tb, True)
        dq_ref[...] = acc_ref[...] * scale

    n_hp = D_MODEL // cw
    qspec = pl.BlockSpec((tb, cw), lambda h, i: (i, h))
    full = pl.BlockSpec((seq, cw), lambda h, i: (0, h))
    return pl.pallas_call(
        body, name="sba_bwd", grid=(n_hp, nq),
        in_specs=[qspec, full, full, qspec, pl.BlockSpec((None, tb, HEADS_PER_BLOCK), lambda h, i: (h, i, 0))],
        out_specs=[qspec, full, full],
        out_shape=[jax.ShapeDtypeStruct((seq, D_MODEL), F32)] * 3,
        scratch_shapes=[pltpu.VMEM((tb, cw), F32), pltpu.VMEM((HEADS_PER_BLOCK * tb, 1), F32),
                        pltpu.VMEM((HEADS_PER_BLOCK * tb, 1), F32)],
        compiler_params=_cparams(("parallel", "arbitrary")),
    )(q, k, v, do, tot)


def _my_place():
    return lax.axis_index("x"), lax.axis_index("y"), lax.axis_index("c")


def _other_chips(x, y):
    return [(1 - x, y), (x, 1 - y), (1 - x, 1 - y)]


def _all_gather_chips(slabs, *, name):
    n = len(slabs)

    def body(*refs):
        ins, outs = refs[:n], refs[n:2 * n]
        send_sems, recv_sems, loc_sems = refs[2 * n:]
        x, y, c = _my_place()
        me = 2 * x + y
        local = [pltpu.make_async_copy(ins[i], outs[i].at[me], loc_sems.at[i]) for i in range(n)]
        for cp in local:
            cp.start()
        remote = []
        for j, (px, py) in enumerate(_other_chips(x, y)):
            for i in range(n):
                remote.append(pltpu.make_async_remote_copy(
                    src_ref=ins[i], dst_ref=outs[i].at[me], send_sem=send_sems.at[j * n + i],
                    recv_sem=recv_sems.at[j * n + i], device_id=(px, py, c), device_id_type=MESH))
        for cp in remote:
            cp.start()
        for cp in remote:
            cp.wait_recv()
        for cp in remote:
            cp.wait_send()
        for cp in local:
            cp.wait()

    return pl.pallas_call(
        body, name=name, in_specs=[ANY] * n, out_specs=[ANY] * n,
        out_shape=[jax.ShapeDtypeStruct((4,) + s.shape, s.dtype) for s in slabs],
        scratch_shapes=[pltpu.SemaphoreType.DMA((3 * n,)), pltpu.SemaphoreType.DMA((3 * n,)),
                        pltpu.SemaphoreType.DMA((n,))],
    )(*slabs)


def _all_gather_chips_two_level(slabs, *, name):
    n = len(slabs)
    halves = [s.shape[0] // 2 for s in slabs]

    def body(*refs):
        ins, outs = refs[:n], refs[n:2 * n]
        ici_send, ici_recv, d2d_send, d2d_recv = refs[2 * n:]
        x, y, c = _my_place()
        chips = _other_chips(x, y)
        mine = [pl.ds(c * h, h) for h in halves]
        theirs = [pl.ds((1 - c) * h, h) for h in halves]

        def over_ici(j, i, chip):
            px, py = chips[j]
            slot = 2 * chip[0] + chip[1]
            return pltpu.make_async_remote_copy(
                src_ref=ins[i].at[mine[i]], dst_ref=outs[i].at[slot, mine[i]], send_sem=ici_send.at[j * n + i],
                recv_sem=ici_recv.at[j * n + i], device_id=(px, py, c), device_id_type=MESH)

        def over_d2d(j, i, rows):
            px, py = chips[j]
            block = outs[i].at[2 * px + py, rows]
            return pltpu.make_async_remote_copy(
                src_ref=block, dst_ref=block, send_sem=d2d_send.at[j * n + i], recv_sem=d2d_recv.at[j * n + i],
                device_id=(x, y, 1 - c), device_id_type=MESH)

        sent = [over_ici(j, i, (x, y)) for j in range(3) for i in range(n)]
        for cp in sent:
            cp.start()
        passed = []
        for j in range(3):
            for i in range(n):
                over_ici(j, i, chips[j]).wait_recv()
                cp = over_d2d(j, i, mine[i])
                cp.start()
                passed.append(cp)
        for j in range(3):
            for i in range(n):
                over_d2d(j, i, theirs[i]).wait_recv()
        for cp in sent + passed:
            cp.wait_send()

    outs = pl.pallas_call(
        body, name=name, in_specs=[ANY] * n, out_specs=[ANY] * n,
        out_shape=[jax.ShapeDtypeStruct((4,) + s.shape, s.dtype) for s in slabs],
        scratch_shapes=[pltpu.SemaphoreType.DMA((3 * n,)), pltpu.SemaphoreType.DMA((3 * n,)),
                        pltpu.SemaphoreType.DMA((3 * n,)), pltpu.SemaphoreType.DMA((3 * n,))],
    )(*slabs)
    me = 2 * lax.axis_index("x") + lax.axis_index("y")
    return [[jnp.where(me == j, s, o[j]) for j in range(4)] for o, s in zip(outs, slabs)]


def _sibling_exchange(send, *, name):
    def body(s_ref, o_ref, send_sem, recv_sem):
        x, y, c = _my_place()
        cp = pltpu.make_async_remote_copy(src_ref=s_ref, dst_ref=o_ref, send_sem=send_sem, recv_sem=recv_sem,
                                          device_id=(x, y, 1 - c), device_id_type=MESH)
        cp.start()
        cp.wait_recv()
        cp.wait_send()

    return pl.pallas_call(
        body, name=name, in_specs=[ANY], out_specs=ANY,
        out_shape=jax.ShapeDtypeStruct(send.shape, send.dtype),
        scratch_shapes=[pltpu.SemaphoreType.DMA, pltpu.SemaphoreType.DMA],
    )(send)


def _all_to_all_chips(parts, *, name):
    def body(p_ref, o_ref, send_sems, recv_sems):
        x, y, c = _my_place()
        me = 2 * x + y
        remote = []
        for j, (px, py) in enumerate(_other_chips(x, y)):
            remote.append(pltpu.make_async_remote_copy(
                src_ref=p_ref.at[2 * px + py], dst_ref=o_ref.at[me], send_sem=send_sems.at[j],
                recv_sem=recv_sems.at[j], device_id=(px, py, c), device_id_type=MESH))
        for cp in remote:
            cp.start()
        for cp in remote:
            cp.wait_recv()
        for cp in remote:
            cp.wait_send()

    out = pl.pallas_call(
        body, name=name, in_specs=[ANY], out_specs=ANY,
        out_shape=jax.ShapeDtypeStruct(parts.shape, parts.dtype),
        scratch_shapes=[pltpu.SemaphoreType.DMA((3,)), pltpu.SemaphoreType.DMA((3,))],
    )(parts)
    me = 2 * lax.axis_index("x") + lax.axis_index("y")
    return [jnp.where(me == j, parts[j], out[j]) for j in range(4)]


_SHARDED = [("a_norm_pre", 1), ("a_norm_post", 1), ("a_d_skip", 1), ("a_b_glu", 1), ("a_w_in", 2), ("a_w_glu", 1),
            ("a_w_out", 1), ("w_kv", 1), ("b_w_in", 2), ("b_w_out", 1), ("ple_w_proj", 2), ("ple_w_gate", 1)]
_REPLICATED = ["a_lam_re", "a_lam_im", "a_log_dt", "a_b_re", "a_b_im", "a_c_re", "a_c_im", "kv_norm", "b_norm_pre",
               "b_norm_post"]
_WEIGHT_ORDER = ["a_norm_pre", "a_norm_post", "a_w_in", "a_lam_re", "a_lam_im", "a_log_dt", "a_b_re", "a_b_im",
                 "a_c_re", "a_c_im", "a_d_skip", "a_w_glu", "a_b_glu", "a_w_out", "kv_norm", "w_kv", "b_norm_pre",
                 "b_norm_post", "b_w_in", "b_w_out", "ple_w_proj", "ple_w_gate"]


def _round_up(n, m):
    return (n + m - 1) // m * m


def _pack(pieces, rows):
    flat = jnp.concatenate([p.reshape(-1).astype(F32) for p in pieces])
    flat = jnp.pad(flat, (0, rows * SLAB_W - flat.shape[0]))
    return flat.reshape(rows, SLAB_W)


def _unpack(slab, shapes):
    flat = slab.reshape(-1)
    out, off = [], 0
    for s in shapes:
        n = math.prod(s)
        out.append(flat[off:off + n].reshape(s))
        off += n
    return out


def _pick_tile(n, cap):
    return max(t for t in range(SUBLANES, cap + 1, SUBLANES) if n % t == 0)


def _size_rows(shapes, mult):
    return _round_up(_round_up(sum(math.prod(s) for s in shapes), SLAB_W) // SLAB_W, mult)


def _adamw_tile(w, g, m, v):
    m2 = ADAM_B1 * m + (1.0 - ADAM_B1) * g
    v2 = ADAM_B2 * v + (1.0 - ADAM_B2) * (g * g)
    m_hat = m2 / (1.0 - ADAM_B1 ** ADAM_STEP)
    v_hat = v2 / (1.0 - ADAM_B2 ** ADAM_STEP)
    delta = -ADAM_LR * (m_hat / (jnp.sqrt(v_hat) + ADAM_EPS) + ADAM_WD * w)
    return delta, m2, v2


def _forward_backward(x0, p0, p1, tgt, W):
    seq = x0.shape[0]
    sw = D_MODEL
    g = {}

    h1, = _ew(lambda xt, gn: (_rms(xt, gn),), [x0], [W["a_norm_pre"]], [MXU_DTYPE], 0, name="a_pre_norm")
    w_u, w_ga = W["a_w_in"][:, :sw], W["a_w_in"][:, sw:]
    u = _mm(h1, w_u, name="a_in_u")
    gate = _mm(h1, w_ga, out_dtype=MXU_DTYPE, name="a_in_gate")
    a_re, a_im, bbt_re, bbt_im, pw_re, pw_im = _s5_params_fwd(W["lam_re"], W["lam_im"], W["log_dt"], W["bt_re"],
                                                               W["bt_im"])
    nb, gpb = N_GROUP_BLOCKS, GROUPS_PER_BLOCK
    bbd = jnp.concatenate([_block_diag(bbt_re.reshape(nb, gpb, GROUP_SIZE, STATE)),
                           _block_diag(bbt_im.reshape(nb, gpb, GROUP_SIZE, STATE))], axis=2).astype(MXU_DTYPE)
    crt = W["c_re"].transpose(0, 2, 1).reshape(nb, gpb, STATE, GROUP_SIZE)
    cit = W["c_im"].transpose(0, 2, 1).reshape(nb, gpb, STATE, GROUP_SIZE)
    cbd = jnp.concatenate([_block_diag(crt), -_block_diag(cit)], axis=1).astype(MXU_DTYPE)
    cbd_t = cbd.transpose(0, 2, 1)
    tab_f, tab_b = _scan_tables(pw_re.reshape(SUBLANES, nb, SSM_LANES), pw_im.reshape(SUBLANES, nb, SSM_LANES))
    ys, hs = _s5_fwd(u, bbd, cbd, tab_f)
    gl, = _ew(lambda a, b, d: (_gelu_skip(a, b, d),), [ys, u], [W["a_d_skip"]], [MXU_DTYPE], 0, name="a_gelu")
    t = _mm(gl, W["a_w_glu"], out_dtype=MXU_DTYPE, name="a_glu")
    y3, = _ew(lambda a, b, c, d: (_glu_gate(a, b, c, d),), [gl, t, gate], [W["a_b_glu"]], [MXU_DTYPE], 0, name="a_gate")
    y4 = _mm(y3, W["a_w_out"], out_dtype=MXU_DTYPE, name="a_out")
    x1, = _ew(lambda a, b, gn: (a + _rms(b, gn),), [x0, y4], [W["a_norm_post"]], [F32], 0, name="a_post_norm")
    gt0 = _mm(x1, W["ple_w_gate"][0], out_dtype=MXU_DTYPE, name="ple0_gate")
    pp0 = _mm(p0, W["ple_w_proj"][0], out_dtype=MXU_DTYPE, name="ple0_proj")
    x2, hk, h2 = _ew(lambda a, b, c, g1, g2: (_ple(a, b, c), _rms(_ple(a, b, c), g1), _rms(_ple(a, b, c), g2)),
                     [x1, gt0, pp0], [W["kv_norm"], W["b_norm_pre"]], [F32, MXU_DTYPE, MXU_DTYPE], 0, name="ple0_mix")

    w_k, w_v = W["w_kv"][:, :sw], W["w_kv"][:, sw:]
    k = _mm(hk, w_k, out_dtype=MXU_DTYPE, name="kv_k")
    v = _mm(hk, w_v, out_dtype=MXU_DTYPE, name="kv_v")
    w_q, w_gb = W["b_w_in"][:, :sw], W["b_w_in"][:, sw:]
    q = _mm(h2, w_q, out_dtype=MXU_DTYPE, name="b_in_q")
    gate2 = _mm(h2, w_gb, out_dtype=MXU_DTYPE, name="b_in_gate")
    o, tot = _sba_fwd(q, k, v)
    y5in, = _ew(lambda a, b: (_ogate(a, b),), [o, gate2], [], [MXU_DTYPE], 0, name="b_gate")
    y5 = _mm(y5in, W["b_w_out"], out_dtype=MXU_DTYPE, name="b_out")
    x3, = _ew(lambda a, b, gn: (a + _rms(b, gn),), [x2, y5], [W["b_norm_post"]], [F32], 0, name="b_post_norm")
    gt1 = _mm(x3, W["ple_w_gate"][1], out_dtype=MXU_DTYPE, name="ple1_gate")
    pp1 = _mm(p1, W["ple_w_proj"][1], out_dtype=MXU_DTYPE, name="ple1_proj")

    def loss_fn(xt, gt, pp, tg):
        s = jax.nn.sigmoid(gt)
        d = (xt + s * pp - tg) * (1.0 / D_MODEL)
        lsum = jnp.sum(d * d, axis=(0, 1), keepdims=True) * (0.5 * D_MODEL)
        return d, d * pp * s * (1.0 - s), d * s, lsum

    dx3a, dgt1, dpp1, loss = _ew(loss_fn, [x3, gt1, pp1, tgt], [], [F32, MXU_DTYPE, MXU_DTYPE], 1, name="loss_head")
    g_gate1 = _mm(x3, dgt1, ta=True, name="ple1_gate_dw")
    g_proj1 = _mm(p1, dpp1, ta=True, name="ple1_proj_dw")
    dx3 = _mm(dgt1, W["ple_w_gate"][1], tb=True, add=dx3a, name="ple1_gate_dx")

    def post_norm_bwd(ct, yt, gn):
        _, vjp = jax.vjp(_rms, yt, gn)
        dy, dg = vjp(ct)
        return dy, dg

    dy5, g["b_norm_post"] = _ew(post_norm_bwd, [dx3, y5], [W["b_norm_post"]], [MXU_DTYPE], 1, name="b_post_norm_bwd")
    g["b_w_out"] = _mm(y5in, dy5, ta=True, name="b_out_dw")
    dy5in = _mm(dy5, W["b_w_out"], tb=True, out_dtype=MXU_DTYPE, name="b_out_dx")

    def ogate_bwd(ct, ot, gt):
        _, vjp = jax.vjp(_ogate, ot, gt)
        return vjp(ct)

    do, dgate2 = _ew(ogate_bwd, [dy5in, o, gate2], [], [MXU_DTYPE, MXU_DTYPE], 0, name="b_gate_bwd")
    dq, dk, dv = _sba_bwd(q, k, v, do, tot)
    g["b_w_in"] = jnp.concatenate([_mm(h2, dq, ta=True, name="b_in_q_dw"),
                                   _mm(h2, dgate2, ta=True, name="b_in_gate_dw")], axis=1)
    dh2 = _mm(dgate2, w_gb, tb=True, add=_mm(dq, w_q, tb=True, name="b_in_q_dx"), out_dtype=MXU_DTYPE,
              name="b_in_gate_dx")
    g["w_kv"] = jnp.concatenate([_mm(hk, dk, ta=True, name="kv_k_dw"), _mm(hk, dv, ta=True, name="kv_v_dw")], axis=1)
    dhk = _mm(dv, w_v, tb=True, add=_mm(dk, w_k, tb=True, name="kv_k_dx"), out_dtype=MXU_DTYPE, name="kv_v_dx")

    def mix_bwd(ct, c2, ck, xt, gt, pp, g1, g2):
        x2v, vjp_ple = jax.vjp(_ple, xt, gt, pp)
        _, vjp_k = jax.vjp(_rms, x2v, g1)
        _, vjp_b = jax.vjp(_rms, x2v, g2)
        dxk, dg1 = vjp_k(ck)
        dxb, dg2 = vjp_b(c2)
        dx2 = ct + dxk + dxb
        dx1, dgt, dpp = vjp_ple(dx2)
        return dx1, dgt, dpp, dg1, dg2

    dx1a, dgt0, dpp0, g["kv_norm"], g["b_norm_pre"] = _ew(
        mix_bwd, [dx3, dh2, dhk, x1, gt0, pp0], [W["kv_norm"], W["b_norm_pre"]], [F32, MXU_DTYPE, MXU_DTYPE], 2,
        name="ple0_mix_bwd")
    g_gate0 = _mm(x1, dgt0, ta=True, name="ple0_gate_dw")
    g_proj0 = _mm(p0, dpp0, ta=True, name="ple0_proj_dw")
    dx1 = _mm(dgt0, W["ple_w_gate"][0], tb=True, add=dx1a, name="ple0_gate_dx")
    g["ple_w_gate"] = jnp.stack([g_gate0, g_gate1])
    g["ple_w_proj"] = jnp.stack([g_proj0, g_proj1])

    dy4, g["a_norm_post"] = _ew(post_norm_bwd, [dx1, y4], [W["a_norm_post"]], [MXU_DTYPE], 1, name="a_post_norm_bwd")
    g["a_w_out"] = _mm(y3, dy4, ta=True, name="a_out_dw")
    dy3 = _mm(dy4, W["a_w_out"], tb=True, out_dtype=MXU_DTYPE, name="a_out_dx")

    def gate_bwd(ct, gt_, tt_, gat, bg):
        _, vjp = jax.vjp(_glu_gate, gt_, tt_, gat, bg)
        dg_, dt_, dgate_, dbg = vjp(ct)
        return dg_, dt_, dgate_, dbg

    dgl_a, dt, dgate, g["a_b_glu"] = _ew(gate_bwd, [dy3, gl, t, gate], [W["a_b_glu"]], [F32, MXU_DTYPE, MXU_DTYPE], 1,
                                         name="a_gate_bwd")
    g["a_w_glu"] = _mm(gl, dt, ta=True, name="a_glu_dw")
    dgl = _mm(dt, W["a_w_glu"], tb=True, add=dgl_a, name="a_glu_dx")

    def gelu_bwd(ct, yt, ut, ds):
        _, vjp = jax.vjp(_gelu_skip, yt, ut, ds)
        return vjp(ct)

    dys, du_a, g["a_d_skip"] = _ew(gelu_bwd, [dgl, ys, u], [W["a_d_skip"]], [MXU_DTYPE, F32], 1, name="a_gelu_bwd")
    du, d_bbd, d_cbd, d_ar8, d_ai8 = _s5_bwd(u, dys, du_a, hs, bbd, cbd_t, tab_f, tab_b)
    g["a_w_in"] = jnp.concatenate([_mm(h1, du, ta=True, name="a_in_u_dw"),
                                   _mm(h1, dgate, ta=True, name="a_in_gate_dw")], axis=1)
    dh1 = _mm(dgate, w_ga, tb=True, add=_mm(du, w_u, tb=True, name="a_in_u_dx"), out_dtype=MXU_DTYPE,
              name="a_in_gate_dx")

    def pre_norm_bwd(ct, ch, xt, gn):
        _, vjp = jax.vjp(_rms, xt, gn)
        dx, dg = vjp(ch)
        return ct + dx, dg

    dx0, g["a_norm_pre"] = _ew(pre_norm_bwd, [dx1, dh1, x0], [W["a_norm_pre"]], [F32], 1, name="a_pre_norm_bwd")

    w = SSM_LANES
    d_bbt_re = _block_diag_extract(d_bbd[:, :, :w], GROUP_SIZE, STATE).reshape(N_GROUPS, GROUP_SIZE, STATE)
    d_bbt_im = _block_diag_extract(d_bbd[:, :, w:], GROUP_SIZE, STATE).reshape(N_GROUPS, GROUP_SIZE, STATE)
    d_crt = _block_diag_extract(d_cbd[:, :w, :], STATE, GROUP_SIZE).reshape(N_GROUPS, STATE, GROUP_SIZE)
    d_cit = -_block_diag_extract(d_cbd[:, w:, :], STATE, GROUP_SIZE).reshape(N_GROUPS, STATE, GROUP_SIZE)
    g["a_c_re"] = d_crt.transpose(0, 2, 1)[None]
    g["a_c_im"] = d_cit.transpose(0, 2, 1)[None]
    d_lr, d_li, d_ldt, d_btr, d_bti = _s5_params_bwd(
        W["lam_re"], W["lam_im"], W["log_dt"], W["bt_re"], W["bt_im"], _per_sublane(d_ar8), _per_sublane(d_ai8),
        d_bbt_re, d_bbt_im)
    g["a_lam_re"], g["a_lam_im"] = d_lr[None], d_li[None]
    g["a_log_dt"] = d_ldt.reshape(1, N_GROUPS)
    g["a_b_re"] = d_btr.transpose(0, 2, 1)[None]
    g["a_b_im"] = d_bti.transpose(0, 2, 1)[None]
    return loss, dx0, g


def kernel(x, p, a_norm_pre, a_norm_post, a_w_in, a_lam_re, a_lam_im, a_log_dt, a_b_re, a_b_im, a_c_re, a_c_im, a_d_skip, a_w_glu, a_b_glu, a_w_out, kv_norm, w_kv, b_norm_pre, b_norm_post, b_w_in, b_w_out, ple_w_proj, ple_w_gate, loss_target, m_a_norm_pre, m_a_norm_post, m_a_w_in, m_a_lam_re, m_a_lam_im, m_a_log_dt, m_a_b_re, m_a_b_im, m_a_c_re, m_a_c_im, m_a_d_skip, m_a_w_glu, m_a_b_glu, m_a_w_out, m_kv_norm, m_w_kv, m_b_norm_pre, m_b_norm_post, m_b_w_in, m_b_w_out, m_ple_w_proj, m_ple_w_gate, v_a_norm_pre, v_a_norm_post, v_a_w_in, v_a_lam_re, v_a_lam_im, v_a_log_dt, v_a_b_re, v_a_b_im, v_a_c_re, v_a_c_im, v_a_d_skip, v_a_w_glu, v_a_b_glu, v_a_w_out, v_kv_norm, v_w_kv, v_b_norm_pre, v_b_norm_post, v_b_w_in, v_b_w_out, v_ple_w_proj, v_ple_w_gate):
    loc = dict(a_norm_pre=a_norm_pre, a_norm_post=a_norm_post, a_w_in=a_w_in, a_lam_re=a_lam_re, a_lam_im=a_lam_im,
               a_log_dt=a_log_dt, a_b_re=a_b_re, a_b_im=a_b_im, a_c_re=a_c_re, a_c_im=a_c_im, a_d_skip=a_d_skip,
               a_w_glu=a_w_glu, a_b_glu=a_b_glu, a_w_out=a_w_out, kv_norm=kv_norm, w_kv=w_kv, b_norm_pre=b_norm_pre,
               b_norm_post=b_norm_post, b_w_in=b_w_in, b_w_out=b_w_out, ple_w_proj=ple_w_proj, ple_w_gate=ple_w_gate)
    mom = dict(a_norm_pre=m_a_norm_pre, a_norm_post=m_a_norm_post, a_w_in=m_a_w_in, a_lam_re=m_a_lam_re,
               a_lam_im=m_a_lam_im, a_log_dt=m_a_log_dt, a_b_re=m_a_b_re, a_b_im=m_a_b_im, a_c_re=m_a_c_re,
               a_c_im=m_a_c_im, a_d_skip=m_a_d_skip, a_w_glu=m_a_w_glu, a_b_glu=m_a_b_glu, a_w_out=m_a_w_out,
               kv_norm=m_kv_norm, w_kv=m_w_kv, b_norm_pre=m_b_norm_pre, b_norm_post=m_b_norm_post, b_w_in=m_b_w_in,
               b_w_out=m_b_w_out, ple_w_proj=m_ple_w_proj, ple_w_gate=m_ple_w_gate)
    var = dict(a_norm_pre=v_a_norm_pre, a_norm_post=v_a_norm_post, a_w_in=v_a_w_in, a_lam_re=v_a_lam_re,
               a_lam_im=v_a_lam_im, a_log_dt=v_a_log_dt, a_b_re=v_a_b_re, a_b_im=v_a_b_im, a_c_re=v_a_c_re,
               a_c_im=v_a_c_im, a_d_skip=v_a_d_skip, a_w_glu=v_a_w_glu, a_b_glu=v_a_b_glu, a_w_out=v_a_w_out,
               kv_norm=v_kv_norm, w_kv=v_w_kv, b_norm_pre=v_b_norm_pre, b_norm_post=v_b_norm_post, b_w_in=v_b_w_in,
               b_w_out=v_b_w_out, ple_w_proj=v_ple_w_proj, ple_w_gate=v_ple_w_gate)
    vec_names = [n for n, _ in _SHARDED if loc[n].shape[-2] == 1 and loc[n].ndim == 2]
    mat_names = [n for n, _ in _SHARDED if n not in vec_names]
    axis_of = dict(_SHARDED)
    mat_shapes = [loc[n].shape for n in mat_names]
    vec_shapes = [loc[n].shape for n in vec_names]
    mat_rows = _size_rows(mat_shapes, 32)
    vec_rows = _size_rows(vec_shapes, 16)
    mat_slab = _pack([loc[n] for n in mat_names], mat_rows).astype(MXU_DTYPE)
    vec_slab = _pack([loc[n] for n in vec_names], vec_rows)
    mat_all, vec_all = _all_gather_chips_two_level([mat_slab, vec_slab], name="gather_weights")
    W = {}
    for names, slab_all, shapes in ((mat_names, mat_all, mat_shapes), (vec_names, vec_all, vec_shapes)):
        per_chip = [_unpack(slab_all[j], shapes) for j in range(4)]
        for i, n in enumerate(names):
            W[n] = jnp.concatenate([per_chip[j][i] for j in range(4)], axis=axis_of[n])
    for n in ("a_w_in", "a_w_glu", "a_w_out", "b_w_in", "b_w_out"):
        W[n] = W[n][0]
    W["kv_norm"] = kv_norm.reshape(1, D_MODEL)
    W["b_norm_pre"], W["b_norm_post"] = b_norm_pre, b_norm_post
    W["lam_re"], W["lam_im"] = a_lam_re[0], a_lam_im[0]
    W["log_dt"] = a_log_dt.reshape(N_GROUPS, 1)
    W["bt_re"], W["bt_im"] = a_b_re[0].transpose(0, 2, 1), a_b_im[0].transpose(0, 2, 1)
    W["c_re"], W["c_im"] = a_c_re[0], a_c_im[0]

    loss, dx0, g = _forward_backward(x[0], p[0, 0], p[1, 0], loss_target[0], W)
    for n in ("a_w_in", "a_w_glu", "a_w_out", "b_w_in", "b_w_out"):
        g[n] = g[n][None]
    g["kv_norm"] = g["kv_norm"].reshape(D_MODEL)
    loss = lax.psum(loss[0, 0], ("x", "y", "c"))

    rep_shapes = [loc[n].shape for n in _REPLICATED]
    rep_rows = _size_rows(rep_shapes, 32)
    rep_slab = _pack([g[n] for n in _REPLICATED], rep_rows).reshape(4, rep_rows // 4, SLAB_W)
    sh_names = [n for n, _ in _SHARDED]
    sh_shapes = [loc[n].shape for n in sh_names]
    sh_rows = _size_rows(sh_shapes, 8)
    rows = _round_up(sh_rows + rep_rows // 4, 1024)
    half = rows // 2
    slabs = []
    for j in range(4):
        pieces = [jnp.split(g[n], 4, axis=axis_of[n])[j] for n in sh_names]
        sh = _pack(pieces, sh_rows)
        slabs.append(jnp.concatenate([sh, rep_slab[j], jnp.zeros((rows - sh_rows - rep_rows // 4, SLAB_W), F32)]))
    gs = jnp.stack(slabs).reshape(4, 2, half, SLAB_W)
    ci = lax.axis_index("c")
    mine = lax.dynamic_index_in_dim(gs, ci, axis=1, keepdims=False).reshape(4 * half, SLAB_W)
    theirs = lax.dynamic_index_in_dim(gs, 1 - ci, axis=1, keepdims=False).reshape(4 * half, SLAB_W)
    got = _sibling_exchange(theirs, name="reduce_cores")
    part, = _ew(lambda a, b: (a + b,), [mine, got], [], [MXU_DTYPE], 0, name="reduce_cores_add", tr=512)
    recv = _all_to_all_chips(part.reshape(4, half, SLAB_W), name="reduce_chips")
    tot_half, = _ew(lambda a, b, c, d: (((a + b) + c) + d,), [recv[0], recv[1], recv[2], recv[3]], [], [F32], 0,
                    name="reduce_chips_add", tr=512)
    other_half = _sibling_exchange(tot_half, name="share_cores")
    lo_half = jnp.where(ci == 0, tot_half, other_half)
    hi_half = jnp.where(ci == 0, other_half, tot_half)
    gsum = jnp.concatenate([lo_half, hi_half])
    rep_q = gsum[sh_rows:sh_rows + rep_rows // 4]
    rep_all, = _all_gather_chips([rep_q], name="gather_replicated")
    gsum_full = jnp.concatenate([gsum[:sh_rows], rep_all.reshape(rep_rows, SLAB_W)])

    all_names = sh_names + _REPLICATED
    all_shapes = sh_shapes + rep_shapes

    def slab_of(d):
        return jnp.concatenate([_pack([d[n] for n in sh_names], sh_rows), _pack([d[n] for n in _REPLICATED], rep_rows)])

    n_rows = sh_rows + rep_rows
    outs = _ew(_adamw_tile, [slab_of(loc), gsum_full, slab_of(mom), slab_of(var)], [], [F32] * 3, 0, name="adamw",
               tr=_pick_tile(n_rows, 512))
    res = []
    for slab in [gsum_full] + list(outs):
        parts = _unpack(slab[:sh_rows], sh_shapes) + _unpack(slab[sh_rows:], rep_shapes)
        res.append(dict(zip(all_names, parts)))
    out = [loss, dx0[None]]
    for d in res:
        out += [d[n] for n in _WEIGHT_ORDER]
    return tuple(out)
```
